```python
import jax
import jax.numpy as jnp
from jax import lax
import numpy as np

D_MODEL = 2048
BATCH = 4
SEQ = 2048
DEPTH = 2

GRID_W = 64
CTX_LEN = 256
N_GROUPS = 4
GROUP_W = D_MODEL // N_GROUPS
HEAD_DIM = 128
EPS = 1e-6

GLA_HEADS = GROUP_W // HEAD_DIM
GLA_DV = HEAD_DIM
GLA_DK = HEAD_DIM // 2
GLA_RANK = 16
GLA_TAU = 16.0
GLA_CHUNK = 32

GDN_HEADS = GROUP_W // HEAD_DIM
GDN_DK = HEAD_DIM
GDN_DV = HEAD_DIM
GDN_CONV = 5
GDN_CHUNK = 64

SC_WIDTH = GROUP_W
SC_CONV = 3

SWA_HEADS = GROUP_W // HEAD_DIM
SWA_KV_HEADS = 2
SWA_WINDOW = 128
SWA_BLOCK = 128
ROPE_BASE = 10000.0

IN_SPLITS = (
    ("gla_q", GLA_HEADS * GLA_DK), ("gla_k", GLA_HEADS * GLA_DK), ("gla_v", GLA_HEADS * GLA_DV),
    ("gla_lr_f", GLA_RANK), ("gla_lr_b", GLA_RANK), ("gla_gate", GROUP_W),
    ("gdn_q", GDN_HEADS * GDN_DK), ("gdn_k", GDN_HEADS * GDN_DK), ("gdn_v", GDN_HEADS * GDN_DV),
    ("gdn_a_f", GDN_HEADS), ("gdn_a_b", GDN_HEADS), ("gdn_b_f", GDN_HEADS), ("gdn_b_b", GDN_HEADS),
    ("gdn_gate", GROUP_W),
    ("sc_b", SC_WIDTH), ("sc_c", SC_WIDTH), ("sc_h", SC_WIDTH), ("sc_gate", GROUP_W),
    ("swa_q", SWA_HEADS * HEAD_DIM), ("swa_k", SWA_KV_HEADS * HEAD_DIM), ("swa_v", SWA_KV_HEADS * HEAD_DIM),
    ("swa_gate", GROUP_W),
)
IN_WIDTH = sum(size for _, size in IN_SPLITS)

kernel_name = "hybrid_parallel_group_flow_block"


def split_cols(u):
    out, off = {}, 0
    for name, size in IN_SPLITS:
        out[name] = u[..., off:off + size]
        off += size
    return out


def rmsnorm(x, g):
    xf = x.astype(jnp.float32)
    y = xf * lax.rsqrt(jnp.mean(xf * xf, axis=-1, keepdims=True) + EPS)
    return (y * g.astype(jnp.float32)).astype(x.dtype)


def l2norm(x):
    xf = x.astype(jnp.float32)
    return (xf * lax.rsqrt(jnp.sum(xf * xf, axis=-1, keepdims=True) + EPS)).astype(x.dtype)


def to_heads(t, n):
    b, l, _ = t.shape
    return t.reshape(b, l, n, -1).transpose(0, 2, 1, 3)


def from_heads(t):
    b, h, l, d = t.shape
    return t.transpose(0, 2, 1, 3).reshape(b, l, h * d)


def flip_seq(t):
    return jnp.flip(t, axis=2)


def dwconv(x, w):
    width = w.shape[0]
    return lax.conv_general_dilated(
        x, w[:, None, :].astype(x.dtype), window_strides=(1,),
        padding=[(width // 2, width // 2)], dimension_numbers=("NWC", "WIO", "NWC"),
        feature_group_count=x.shape[-1])


def axial_rope_tables(seq_len):
    rows = seq_len // GRID_W
    row = jnp.repeat(jnp.arange(rows, dtype=jnp.float32), GRID_W)
    col = jnp.tile(jnp.arange(GRID_W, dtype=jnp.float32), rows)
    axis_dim = HEAD_DIM // 2
    inv_freq = ROPE_BASE ** (-jnp.arange(0, axis_dim, 2, dtype=jnp.float32) / axis_dim)
    ang_r = row[:, None] * inv_freq
    ang_c = col[:, None] * inv_freq
    return (jnp.cos(ang_r), jnp.sin(ang_r), jnp.cos(ang_c), jnp.sin(ang_c))


def rotate_half(x, cos, sin):
    x1, x2 = jnp.split(x, 2, axis=-1)
    cos = cos[None, :, None, :].astype(x.dtype)
    sin = sin[None, :, None, :].astype(x.dtype)
    return jnp.concatenate([x1 * cos - x2 * sin, x2 * cos + x1 * sin], axis=-1)


def rope2d(x, tabs):
    cos_r, sin_r, cos_c, sin_c = tabs
    half = HEAD_DIM // 2
    return jnp.concatenate([rotate_half(x[..., :half], cos_r, sin_r),
                            rotate_half(x[..., half:], cos_c, sin_c)], axis=-1)


def gla_scan(q, k, v, log_a, s0):
    out_dtype = v.dtype
    q, k, v, log_a = (t.astype(jnp.float32) for t in (q, k, v, log_a))
    b, h, l, dk = q.shape
    dv = v.shape[-1]
    n = l // GLA_CHUNK
    q, k, log_a = (t.reshape(b, h, n, GLA_CHUNK, dk) for t in (q, k, log_a))
    v = v.reshape(b, h, n, GLA_CHUNK, dv)
    cum = jnp.cumsum(log_a, axis=3)
    incl = jnp.tril(jnp.ones((GLA_CHUNK, GLA_CHUNK), bool))[:, :, None]
    pair = jnp.exp(jnp.where(incl, cum[..., :, None, :] - cum[..., None, :, :], -jnp.inf))
    attn = jnp.einsum("bhnid,bhnjd,bhnijd->bhnij", q, k, pair)
    o_intra = jnp.einsum("bhnij,bhnjv->bhniv", attn, v)
    q_in = q * jnp.exp(cum)
    k_out = k * jnp.exp(cum[..., -1:, :] - cum)
    chunk_decay = jnp.exp(cum[..., -1, :])

    def step(state, inp):
        q_n, k_n, v_n, d_n = inp
        o = jnp.einsum("bhcd,bhdv->bhcv", q_n, state)
        state = state * d_n[..., None] + jnp.einsum("bhcd,bhcv->bhdv", k_n, v_n)
        return state, o

    xs = tuple(jnp.moveaxis(t, 2, 0) for t in (q_in, k_out, v, chunk_decay))
    state, o_inter = lax.scan(step, s0, xs)
    o = jnp.moveaxis(o_inter, 0, 2) + o_intra
    return o.reshape(b, h, l, dv).astype(out_dtype), state


def gla_mixer(s, sc, w_decay, b_decay, norm_w, with_ctx):
    def prep(seg):
        q = to_heads(seg["gla_q"], GLA_HEADS) * (GLA_DK ** -0.5)
        k = to_heads(seg["gla_k"], GLA_HEADS)
        v = to_heads(seg["gla_v"], GLA_HEADS)

        def log_decay(lr, d):
            z = (lr @ w_decay[d] + b_decay[d]).astype(jnp.float32)
            return to_heads(jax.nn.log_sigmoid(z) / GLA_TAU, GLA_HEADS)
        return q, k, v, log_decay(seg["gla_lr_f"], 0), log_decay(seg["gla_lr_b"], 1)

    q, k, v, la_f, la_b = prep(s)
    qc, kc, vc, lc_f, lc_b = prep(sc)
    s0 = jnp.zeros((q.shape[0], GLA_HEADS, GLA_DK, GLA_DV), jnp.float32)
    oc_f, st_f = gla_scan(qc, kc, vc, lc_f, s0)
    oc_b, st_b = gla_scan(flip_seq(qc), flip_seq(kc), flip_seq(vc), flip_seq(lc_b), s0)
    o_f, _ = gla_scan(q, k, v, la_f, st_f)
    o_b, _ = gla_scan(flip_seq(q), flip_seq(k), flip_seq(v), flip_seq(la_b), st_b)

    def finish(o, seg):
        return from_heads(rmsnorm(o, norm_w)) * jax.nn.silu(seg["gla_gate"])
    y = finish(o_f + flip_seq(o_b), s)
    yc = finish(oc_f + flip_seq(oc_b), sc) if with_ctx else None
    return y, yc


def gdn_scan(q, k, v, g, beta, s0):
    out_dtype = v.dtype
    q, k, v, g, beta = (t.astype(jnp.float32) for t in (q, k, v, g, beta))
    b, h, l, dk = q.shape
    dv = v.shape[-1]
    n = l // GDN_CHUNK
    q, k = (t.reshape(b, h, n, GDN_CHUNK, dk) for t in (q, k))
    v = v.reshape(b, h, n, GDN_CHUNK, dv)
    g, beta = (t.reshape(b, h, n, GDN_CHUNK) for t in (g, beta))
    cum = jnp.cumsum(g, axis=-1)
    incl = jnp.tril(jnp.ones((GDN_CHUNK, GDN_CHUNK), bool))
    strict = jnp.tril(jnp.ones((GDN_CHUNK, GDN_CHUNK), bool), -1)
    decay = jnp.exp(jnp.where(incl, cum[..., :, None] - cum[..., None, :], -jnp.inf))
    k_beta = k * beta[..., None]
    lower = jnp.where(strict, jnp.einsum("bhnid,bhnjd->bhnij", k_beta, k) * decay, 0.0)
    a_mat = lower + jnp.eye(GDN_CHUNK, dtype=jnp.float32)
    u = lax.linalg.triangular_solve(a_mat, v * beta[..., None], left_side=True, lower=True)
    w = lax.linalg.triangular_solve(a_mat, k_beta * jnp.exp(cum)[..., None], left_side=True, lower=True)
    attn = jnp.einsum("bhnid,bhnjd->bhnij", q, k) * decay
    q_in = q * jnp.exp(cum)[..., None]
    k_out = k * jnp.exp(cum[..., -1:] - cum)[..., None]
    chunk_decay = jnp.exp(cum[..., -1])

    def step(state, inp):
        u_n, w_n, q_n, k_n, a_n, d_n = inp
        v_new = u_n - jnp.einsum("bhcd,bhdv->bhcv", w_n, state)
        o = jnp.einsum("bhcd,bhdv->bhcv", q_n, state) + jnp.einsum("bhij,bhjv->bhiv", a_n, v_new)
        state = state * d_n[..., None, None] + jnp.einsum("bhcd,bhcv->bhdv", k_n, v_new)
        return state, o

    xs = tuple(jnp.moveaxis(t, 2, 0) for t in (u, w, q_in, k_out, attn, chunk_decay))
    state, o = lax.scan(step, s0, xs)
    o = jnp.moveaxis(o, 0, 2).reshape(b, h, l, dv)
    return o.astype(out_dtype), state


def gdn_mixer(s, sc, conv_w, a_log, dt_bias, norm_w, with_ctx):
    qd = GDN_HEADS * GDN_DK

    def prep(seg):
        qkv = jnp.concatenate([seg["gdn_q"], seg["gdn_k"], seg["gdn_v"]], axis=-1)
        qkv = jax.nn.silu(dwconv(qkv, conv_w))
        q = l2norm(to_heads(qkv[..., :qd], GDN_HEADS)) * (GDN_DK ** -0.5)
        k = l2norm(to_heads(qkv[..., qd:2 * qd], GDN_HEADS))
        v = to_heads(qkv[..., 2 * qd:], GDN_HEADS)

        def gates(a_in, b_in, d):
            a32 = a_in.astype(jnp.float32)
            g = -jnp.exp(a_log[d].astype(jnp.float32)) * jax.nn.softplus(a32 + dt_bias[d].astype(jnp.float32))
            return g.transpose(0, 2, 1), jax.nn.sigmoid(b_in.astype(jnp.float32)).transpose(0, 2, 1)
        g_f, b_f = gates(seg["gdn_a_f"], seg["gdn_b_f"], 0)
        g_b, b_b = gates(seg["gdn_a_b"], seg["gdn_b_b"], 1)
        return q, k, v, g_f, b_f, g_b, b_b

    q, k, v, g_f, b_f, g_b, b_b = prep(s)
    qc, kc, vc, gc_f, bc_f, gc_b, bc_b = prep(sc)
    s0 = jnp.zeros((q.shape[0], GDN_HEADS, GDN_DK, GDN_DV), jnp.float32)
    oc_f, st_f = gdn_scan(qc, kc, vc, gc_f, bc_f, s0)
    oc_b, st_b = gdn_scan(flip_seq(qc), flip_seq(kc), flip_seq(vc), flip_seq(gc_b), flip_seq(bc_b), s0)
    o_f, _ = gdn_scan(q, k, v, g_f, b_f, st_f)
    o_b, _ = gdn_scan(flip_seq(q), flip_seq(k), flip_seq(v), flip_seq(g_b), flip_seq(b_b), st_b)

    def finish(o, seg):
        return from_heads(rmsnorm(o, norm_w)) * jax.nn.silu(seg["gdn_gate"])
    y = finish(o_f + flip_seq(o_b), s)
    yc = finish(oc_f + flip_seq(oc_b), sc) if with_ctx else None
    return y, yc


def shortconv_branch(seg, conv_w):
    return seg["sc_b"] * dwconv(seg["sc_c"] * seg["sc_h"], conv_w) * jax.nn.silu(seg["sc_gate"])


def swa_latent(q, k, v, kc, vc, sink):
    b, l, hq, d = q.shape
    hkv = k.shape[2]
    grp = hq // hkv
    blk = SWA_BLOCK
    nb = l // blk
    lc = kc.shape[1]
    qb = q.reshape(b, nb, blk, hkv, grp, d)

    def band(t):
        tp = jnp.pad(t, ((0, 0), (blk, blk), (0, 0), (0, 0))).reshape(b, nb + 2, blk, hkv, d)
        return jnp.concatenate([tp[:, :-2], tp[:, 1:-1], tp[:, 2:]], axis=2)
    kb, vb = band(k), band(v)
    qpos = jnp.arange(l).reshape(nb, blk)
    kpos = qpos[:, :1] - blk + jnp.arange(3 * blk)[None, :]
    valid = ((jnp.abs(qpos[:, :, None] - kpos[:, None, :]) <= SWA_WINDOW)
             & (kpos[:, None, :] >= 0) & (kpos[:, None, :] < l))
    scale = HEAD_DIM ** -0.5
    s_loc = jnp.einsum("bnqhgd,bnkhd->bnhgqk", qb, kb).astype(jnp.float32) * scale
    s_loc = jnp.where(valid[None, :, None, None], s_loc, -jnp.inf)
    s_ctx = jnp.einsum("bnqhgd,bchd->bnhgqc", qb, kc).astype(jnp.float32) * scale
    s_sink = jnp.broadcast_to(sink.astype(jnp.float32).reshape(hkv, grp)[None, None, :, :, None, None],
                              s_loc.shape[:-1] + (1,))
    p = jax.nn.softmax(jnp.concatenate([s_loc, s_ctx, s_sink], axis=-1), axis=-1).astype(v.dtype)
    o = (jnp.einsum("bnhgqk,bnkhd->bnqhgd", p[..., :3 * blk], vb)
         + jnp.einsum("bnhgqc,bchd->bnqhgd", p[..., 3 * blk:3 * blk + lc], vc))
    return o.reshape(b, l, hq * d)


def ctx_attention(qc, kc, vc, sink):
    b, lc, hq, d = qc.shape
    hkv = kc.shape[2]
    grp = hq // hkv
    qg = qc.reshape(b, lc, hkv, grp, d)
    s = jnp.einsum("bqhgd,bkhd->bhgqk", qg, kc).astype(jnp.float32) * (HEAD_DIM ** -0.5)
    s_sink = jnp.broadcast_to(sink.astype(jnp.float32).reshape(hkv, grp)[None, :, :, None, None],
                              s.shape[:-1] + (1,))
    p = jax.nn.softmax(jnp.concatenate([s, s_sink], axis=-1), axis=-1)[..., :lc].astype(vc.dtype)
    return jnp.einsum("bhgqk,bkhd->bqhgd", p, vc).reshape(b, lc, hq * d)


def swa_mixer(s, sc, sink, rope, with_ctx):
    def qkv(seg):
        b, l, _ = seg["swa_q"].shape
        return (seg["swa_q"].reshape(b, l, SWA_HEADS, HEAD_DIM),
                seg["swa_k"].reshape(b, l, SWA_KV_HEADS, HEAD_DIM),
                seg["swa_v"].reshape(b, l, SWA_KV_HEADS, HEAD_DIM))
    q, k, v = qkv(s)
    qc, kc, vc = qkv(sc)
    q, k = rope2d(q, rope), rope2d(k, rope)
    y = swa_latent(q, k, v, kc, vc, sink) * jax.nn.silu(s["swa_gate"])
    yc = ctx_attention(qc, kc, vc, sink) * jax.nn.silu(sc["swa_gate"]) if with_ctx else None
    return y, yc


def setup_inputs(seed: int = 0) -> dict:
    key = jax.random.key(seed)
    ks = jax.random.split(key, 24)
    D = D_MODEL

    def nrm(k, shape, scale):
        return jax.random.normal(k, shape, jnp.float32) * scale

    dt = jnp.exp(jax.random.uniform(ks[14], (DEPTH, 2, GDN_HEADS), jnp.float32,
                                    np.log(1e-3), np.log(1e-1)))
    return {
        "x": nrm(ks[0], (BATCH, SEQ, D), 1.0),
        "c": nrm(ks[1], (BATCH, D), 1.0),
        "ctx": nrm(ks[2], (BATCH, CTX_LEN, D), 1.0),
        "c_ctx": nrm(ks[3], (D,), 1.0),
        "ada_w": nrm(ks[4], (DEPTH, D, 3 * D), 0.5 * D ** -0.5),
        "ada_b": nrm(ks[5], (DEPTH, 3 * D), 0.02),
        "norm_pre": 1.0 + nrm(ks[6], (DEPTH, D), 0.05),
        "norm_post": 1.0 + nrm(ks[7], (DEPTH, D), 0.05),
        "w_in": nrm(ks[8], (DEPTH, D, IN_WIDTH), D ** -0.5),
        "w_out": nrm(ks[9], (DEPTH, D, D), D ** -0.5),
        "gla_w_decay": nrm(ks[10], (DEPTH, 2, GLA_RANK, GLA_HEADS * GLA_DK), GLA_RANK ** -0.5),
        "gla_b_decay": 1.0 + nrm(ks[11], (DEPTH, 2, GLA_HEADS * GLA_DK), 0.5),
        "gla_norm": 1.0 + nrm(ks[12], (DEPTH, GLA_DV), 0.05),
        "gdn_conv": nrm(ks[13], (DEPTH, GDN_CONV, 3 * GDN_HEADS * GDN_DK), GDN_CONV ** -0.5),
        "gdn_a_log": jnp.log(jax.random.uniform(ks[15], (DEPTH, 2, GDN_HEADS), jnp.float32, 1.0, 16.0)),
        "gdn_dt_bias": dt + jnp.log(-jnp.expm1(-dt)),
        "gdn_norm": 1.0 + nrm(ks[16], (DEPTH, GDN_DV), 0.05),
        "sc_conv": nrm(ks[17], (DEPTH, SC_CONV, SC_WIDTH), SC_CONV ** -0.5),
        "swa_sink": nrm(ks[18], (DEPTH, SWA_HEADS), 0.5),
    }


def reference(x, c, ctx, c_ctx, ada_w, ada_b, norm_pre, norm_post, w_in, w_out,
              gla_w_decay, gla_b_decay, gla_norm, gdn_conv, gdn_a_log, gdn_dt_bias, gdn_norm,
              sc_conv, swa_sink):
    rope = axial_rope_tables(x.shape[1])
    h, hc = x, ctx
    for layer in range(DEPTH):
        with_ctx = layer < DEPTH - 1
        shift, scale, gate = jnp.split(jax.nn.silu(c) @ ada_w[layer] + ada_b[layer], 3, axis=-1)
        shift_c, scale_c, gate_c = jnp.split(jax.nn.silu(c_ctx) @ ada_w[layer] + ada_b[layer], 3, axis=-1)
        n = rmsnorm(h, norm_pre[layer]) * (1.0 + scale[:, None]) + shift[:, None]
        nc = rmsnorm(hc, norm_pre[layer]) * (1.0 + scale_c) + shift_c
        s = split_cols(n @ w_in[layer])
        sc = split_cols(nc @ w_in[layer])

        ya, yca = gla_mixer(s, sc, gla_w_decay[layer], gla_b_decay[layer], gla_norm[layer], with_ctx)
        yb, ycb = gdn_mixer(s, sc, gdn_conv[layer], gdn_a_log[layer], gdn_dt_bias[layer], gdn_norm[layer], with_ctx)
        yc_ = shortconv_branch(s, sc_conv[layer])
        yd, ycd = swa_mixer(s, sc, swa_sink[layer], rope, with_ctx)

        y = jnp.concatenate([ya, yb, yc_, yd], axis=-1) @ w_out[layer]
        h = h + gate[:, None] * rmsnorm(y, norm_post[layer])
        if with_ctx:
            ycc = shortconv_branch(sc, sc_conv[layer])
            yctx = jnp.concatenate([yca, ycb, ycc, ycd], axis=-1) @ w_out[layer]
            hc = hc + gate_c * rmsnorm(yctx, norm_post[layer])
    return h
```

```python
import functools

import jax
import jax.numpy as jnp
from jax import lax
from jax.experimental import pallas as pl
from jax.experimental.pallas import tpu as pltpu

F32 = jnp.float32
BF16 = jnp.bfloat16

D_MODEL = 2048
N_LAYERS = 2
GRID_W = 64
GROUP_W = 512
HEAD_DIM = 128
N_HEADS = 4
EPS = 1e-6
GLA_DK = 64
GLA_RANK = 16
GLA_TAU = 16.0
GDN_CONV = 5
SC_CONV = 3
SWA_KV_HEADS = 2
SWA_WINDOW = 128
ROPE_BASE = 10000.0

_SPLITS = (
    ("gla_q", 256), ("gla_k", 256), ("gla_v", 512), ("gla_lr_f", 16), ("gla_lr_b", 16), ("gla_gate", 512),
    ("gdn_q", 512), ("gdn_k", 512), ("gdn_v", 512),
    ("gdn_a_f", 4), ("gdn_a_b", 4), ("gdn_b_f", 4), ("gdn_b_b", 4), ("gdn_gate", 512),
    ("sc_b", 512), ("sc_c", 512), ("sc_h", 512), ("sc_gate", 512),
    ("swa_q", 512), ("swa_k", 256), ("swa_v", 256), ("swa_gate", 512),
)
_BIG_ORDER = ("gla_q", "gla_k", "gla_v", "gla_gate", "gdn_q", "gdn_k", "gdn_v", "gdn_gate",
              "sc_b", "sc_c", "sc_h", "sc_gate", "swa_q", "swa_k", "swa_v", "swa_gate")
_SMALL_ORDER = ("gla_lr_f", "gla_lr_b", "gdn_a_f", "gdn_a_b", "gdn_b_f", "gdn_b_b")
BIG_W = 7168
SMALL_W = 128
SM_LR = (0, 16)
SM_A = (32, 36)
SM_B = (40, 44)

LANES = 128
VMEM_LIMIT = 56 * 1024 * 1024

CHUNK = 128
SWA_BLOCK = 128

NEG_BIG = -1e30


def _cparams(sem):
    return pltpu.CompilerParams(dimension_semantics=sem, vmem_limit_bytes=VMEM_LIMIT)


def _dot(a, b):
    return lax.dot_general(a.astype(BF16), b.astype(BF16), (((1,), (0,)), ((), ())),
                           preferred_element_type=F32)


def _dot_nt(a, b):
    return lax.dot_general(a.astype(BF16), b.astype(BF16), (((1,), (1,)), ((), ())),
                           preferred_element_type=F32)


def _dot_tn(a, b):
    return lax.dot_general(a.astype(BF16), b.astype(BF16), (((0,), (0,)), ((), ())),
                           preferred_element_type=F32)


def _split(x):
    hi = x.astype(BF16)
    lo = (x - hi.astype(F32)).astype(BF16)
    return hi, lo


def _dot_exact_lhs(a_bf16, x):
    hi, lo = _split(x)
    return _dot(a_bf16, hi) + _dot(a_bf16, lo)


def _dot_exact_rhs(x, b_bf16):
    hi, lo = _split(x)
    return _dot(hi, b_bf16) + _dot(lo, b_bf16)


def _dot_hp(a, b):
    ah, al = _split(a)
    bh, bl = _split(b)
    return _dot(ah, bh) + (_dot(ah, bl) + _dot(al, bh))


def _silu(x):
    return x * (1.0 / (1.0 + jnp.exp(-x)))


def _sigmoid(x):
    return 1.0 / (1.0 + jnp.exp(-x))


def _softplus(x):
    return jnp.maximum(x, 0.0) + jnp.log(1.0 + jnp.exp(-jnp.abs(x)))


def _log_sigmoid(x):
    return jnp.minimum(x, 0.0) - jnp.log(1.0 + jnp.exp(-jnp.abs(x)))


def _iota2(shape, dim):
    return lax.broadcasted_iota(jnp.int32, shape, dim)


def _ada_kernel(c_ref, w_ref, b_ref, o_ref):
    a = _silu(c_ref[...])
    o_ref[0] = _dot(a, w_ref[0]) + b_ref[0]


def _ada(cc, ada_w, ada_b):
    n_layers, d, n = ada_w.shape
    tn = 1024
    return pl.pallas_call(
        _ada_kernel,
        out_shape=jax.ShapeDtypeStruct((n_layers, 8, n), F32),
        grid=(n_layers, n // tn),
        in_specs=[pl.BlockSpec((8, d), lambda l, j: (0, 0)),
                  pl.BlockSpec((1, d, tn), lambda l, j: (l, 0, j)),
                  pl.BlockSpec((1, 1, tn), lambda l, j: (l, 0, j))],
        out_specs=pl.BlockSpec((1, 8, tn), lambda l, j: (l, 0, j)),
        compiler_params=_cparams(("parallel", "parallel")),
        name="ada_mod",
    )(cc, ada_w, ada_b.reshape(n_layers, 1, n))


def _inproj_kernel(x_ref, sc_ref, sh_ref, g_ref, wb_ref, ws_ref, ob_ref, os_ref, xn_ref):
    @pl.when(pl.program_id(2) == 0)
    def _():
        rows_per_step = 128

        def norm_rows(i, _):
            rows = pl.ds(pl.multiple_of(i * rows_per_step, rows_per_step), rows_per_step)
            x = x_ref[0, rows, :]
            y = x * lax.rsqrt(jnp.mean(x * x, axis=-1, keepdims=True) + EPS) * g_ref[...]
            xn_ref[rows, :] = (y * (1.0 + sc_ref[0]) + sh_ref[0]).astype(BF16)
            return _

        lax.fori_loop(0, xn_ref.shape[0] // rows_per_step, norm_rows, None)
        os_ref[0] = jnp.dot(xn_ref[...], ws_ref[...], preferred_element_type=F32)

    ob_ref[0] = jnp.dot(xn_ref[...], wb_ref[...], preferred_element_type=F32).astype(BF16)


def _inproj(x, scale, shift, g, w_big, w_small, *, tm, tn=512):
    b, l, d = x.shape
    return pl.pallas_call(
        _inproj_kernel,
        out_shape=(jax.ShapeDtypeStruct((b, l, BIG_W), BF16), jax.ShapeDtypeStruct((b, l, SMALL_W), F32)),
        grid=(b, l // tm, BIG_W // tn),
        in_specs=[pl.BlockSpec((1, tm, d), lambda bi, i, j: (bi, i, 0)),
                  pl.BlockSpec((1, 1, d), lambda bi, i, j: (bi, 0, 0)),
                  pl.BlockSpec((1, 1, d), lambda bi, i, j: (bi, 0, 0)),
                  pl.BlockSpec((1, d), lambda bi, i, j: (0, 0)),
                  pl.BlockSpec((d, tn), lambda bi, i, j: (0, j)),
                  pl.BlockSpec((d, SMALL_W), lambda bi, i, j: (0, 0))],
        out_specs=(pl.BlockSpec((1, tm, tn), lambda bi, i, j: (bi, i, j)),
                   pl.BlockSpec((1, tm, SMALL_W), lambda bi, i, j: (bi, i, 0))),
        scratch_shapes=[pltpu.VMEM((tm, d), BF16)],
        compiler_params=_cparams(("parallel", "parallel", "arbitrary")),
        name="inproj",
    )(x, scale, shift, g, w_big, w_small)


def _outproj_kernel(ya_ref, yb_ref, yc_ref, yd_ref, w_ref, h_ref, gate_ref, g_ref, o_ref):
    y = jnp.dot(ya_ref[0], w_ref[0:GROUP_W, :], preferred_element_type=F32)
    y += jnp.dot(yb_ref[0], w_ref[GROUP_W:2 * GROUP_W, :], preferred_element_type=F32)
    y += jnp.dot(yc_ref[0], w_ref[2 * GROUP_W:3 * GROUP_W, :], preferred_element_type=F32)
    y += jnp.dot(yd_ref[0], w_ref[3 * GROUP_W:4 * GROUP_W, :], preferred_element_type=F32)
    yn = y * lax.rsqrt(jnp.mean(y * y, axis=-1, keepdims=True) + EPS) * g_ref[...]
    o_ref[0] = h_ref[0] + gate_ref[0] * yn


def _outproj(ya, yb, yc, yd, w_out, h, gate, g, *, tm=256):
    b, l, d = h.shape
    yspec = pl.BlockSpec((1, tm, GROUP_W), lambda bi, i: (bi, i, 0))
    return pl.pallas_call(
        _outproj_kernel,
        out_shape=jax.ShapeDtypeStruct((b, l, d), F32),
        grid=(b, l // tm),
        in_specs=[yspec, yspec, yspec, yspec,
                  pl.BlockSpec((d, d), lambda bi, i: (0, 0)),
                  pl.BlockSpec((1, tm, d), lambda bi, i: (bi, i, 0)),
                  pl.BlockSpec((1, 1, d), lambda bi, i: (bi, 0, 0)),
                  pl.BlockSpec((1, d), lambda bi, i: (0, 0))],
        out_specs=pl.BlockSpec((1, tm, d), lambda bi, i: (bi, i, 0)),
        compiler_params=_cparams(("parallel", "parallel")),
        name="outproj",
    )(ya, yb, yc, yd, w_out, h, gate, g)


def _sconv_kernel(b_ref, c_ref, h_ref, gate_ref, w_ref, o_ref, p_ref, *, seq):
    zeros = jnp.zeros((8, GROUP_W), F32)
    p_ref[0:8, :] = zeros
    p_ref[seq + 8:seq + 16, :] = zeros
    p_ref[8:seq + 8, :] = c_ref[0].astype(F32) * h_ref[0].astype(F32)
    rows = min(seq, 256)
    w = w_ref[...]
    for t0 in range(0, seq, rows):
        conv = (p_ref[t0 + 7:t0 + 7 + rows, :] * w[0:1, :] + p_ref[t0 + 8:t0 + 8 + rows, :] * w[1:2, :]
                + p_ref[t0 + 9:t0 + 9 + rows, :] * w[2:3, :])
        y = b_ref[0, t0:t0 + rows, :].astype(F32) * conv * _silu(gate_ref[0, t0:t0 + rows, :].astype(F32))
        o_ref[0, t0:t0 + rows, :] = y.astype(BF16)


def _sconv(big, w):
    b, l, _ = big.shape
    col0 = 3584 // GROUP_W

    def spec(k):
        return pl.BlockSpec((1, l, GROUP_W), lambda bi, k=k: (bi, 0, col0 + k))
    return pl.pallas_call(
        functools.partial(_sconv_kernel, seq=l),
        out_shape=jax.ShapeDtypeStruct((b, l, GROUP_W), BF16),
        grid=(b,),
        in_specs=[spec(0), spec(1), spec(2), spec(3), pl.BlockSpec((SC_CONV, GROUP_W), lambda bi: (0, 0))],
        out_specs=pl.BlockSpec((1, l, GROUP_W), lambda bi: (bi, 0, 0)),
        scratch_shapes=[pltpu.VMEM((l + 16, GROUP_W), F32)],
        compiler_params=_cparams(("parallel",)),
        name="sconv",
    )(big, big, big, big, w)


def _gla_kernel(q_ref, k_ref, v_ref, gate_ref, sm_ref, wd_ref, bd_ref, nw_ref, s0f_ref, s0b_ref,
                *rest, seq, with_out):
    if with_out:
        y_ref, stf_ref, stb_ref, of_ref, st_ref = rest
    else:
        stf_ref, stb_ref, st_ref = rest
        y_ref = of_ref = None
    c = CHUNK
    n_chunks = seq // c
    ri = _iota2((c, c), 0)
    ci = _iota2((c, c), 1)
    lower = ri >= ci
    upper = ri <= ci
    tril = jnp.where(lower, 1.0, 0.0).astype(BF16)
    triu = jnp.where(upper, 1.0, 0.0).astype(BF16)
    qscale = GLA_DK ** -0.5

    def chunk(n, d):
        r0 = pl.multiple_of(n * c, c)
        rows = pl.ds(r0, c)
        lr = sm_ref[0, rows, :][:, SM_LR[d]:SM_LR[d] + GLA_RANK]
        z = _dot(lr, wd_ref[d]) + bd_ref[d]
        la = _log_sigmoid(z) * (1.0 / GLA_TAU)
        cum = _dot_exact_lhs(tril if d == 0 else triu, la)
        last = c - 1 if d == 0 else 0
        tot = cum[last:last + 1, :]
        kf = k_ref[0, rows, :].astype(F32)
        ko = kf * jnp.exp(tot - cum)
        st = st_ref[...]
        v = v_ref[0, rows, :]
        if with_out:
            mid = cum[c // 2:c // 2 + 1, :]
            qf = q_ref[0, rows, :].astype(F32) * qscale
            qp = (qf * jnp.exp(cum - mid)).astype(BF16)
            kp = (kf * jnp.exp(mid - cum)).astype(BF16)
            qi = (qf * jnp.exp(cum)).astype(BF16)
            stb = st.astype(BF16)
        upd = []
        for h in range(N_HEADS):
            sl = slice(h * GLA_DK, (h + 1) * GLA_DK)
            hs = slice(h * HEAD_DIM, (h + 1) * HEAD_DIM)
            vh = v[:, hs]
            if with_out:
                a = _dot_nt(qp[:, sl], kp[:, sl])
                a = jnp.where(lower if d == 0 else upper, a, 0.0)
                o = _dot(a, vh) + _dot_nt(qi[:, sl], stb[:, sl])
                if d == 0:
                    of_ref[rows, hs] = o
                else:
                    o = o + of_ref[rows, hs]
                    o = o * lax.rsqrt(jnp.mean(o * o, axis=-1, keepdims=True) + EPS) * nw_ref[...]
                    y_ref[0, rows, hs] = (o * _silu(gate_ref[0, rows, hs].astype(F32))).astype(BF16)
            upd.append(_dot_tn(vh, ko[:, sl]))
        st_ref[...] = st * jnp.exp(tot) + jnp.concatenate(upd, axis=1)

    st_ref[...] = s0f_ref[0]
    lax.fori_loop(0, n_chunks, lambda n, _: chunk(n, 0), None)
    stf_ref[0] = st_ref[...]
    st_ref[...] = s0b_ref[0]
    lax.fori_loop(0, n_chunks, lambda n, _: chunk(n_chunks - 1 - n, 1), None)
    stb_ref[0] = st_ref[...]


def _gla(big, small, wd, bd, nw, s0f, s0b, *, with_out):
    b, l, _ = big.shape
    st_shape = jax.ShapeDtypeStruct((b, HEAD_DIM, N_HEADS * GLA_DK), F32)
    st_spec = pl.BlockSpec((1, HEAD_DIM, N_HEADS * GLA_DK), lambda bi: (bi, 0, 0))
    in_specs = [pl.BlockSpec((1, l, 256), lambda bi: (bi, 0, 0)),
                pl.BlockSpec((1, l, 256), lambda bi: (bi, 0, 1)),
                pl.BlockSpec((1, l, 512), lambda bi: (bi, 0, 1)),
                pl.BlockSpec((1, l, 512), lambda bi: (bi, 0, 2)),
                pl.BlockSpec((1, l, SMALL_W), lambda bi: (bi, 0, 0)),
                pl.BlockSpec((2, GLA_RANK, 256), lambda bi: (0, 0, 0)),
                pl.BlockSpec((2, 1, 256), lambda bi: (0, 0, 0)),
                pl.BlockSpec((1, HEAD_DIM), lambda bi: (0, 0)),
                st_spec, st_spec]
    out_shape = [st_shape, st_shape]
    out_specs = [st_spec, st_spec]
    scratch = [pltpu.VMEM((HEAD_DIM, N_HEADS * GLA_DK), F32)]
    if with_out:
        out_shape = [jax.ShapeDtypeStruct((b, l, GROUP_W), BF16)] + out_shape
        out_specs = [pl.BlockSpec((1, l, GROUP_W), lambda bi: (bi, 0, 0))] + out_specs
        scratch = [pltpu.VMEM((l, GROUP_W), F32)] + scratch
    res = pl.pallas_call(
        functools.partial(_gla_kernel, seq=l, with_out=with_out),
        out_shape=tuple(out_shape), grid=(b,), in_specs=in_specs, out_specs=tuple(out_specs),
        scratch_shapes=scratch, compiler_params=_cparams(("parallel",)),
        name="gla_out" if with_out else "gla_state",
    )(big, big, big, big, small, wd, bd.reshape(2, 1, -1), nw.reshape(1, -1), s0f, s0b)
    return res if with_out else (None,) + tuple(res)


def _unit_lower_inverse(low, ri, ci):
    c = low.shape[0]
    eye = jnp.where(ri == ci, 1.0, 0.0)
    t = eye - jnp.where((ri // 2) == (ci // 2), low, 0.0)
    s = 2
    while s < c:
        off = jnp.where(((ri // (2 * s)) == (ci // (2 * s))) & ((ri // s) != (ci // s)), low, 0.0)
        t = t - _dot_hp(_dot_hp(t, off), t)
        s *= 2
    return t


def _gdn_kernel(qkv_ref, gate_ref, sm_ref, cw_ref, gp_ref, nw_ref, s0f_ref, s0b_ref,
                *rest, seq, with_out):
    if with_out:
        y_ref, stf_ref, stb_ref, x_ref, of_ref, st_ref = rest
    else:
        stf_ref, stb_ref, x_ref, st_ref = rest
        y_ref = of_ref = None
    c = CHUNK
    n_chunks = seq // c
    width = 3 * GROUP_W
    ri = _iota2((c, c), 0)
    ci = _iota2((c, c), 1)
    lower = ri >= ci
    upper = ri <= ci
    tril = jnp.where(lower, 1.0, 0.0).astype(BF16)
    triu = jnp.where(upper, 1.0, 0.0).astype(BF16)

    def conv_chunk(n, _):
        r0 = pl.multiple_of(n * c, c)
        prev0 = pl.multiple_of(jnp.maximum(r0 - 16, 0), 16)
        next0 = pl.multiple_of(jnp.minimum(r0 + c, seq - 16), 16)
        prev = jnp.where(n > 0, qkv_ref[0, pl.ds(prev0, 16), :].astype(F32), 0.0)
        nxt = jnp.where(n < n_chunks - 1, qkv_ref[0, pl.ds(next0, 16), :].astype(F32), 0.0)
        xe = jnp.concatenate([prev, qkv_ref[0, pl.ds(r0, c), :].astype(F32), nxt], axis=0)
        acc = xe[14:14 + c, :] * cw_ref[0:1, :]
        for j in range(1, GDN_CONV):
            acc = acc + xe[14 + j:14 + j + c, :] * cw_ref[j:j + 1, :]
        acc = _silu(acc)
        for h in range(2 * N_HEADS):
            hs = slice(h * HEAD_DIM, (h + 1) * HEAD_DIM)
            t = acc[:, hs]
            t = t * lax.rsqrt(jnp.sum(t * t, axis=-1, keepdims=True) + EPS)
            if h < N_HEADS:
                t = t * (HEAD_DIM ** -0.5)
            x_ref[pl.ds(r0, c), hs] = t
        x_ref[pl.ds(r0, c), 2 * GROUP_W:width] = acc[:, 2 * GROUP_W:width]
        return _

    lax.fori_loop(0, n_chunks, conv_chunk, None)

    def chunk(n, d):
        r0 = pl.multiple_of(n * c, c)
        rows = pl.ds(r0, c)
        sm = sm_ref[0, rows, :]
        g = gp_ref[0:1, :] * _softplus(sm + gp_ref[1:2, :])
        beta = _sigmoid(sm)
        cum = _dot_exact_lhs(tril if d == 0 else triu, g)
        cum_row = _dot_exact_rhs(g.T, triu if d == 0 else tril)
        last = c - 1 if d == 0 else 0
        incl = lower if d == 0 else upper
        strict = (ri > ci) if d == 0 else (ri < ci)
        for h in range(N_HEADS):
            hs = slice(h * HEAD_DIM, (h + 1) * HEAD_DIM)
            la = SM_A[d] + h
            lb = SM_B[d] + h
            cum_c = cum[:, la:la + 1]
            cum_r = cum_row[la:la + 1, :]
            tot = cum[last:last + 1, la:la + 1]
            beta_c = beta[:, lb:lb + 1]
            decay = jnp.where(incl, jnp.exp(jnp.minimum(cum_c - cum_r, 0.0)), 0.0)
            kf = x_ref[rows, GROUP_W + h * HEAD_DIM:GROUP_W + (h + 1) * HEAD_DIM]
            vf = x_ref[rows, 2 * GROUP_W + h * HEAD_DIM:2 * GROUP_W + (h + 1) * HEAD_DIM]
            kb = kf * beta_c
            low = jnp.where(strict, _dot_nt(kb, kf) * decay, 0.0)
            t = _unit_lower_inverse(low, ri, ci)
            e_c = jnp.exp(cum_c)
            u = _dot_hp(t, vf * beta_c)
            w = _dot_hp(t, kb * e_c)
            ko = kf * jnp.exp(tot - cum_c)
            st = st_ref[h * HEAD_DIM:(h + 1) * HEAD_DIM, :]
            v_new = u - _dot(w, st)
            if with_out:
                qf = x_ref[rows, hs]
                attn = _dot_nt(qf, kf) * decay
                o = _dot(qf * e_c, st) + _dot(attn, v_new)
                if d == 0:
                    of_ref[rows, hs] = o
                else:
                    o = o + of_ref[rows, hs]
                    o = o * lax.rsqrt(jnp.mean(o * o, axis=-1, keepdims=True) + EPS) * nw_ref[...]
                    y_ref[0, rows, hs] = (o * _silu(gate_ref[0, rows, hs].astype(F32))).astype(BF16)
            st_ref[h * HEAD_DIM:(h + 1) * HEAD_DIM, :] = st * jnp.exp(tot) + _dot_tn(ko, v_new)

    st_ref[...] = s0f_ref[0]
    lax.fori_loop(0, n_chunks, lambda n, _: chunk(n, 0), None)
    stf_ref[0] = st_ref[...]
    st_ref[...] = s0b_ref[0]
    lax.fori_loop(0, n_chunks, lambda n, _: chunk(n_chunks - 1 - n, 1), None)
    stb_ref[0] = st_ref[...]


def _gdn(big, small, cw, gp, nw, s0f, s0b, *, with_out):
    b, l, _ = big.shape
    st_shape = jax.ShapeDtypeStruct((b, N_HEADS * HEAD_DIM, HEAD_DIM), F32)
    st_spec = pl.BlockSpec((1, N_HEADS * HEAD_DIM, HEAD_DIM), lambda bi: (bi, 0, 0))
    in_specs = [pl.BlockSpec((1, l, 3 * GROUP_W), lambda bi: (bi, 0, 1)),
                pl.BlockSpec((1, l, GROUP_W), lambda bi: (bi, 0, 6)),
                pl.BlockSpec((1, l, SMALL_W), lambda bi: (bi, 0, 0)),
                pl.BlockSpec((GDN_CONV, 3 * GROUP_W), lambda bi: (0, 0)),
                pl.BlockSpec((8, SMALL_W), lambda bi: (0, 0)),
                pl.BlockSpec((1, HEAD_DIM), lambda bi: (0, 0)),
                st_spec, st_spec]
    out_shape = [st_shape, st_shape]
    out_specs = [st_spec, st_spec]
    scratch = [pltpu.VMEM((l, 3 * GROUP_W), F32), pltpu.VMEM((N_HEADS * HEAD_DIM, HEAD_DIM), F32)]
    if with_out:
        out_shape = [jax.ShapeDtypeStruct((b, l, GROUP_W), BF16)] + out_shape
        out_specs = [pl.BlockSpec((1, l, GROUP_W), lambda bi: (bi, 0, 0))] + out_specs
        scratch = [scratch[0], pltpu.VMEM((l, GROUP_W), F32), scratch[1]]
    res = pl.pallas_call(
        functools.partial(_gdn_kernel, seq=l, with_out=with_out),
        out_shape=tuple(out_shape), grid=(b,), in_specs=in_specs, out_specs=tuple(out_specs),
        scratch_shapes=scratch, compiler_params=_cparams(("parallel",)),
        name="gdn_out" if with_out else "gdn_state",
    )(big, big, small, cw, gp, nw.reshape(1, -1), s0f, s0b)
    return res if with_out else (None,) + tuple(res)


def _gdn_gate_params(a_log, dt_bias):
    gp = jnp.zeros((8, SMALL_W), F32)
    gp = gp.at[0, SM_A[0]:SM_A[0] + 2 * N_HEADS].set(-jnp.exp(a_log.astype(F32)).reshape(-1))
    gp = gp.at[1, SM_A[0]:SM_A[0] + 2 * N_HEADS].set(dt_bias.astype(F32).reshape(-1))
    return gp


def _rope(x, cos, sin, lane):
    swapped = jnp.where((lane % 64) < 32, pltpu.roll(x, 96, 1), pltpu.roll(x, 32, 1))
    return x * cos + swapped * sin


def _softmax_sink_pv(s, sink_col, vv):
    m = jnp.maximum(jnp.max(s, axis=-1, keepdims=True), sink_col)
    p = jnp.exp(s - m)
    den = jnp.sum(p, axis=-1, keepdims=True) + jnp.exp(sink_col - m)
    return _dot(p, vv) * (1.0 / den)


def _swa_kernel(*refs, seq, ctx_len, with_ctx):
    if with_ctx:
        (q_ref, k_ref, v_ref, gate_ref, kc_ref, vc_ref, cos_ref, sin_ref, sink_ref, qc_ref, gatec_ref,
         y_ref, yc_ref, kr_ref) = refs
    else:
        (q_ref, k_ref, v_ref, gate_ref, kc_ref, vc_ref, cos_ref, sin_ref, sink_ref,
         y_ref, kr_ref) = refs
    blk = SWA_BLOCK
    nb = seq // blk
    win = 3 * blk
    grp = N_HEADS // SWA_KV_HEADS
    scale = HEAD_DIM ** -0.5
    lane = _iota2((blk, HEAD_DIM), 1)

    def rope_k(n, _):
        rows = pl.ds(pl.multiple_of(n * blk, blk), blk)
        cos = cos_ref[rows, :]
        sin = sin_ref[rows, :]
        for g in range(SWA_KV_HEADS):
            hs = slice(g * HEAD_DIM, (g + 1) * HEAD_DIM)
            kr_ref[rows, hs] = _rope(k_ref[0, rows, hs].astype(F32), cos, sin, lane).astype(BF16)
        return _

    lax.fori_loop(0, nb, rope_k, None)

    rowq = _iota2((grp * blk, win + ctx_len), 0) % blk
    colk = _iota2((grp * blk, win + ctx_len), 1)
    is_ctx = colk >= win
    head_row = _iota2((grp * blk, 1), 0) // blk

    def sink_column(g, n_rows):
        hr = _iota2((grp * n_rows, 1), 0) // n_rows
        col = jnp.zeros((grp * n_rows, 1), F32)
        for j in range(grp):
            col = jnp.where(hr == j, sink_ref[0:1, g * grp + j:g * grp + j + 1], col)
        return col

    def q_block(n, _):
        r0 = pl.multiple_of(n * blk, blk)
        rows = pl.ds(r0, blk)
        k0 = pl.multiple_of(jnp.clip(r0 - blk, 0, seq - win), blk)
        cos = cos_ref[rows, :]
        sin = sin_ref[rows, :]
        valid = is_ctx | (jnp.abs(r0 + rowq - (k0 + colk)) <= SWA_WINDOW)
        for g in range(SWA_KV_HEADS):
            hs = slice(g * HEAD_DIM, (g + 1) * HEAD_DIM)
            qg = jnp.concatenate(
                [_rope(q_ref[0, rows, (g * grp + j) * HEAD_DIM:(g * grp + j + 1) * HEAD_DIM].astype(F32),
                       cos, sin, lane) for j in range(grp)], axis=0).astype(BF16)
            kk = jnp.concatenate([kr_ref[pl.ds(k0, win), hs], kc_ref[0, :, hs]], axis=0)
            vv = jnp.concatenate([v_ref[0, pl.ds(k0, win), hs], vc_ref[0, :, hs]], axis=0)
            s = jnp.where(valid, _dot_nt(qg, kk) * scale, NEG_BIG)
            o = _softmax_sink_pv(s, sink_column(g, blk), vv)
            for j in range(grp):
                cs = slice((g * grp + j) * HEAD_DIM, (g * grp + j + 1) * HEAD_DIM)
                y_ref[0, rows, cs] = (o[j * blk:(j + 1) * blk, :]
                                      * _silu(gate_ref[0, rows, cs].astype(F32))).astype(BF16)
        return _

    lax.fori_loop(0, nb, q_block, None)

    if with_ctx:
        for g in range(SWA_KV_HEADS):
            hs = slice(g * HEAD_DIM, (g + 1) * HEAD_DIM)
            qg = jnp.concatenate([qc_ref[0, :, (g * grp + j) * HEAD_DIM:(g * grp + j + 1) * HEAD_DIM]
                                  for j in range(grp)], axis=0)
            s = _dot_nt(qg, kc_ref[0, :, hs]) * scale
            o = _softmax_sink_pv(s, sink_column(g, ctx_len), vc_ref[0, :, hs])
            for j in range(grp):
                cs = slice((g * grp + j) * HEAD_DIM, (g * grp + j + 1) * HEAD_DIM)
                yc_ref[0, :, cs] = (o[j * ctx_len:(j + 1) * ctx_len, :]
                                    * _silu(gatec_ref[0, :, cs].astype(F32))).astype(BF16)


def _swa(big, bigc, cos, sin, sink, *, with_ctx):
    b, l, _ = big.shape
    lc = bigc.shape[1]
    in_specs = [pl.BlockSpec((1, l, 512), lambda bi: (bi, 0, 11)),
                pl.BlockSpec((1, l, 256), lambda bi: (bi, 0, 24)),
                pl.BlockSpec((1, l, 256), lambda bi: (bi, 0, 25)),
                pl.BlockSpec((1, l, 512), lambda bi: (bi, 0, 13)),
                pl.BlockSpec((1, lc, 256), lambda bi: (bi, 0, 24)),
                pl.BlockSpec((1, lc, 256), lambda bi: (bi, 0, 25)),
                pl.BlockSpec((l, HEAD_DIM), lambda bi: (0, 0)),
                pl.BlockSpec((l, HEAD_DIM), lambda bi: (0, 0)),
                pl.BlockSpec((1, LANES), lambda bi: (0, 0))]
    args = [big, big, big, big, bigc, bigc, cos, sin, sink]
    out_shape = [jax.ShapeDtypeStruct((b, l, GROUP_W), BF16)]
    out_specs = [pl.BlockSpec((1, l, GROUP_W), lambda bi: (bi, 0, 0))]
    if with_ctx:
        in_specs += [pl.BlockSpec((1, lc, 512), lambda bi: (bi, 0, 11)),
                     pl.BlockSpec((1, lc, 512), lambda bi: (bi, 0, 13))]
        args += [bigc, bigc]
        out_shape.append(jax.ShapeDtypeStruct((b, lc, GROUP_W), BF16))
        out_specs.append(pl.BlockSpec((1, lc, GROUP_W), lambda bi: (bi, 0, 0)))
    res = pl.pallas_call(
        functools.partial(_swa_kernel, seq=l, ctx_len=lc, with_ctx=with_ctx),
        out_shape=tuple(out_shape), grid=(b,), in_specs=in_specs, out_specs=tuple(out_specs),
        scratch_shapes=[pltpu.VMEM((l, SWA_KV_HEADS * HEAD_DIM), BF16)],
        compiler_params=_cparams(("parallel",)),
        name="swa_ctx" if with_ctx else "swa",
    )(*args)
    return (res[0], res[1]) if with_ctx else (res[0], None)


def _rope_tables(seq_len):
    rows = seq_len // GRID_W
    row = jnp.repeat(jnp.arange(rows, dtype=F32), GRID_W)
    col = jnp.tile(jnp.arange(GRID_W, dtype=F32), rows)
    axis_dim = HEAD_DIM // 2
    inv_freq = ROPE_BASE ** (-jnp.arange(0, axis_dim, 2, dtype=F32) / axis_dim)
    ang_r = row[:, None] * inv_freq
    ang_c = col[:, None] * inv_freq
    cos = jnp.concatenate([jnp.cos(ang_r), jnp.cos(ang_r), jnp.cos(ang_c), jnp.cos(ang_c)], axis=-1)
    sin = jnp.concatenate([-jnp.sin(ang_r), jnp.sin(ang_r), -jnp.sin(ang_c), jnp.sin(ang_c)], axis=-1)
    return cos, sin


def _pack_w_in(w):
    off, cols = 0, {}
    for name, size in _SPLITS:
        cols[name] = w[:, off:off + size]
        off += size
    big = jnp.concatenate([cols[n] for n in _BIG_ORDER], axis=1).astype(BF16)
    small = jnp.concatenate([cols[n] for n in _SMALL_ORDER], axis=1)
    small = jnp.pad(small, ((0, 0), (0, SMALL_W - small.shape[1]))).astype(BF16)
    return big, small


def kernel(x, c, ctx, c_ctx, ada_w, ada_b, norm_pre, norm_post, w_in, w_out, gla_w_decay, gla_b_decay,
           gla_norm, gdn_conv, gdn_a_log, gdn_dt_bias, gdn_norm, sc_conv, swa_sink):
    b, l, d = x.shape
    cos, sin = _rope_tables(l)
    cc = jnp.zeros((8, d), F32).at[:b].set(c).at[b].set(c_ctx)
    mod = _ada(cc, ada_w, ada_b)
    h, hc = x, ctx
    n_layers = ada_w.shape[0]
    for layer in range(n_layers):
        with_ctx = layer < n_layers - 1
        shift, scale, gate = (mod[layer, :, i * d:(i + 1) * d] for i in range(3))

        def per_batch(t):
            return t[:b, None, :], jnp.broadcast_to(t[b][None, None, :], (b, 1, d))
        (shift_l, shift_c), (scale_l, scale_c), (gate_l, gate_c) = per_batch(shift), per_batch(scale), per_batch(gate)
        w_big, w_small = _pack_w_in(w_in[layer])
        g_pre = norm_pre[layer].reshape(1, d)
        big, small = _inproj(h, scale_l, shift_l, g_pre, w_big, w_small, tm=min(l, 1024))
        bigc, smallc = _inproj(hc, scale_c, shift_c, g_pre, w_big, w_small, tm=hc.shape[1])

        zf = jnp.zeros((b, HEAD_DIM, N_HEADS * GLA_DK), F32)
        yca, stf, stb = _gla(bigc, smallc, gla_w_decay[layer], gla_b_decay[layer], gla_norm[layer], zf, zf,
                             with_out=with_ctx)
        ya, _, _ = _gla(big, small, gla_w_decay[layer], gla_b_decay[layer], gla_norm[layer], stf, stb,
                        with_out=True)

        gp = _gdn_gate_params(gdn_a_log[layer], gdn_dt_bias[layer])
        zg = jnp.zeros((b, N_HEADS * HEAD_DIM, HEAD_DIM), F32)
        ycb, gtf, gtb = _gdn(bigc, smallc, gdn_conv[layer], gp, gdn_norm[layer], zg, zg, with_out=with_ctx)
        yb, _, _ = _gdn(big, small, gdn_conv[layer], gp, gdn_norm[layer], gtf, gtb, with_out=True)

        yc_ = _sconv(big, sc_conv[layer])
        sink = jnp.zeros((1, LANES), F32).at[0, :N_HEADS].set(swa_sink[layer])
        yd, ycd = _swa(big, bigc, cos, sin, sink, with_ctx=with_ctx)

        w_o = w_out[layer].astype(BF16)
        g_post = norm_post[layer].reshape(1, d)
        h_new = _outproj(ya, yb, yc_, yd, w_o, h, gate_l, g_post)
        if with_ctx:
            ycc = _sconv(bigc, sc_conv[layer])
            hc = _outproj(yca, ycb, ycc, ycd, w_o, hc, gate_c, g_post)
        h = h_new
    return h
```

```python
import functools

import jax
import jax.numpy as jnp
from jax import lax
from jax.experimental import pallas as pl
from jax.experimental.pallas import tpu as pltpu

F32 = jnp.float32
BF16 = jnp.bfloat16

D_MODEL = 2048
N_LAYERS = 2
GRID_W = 64
GROUP_W = 512
HEAD_DIM = 128
N_HEADS = 4
EPS = 1e-6
GLA_DK = 64
GLA_RANK = 16
GLA_TAU = 16.0
GDN_CONV = 5
SC_CONV = 3
SWA_KV_HEADS = 2
SWA_WINDOW = 128
ROPE_BASE = 10000.0

_SPLITS = (
    ("gla_q", 256), ("gla_k", 256), ("gla_v", 512), ("gla_lr_f", 16), ("gla_lr_b", 16), ("gla_gate", 512),
    ("gdn_q", 512), ("gdn_k", 512), ("gdn_v", 512),
    ("gdn_a_f", 4), ("gdn_a_b", 4), ("gdn_b_f", 4), ("gdn_b_b", 4), ("gdn_gate", 512),
    ("sc_b", 512), ("sc_c", 512), ("sc_h", 512), ("sc_gate", 512),
    ("swa_q", 512), ("swa_k", 256), ("swa_v", 256), ("swa_gate", 512),
)
_BIG_ORDER = ("gla_q", "gla_k", "gla_v", "gla_gate", "gdn_q", "gdn_k", "gdn_v", "gdn_gate",
              "sc_b", "sc_c", "sc_h", "sc_gate", "swa_q", "swa_k", "swa_v", "swa_gate")
_SMALL_ORDER = ("gla_lr_f", "gla_lr_b", "gdn_a_f", "gdn_a_b", "gdn_b_f", "gdn_b_b")
BIG_W = 7168
SMALL_W = 128
SM_LR = (0, 16)
SM_A = (32, 36)
SM_B = (40, 44)

LANES = 128
VMEM_LIMIT = 56 * 1024 * 1024

CHUNK = 128
SWA_BLOCK = 128

NEG_BIG = -1e30


def _cparams(sem):
    return pltpu.CompilerParams(dimension_semantics=sem, vmem_limit_bytes=VMEM_LIMIT)


def _dot(a, b):
    return lax.dot_general(a.astype(BF16), b.astype(BF16), (((1,), (0,)), ((), ())),
                           preferred_element_type=F32)


def _dot_nt(a, b):
    return lax.dot_general(a.astype(BF16), b.astype(BF16), (((1,), (1,)), ((), ())),
                           preferred_element_type=F32)


def _dot_tn(a, b):
    return lax.dot_general(a.astype(BF16), b.astype(BF16), (((0,), (0,)), ((), ())),
                           preferred_element_type=F32)


def _split(x):
    hi = x.astype(BF16)
    lo = (x - hi.astype(F32)).astype(BF16)
    return hi, lo


def _dot_exact_lhs(a_bf16, x):
    hi, lo = _split(x)
    return _dot(a_bf16, hi) + _dot(a_bf16, lo)


def _dot_exact_rhs(x, b_bf16):
    hi, lo = _split(x)
    return _dot(hi, b_bf16) + _dot(lo, b_bf16)


def _dot_hp(a, b):
    ah, al = _split(a)
    bh, bl = _split(b)
    return _dot(ah, bh) + (_dot(ah, bl) + _dot(al, bh))


def _silu(x):
    return x * (1.0 / (1.0 + jnp.exp(-x)))


def _sigmoid(x):
    return 1.0 / (1.0 + jnp.exp(-x))


def _softplus(x):
    return jnp.maximum(x, 0.0) + jnp.log(1.0 + jnp.exp(-jnp.abs(x)))


def _log_sigmoid(x):
    return jnp.minimum(x, 0.0) - jnp.log(1.0 + jnp.exp(-jnp.abs(x)))


def _iota2(shape, dim):
    return lax.broadcasted_iota(jnp.int32, shape, dim)


def _ada_kernel(c_ref, w_ref, b_ref, o_ref):
    a = _silu(c_ref[...])
    o_ref[0] = _dot(a, w_ref[0]) + b_ref[0]


def _ada(cc, ada_w, ada_b):
    n_layers, d, n = ada_w.shape
    tn = 1024
    return pl.pallas_call(
        _ada_kernel,
        out_shape=jax.ShapeDtypeStruct((n_layers, 8, n), F32),
        grid=(n_layers, n // tn),
        in_specs=[pl.BlockSpec((8, d), lambda l, j: (0, 0)),
                  pl.BlockSpec((1, d, tn), lambda l, j: (l, 0, j)),
                  pl.BlockSpec((1, 1, tn), lambda l, j: (l, 0, j))],
        out_specs=pl.BlockSpec((1, 8, tn), lambda l, j: (l, 0, j)),
        compiler_params=_cparams(("parallel", "parallel")),
        name="ada_mod",
    )(cc, ada_w, ada_b.reshape(n_layers, 1, n))


def _inproj_kernel(x_ref, sc_ref, sh_ref, g_ref, wb_ref, ws_ref, ob_ref, os_ref, xn_ref):
    @pl.when(pl.program_id(2) == 0)
    def _():
        rows_per_step = 128

        def norm_rows(i, _):
            rows = pl.ds(pl.multiple_of(i * rows_per_step, rows_per_step), rows_per_step)
            x = x_ref[0, rows, :]
            y = x * lax.rsqrt(jnp.mean(x * x, axis=-1, keepdims=True) + EPS) * g_ref[...]
            xn_ref[rows, :] = (y * (1.0 + sc_ref[0]) + sh_ref[0]).astype(BF16)
            return _

        lax.fori_loop(0, xn_ref.shape[0] // rows_per_step, norm_rows, None)
        os_ref[0] = jnp.dot(xn_ref[...], ws_ref[...], preferred_element_type=F32)

    ob_ref[0] = jnp.dot(xn_ref[...], wb_ref[...], preferred_element_type=F32).astype(BF16)


def _inproj(x, scale, shift, g, w_big, w_small, *, tm, tn=512):
    b, l, d = x.shape
    return pl.pallas_call(
        _inproj_kernel,
        out_shape=(jax.ShapeDtypeStruct((b, l, BIG_W), BF16), jax.ShapeDtypeStruct((b, l, SMALL_W), F32)),
        grid=(b, l // tm, BIG_W // tn),
        in_specs=[pl.BlockSpec((1, tm, d), lambda bi, i, j: (bi, i, 0)),
                  pl.BlockSpec((1, 1, d), lambda bi, i, j: (bi, 0, 0)),
                  pl.BlockSpec((1, 1, d), lambda bi, i, j: (bi, 0, 0)),
                  pl.BlockSpec((1, d), lambda bi, i, j: (0, 0)),
                  pl.BlockSpec((d, tn), lambda bi, i, j: (0, j)),
                  pl.BlockSpec((d, SMALL_W), lambda bi, i, j: (0, 0))],
        out_specs=(pl.BlockSpec((1, tm, tn), lambda bi, i, j: (bi, i, j)),
                   pl.BlockSpec((1, tm, SMALL_W), lambda bi, i, j: (bi, i, 0))),
        scratch_shapes=[pltpu.VMEM((tm, d), BF16)],
        compiler_params=_cparams(("parallel", "parallel", "arbitrary")),
        name="inproj",
    )(x, scale, shift, g, w_big, w_small)


def _outproj_kernel(ya_ref, yb_ref, yc_ref, yd_ref, w_ref, h_ref, gate_ref, g_ref, o_ref):
    y = jnp.dot(ya_ref[0], w_ref[0:GROUP_W, :], preferred_element_type=F32)
    y += jnp.dot(yb_ref[0], w_ref[GROUP_W:2 * GROUP_W, :], preferred_element_type=F32)
    y += jnp.dot(yc_ref[0], w_ref[2 * GROUP_W:3 * GROUP_W, :], preferred_element_type=F32)
    y += jnp.dot(yd_ref[0], w_ref[3 * GROUP_W:4 * GROUP_W, :], preferred_element_type=F32)
    yn = y * lax.rsqrt(jnp.mean(y * y, axis=-1, keepdims=True) + EPS) * g_ref[...]
    o_ref[0] = h_ref[0] + gate_ref[0] * yn


def _outproj(ya, yb, yc, yd, w_out, h, gate, g, *, tm=256):
    b, l, d = h.shape
    yspec = pl.BlockSpec((1, tm, GROUP_W), lambda bi, i: (bi, i, 0))
    return pl.pallas_call(
        _outproj_kernel,
        out_shape=jax.ShapeDtypeStruct((b, l, d), F32),
        grid=(b, l // tm),
        in_specs=[yspec, yspec, yspec, yspec,
                  pl.BlockSpec((d, d), lambda bi, i: (0, 0)),
                  pl.BlockSpec((1, tm, d), lambda bi, i: (bi, i, 0)),
                  pl.BlockSpec((1, 1, d), lambda bi, i: (bi, 0, 0)),
                  pl.BlockSpec((1, d), lambda bi, i: (0, 0))],
        out_specs=pl.BlockSpec((1, tm, d), lambda bi, i: (bi, i, 0)),
        compiler_params=_cparams(("parallel", "parallel")),
        name="outproj",
    )(ya, yb, yc, yd, w_out, h, gate, g)


def _sconv_kernel(b_ref, c_ref, h_ref, gate_ref, w_ref, o_ref, p_ref, *, seq):
    zeros = jnp.zeros((8, GROUP_W), F32)
    p_ref[0:8, :] = zeros
    p_ref[seq + 8:seq + 16, :] = zeros
    p_ref[8:seq + 8, :] = c_ref[0].astype(F32) * h_ref[0].astype(F32)
    rows = min(seq, 256)
    w = w_ref[...]
    for t0 in range(0, seq, rows):
        conv = (p_ref[t0 + 7:t0 + 7 + rows, :] * w[0:1, :] + p_ref[t0 + 8:t0 + 8 + rows, :] * w[1:2, :]
                + p_ref[t0 + 9:t0 + 9 + rows, :] * w[2:3, :])
        y = b_ref[0, t0:t0 + rows, :].astype(F32) * conv * _silu(gate_ref[0, t0:t0 + rows, :].astype(F32))
        o_ref[0, t0:t0 + rows, :] = y.astype(BF16)


def _sconv(big, w):
    b, l, _ = big.shape
    col0 = 3584 // GROUP_W

    def spec(k):
        return pl.BlockSpec((1, l, GROUP_W), lambda bi, k=k: (bi, 0, col0 + k))
    return pl.pallas_call(
        functools.partial(_sconv_kernel, seq=l),
        out_shape=jax.ShapeDtypeStruct((b, l, GROUP_W), BF16),
        grid=(b,),
        in_specs=[spec(0), spec(1), spec(2), spec(3), pl.BlockSpec((SC_CONV, GROUP_W), lambda bi: (0, 0))],
        out_specs=pl.BlockSpec((1, l, GROUP_W), lambda bi: (bi, 0, 0)),
        scratch_shapes=[pltpu.VMEM((l + 16, GROUP_W), F32)],
        compiler_params=_cparams(("parallel",)),
        name="sconv",
    )(big, big, big, big, w)


def _gla_kernel(q_ref, k_ref, v_ref, gate_ref, sm_ref, wd_ref, bd_ref, nw_ref, s0f_ref, s0b_ref,
                *rest, seq, with_out):
    if with_out:
        y_ref, stf_ref, stb_ref, o_ref, st_ref = rest
    else:
        stf_ref, stb_ref, st_ref = rest
        y_ref = o_ref = None
    c = CHUNK
    n_chunks = seq // c
    ri = _iota2((c, c), 0)
    ci = _iota2((c, c), 1)
    incl = (ri >= ci, ri <= ci)
    tri = tuple(jnp.where(m, 1.0, 0.0).astype(BF16) for m in incl)
    qscale = GLA_DK ** -0.5
    last = (c - 1, 0)
    chains = [(d, h) for d in (0, 1) for h in range(N_HEADS)]

    def step(i, _):
        rows = [pl.ds(pl.multiple_of(n * c, c), c) for n in (i, n_chunks - 1 - i)]
        tots, kos, sts, vs, qps, kps, qis, stbs = [], [], [], [], [], [], [], []
        for d in (0, 1):
            lr = sm_ref[0, rows[d], :][:, SM_LR[d]:SM_LR[d] + GLA_RANK]
            z = _dot(lr, wd_ref[d]) + bd_ref[d]
            la = _log_sigmoid(z) * (1.0 / GLA_TAU)
            cum = _dot_exact_lhs(tri[d], la)
            tot = cum[last[d]:last[d] + 1, :]
            kf = k_ref[0, rows[d], :].astype(F32)
            tots.append(tot)
            kos.append((kf * jnp.exp(tot - cum)).astype(BF16))
            sts.append(st_ref[d])
            vs.append(v_ref[0, rows[d], :])
            if with_out:
                mid = cum[c // 2:c // 2 + 1, :]
                qf = q_ref[0, rows[d], :].astype(F32) * qscale
                qps.append((qf * jnp.exp(cum - mid)).astype(BF16))
                kps.append((kf * jnp.exp(mid - cum)).astype(BF16))
                qis.append((qf * jnp.exp(cum)).astype(BF16))
                stbs.append(sts[d].astype(BF16))

        def dk(h):
            return slice(h * GLA_DK, (h + 1) * GLA_DK)

        def dv(h):
            return slice(h * HEAD_DIM, (h + 1) * HEAD_DIM)

        if with_out:
            attn = [jnp.where(incl[d], _dot_nt(qps[d][:, dk(h)], kps[d][:, dk(h)]), 0.0) for d, h in chains]
            inter = [_dot_nt(qis[d][:, dk(h)], stbs[d][:, dk(h)]) for d, h in chains]
            intra = [_dot(a, vs[d][:, dv(h)]) for a, (d, h) in zip(attn, chains)]
            for (d, h), o1, o2 in zip(chains, inter, intra):
                o_ref[d, rows[d], dv(h)] = o1 + o2
        upd = [_dot_tn(vs[d][:, dv(h)], kos[d][:, dk(h)]) for d, h in chains]
        for d in (0, 1):
            st_ref[d] = sts[d] * jnp.exp(tots[d]) + jnp.concatenate(upd[d * N_HEADS:(d + 1) * N_HEADS], axis=1)
        return _

    st_ref[0] = s0f_ref[0]
    st_ref[1] = s0b_ref[0]
    lax.fori_loop(0, n_chunks, step, None)
    stf_ref[0] = st_ref[0]
    stb_ref[0] = st_ref[1]

    if with_out:
        def finish(n, _):
            rows = pl.ds(pl.multiple_of(n * c, c), c)
            for h in range(N_HEADS):
                hs = slice(h * HEAD_DIM, (h + 1) * HEAD_DIM)
                o = o_ref[0, rows, hs] + o_ref[1, rows, hs]
                o = o * lax.rsqrt(jnp.mean(o * o, axis=-1, keepdims=True) + EPS) * nw_ref[...]
                y_ref[0, rows, hs] = (o * _silu(gate_ref[0, rows, hs].astype(F32))).astype(BF16)
            return _

        lax.fori_loop(0, n_chunks, finish, None)


def _gla(big, small, wd, bd, nw, s0f, s0b, *, with_out):
    b, l, _ = big.shape
    st_shape = jax.ShapeDtypeStruct((b, HEAD_DIM, N_HEADS * GLA_DK), F32)
    st_spec = pl.BlockSpec((1, HEAD_DIM, N_HEADS * GLA_DK), lambda bi: (bi, 0, 0))
    in_specs = [pl.BlockSpec((1, l, 256), lambda bi: (bi, 0, 0)),
                pl.BlockSpec((1, l, 256), lambda bi: (bi, 0, 1)),
                pl.BlockSpec((1, l, 512), lambda bi: (bi, 0, 1)),
                pl.BlockSpec((1, l, 512), lambda bi: (bi, 0, 2)),
                pl.BlockSpec((1, l, SMALL_W), lambda bi: (bi, 0, 0)),
                pl.BlockSpec((2, GLA_RANK, 256), lambda bi: (0, 0, 0)),
                pl.BlockSpec((2, 1, 256), lambda bi: (0, 0, 0)),
                pl.BlockSpec((1, HEAD_DIM), lambda bi: (0, 0)),
                st_spec, st_spec]
    out_shape = [st_shape, st_shape]
    out_specs = [st_spec, st_spec]
    scratch = [pltpu.VMEM((2, HEAD_DIM, N_HEADS * GLA_DK), F32)]
    if with_out:
        out_shape = [jax.ShapeDtypeStruct((b, l, GROUP_W), BF16)] + out_shape
        out_specs = [pl.BlockSpec((1, l, GROUP_W), lambda bi: (bi, 0, 0))] + out_specs
        scratch = [pltpu.VMEM((2, l, GROUP_W), F32)] + scratch
    res = pl.pallas_call(
        functools.partial(_gla_kernel, seq=l, with_out=with_out),
        out_shape=tuple(out_shape), grid=(b,), in_specs=in_specs, out_specs=tuple(out_specs),
        scratch_shapes=scratch, compiler_params=_cparams(("parallel",)),
        name="gla_out" if with_out else "gla_state",
    )(big, big, big, big, small, wd, bd.reshape(2, 1, -1), nw.reshape(1, -1), s0f, s0b)
    return res if with_out else (None,) + tuple(res)


def _unit_triangular_inverses(lows, ri, ci):
    c = lows[0].shape[0]
    eye = jnp.where(ri == ci, 1.0, 0.0)
    pair = (ri // 2) == (ci // 2)
    ts = [eye - jnp.where(pair, low, 0.0) for low in lows]
    s = 2
    while s < c:
        sel = ((ri // (2 * s)) == (ci // (2 * s))) & ((ri // s) != (ci // s))
        tb = [t.astype(BF16) for t in ts]
        ps = [_dot(t, jnp.where(sel, low, 0.0)) for t, low in zip(tb, lows)]
        ts = [t - _dot(p, t16) for t, p, t16 in zip(ts, ps, tb)]
        s *= 2
    return ts


def _gdn_kernel(qkv_ref, gate_ref, sm_ref, cw_ref, gp_ref, nw_ref, s0f_ref, s0b_ref,
                *rest, seq, with_out):
    if with_out:
        y_ref, stf_ref, stb_ref, x_ref, o_ref, st_ref = rest
    else:
        stf_ref, stb_ref, x_ref, st_ref = rest
        y_ref = o_ref = None
    c = CHUNK
    n_chunks = seq // c
    width = 3 * GROUP_W
    ri = _iota2((c, c), 0)
    ci = _iota2((c, c), 1)
    lower = ri >= ci
    upper = ri <= ci
    tril = jnp.where(lower, 1.0, 0.0).astype(BF16)
    triu = jnp.where(upper, 1.0, 0.0).astype(BF16)

    def conv_chunk(n, _):
        r0 = pl.multiple_of(n * c, c)
        prev0 = pl.multiple_of(jnp.maximum(r0 - 16, 0), 16)
        next0 = pl.multiple_of(jnp.minimum(r0 + c, seq - 16), 16)
        prev = jnp.where(n > 0, qkv_ref[0, pl.ds(prev0, 16), :].astype(F32), 0.0)
        nxt = jnp.where(n < n_chunks - 1, qkv_ref[0, pl.ds(next0, 16), :].astype(F32), 0.0)
        xe = jnp.concatenate([prev, qkv_ref[0, pl.ds(r0, c), :].astype(F32), nxt], axis=0)
        acc = xe[14:14 + c, :] * cw_ref[0:1, :]
        for j in range(1, GDN_CONV):
            acc = acc + xe[14 + j:14 + j + c, :] * cw_ref[j:j + 1, :]
        acc = _silu(acc)
        for h in range(2 * N_HEADS):
            hs = slice(h * HEAD_DIM, (h + 1) * HEAD_DIM)
            t = acc[:, hs]
            t = t * lax.rsqrt(jnp.sum(t * t, axis=-1, keepdims=True) + EPS)
            if h < N_HEADS:
                t = t * (HEAD_DIM ** -0.5)
            x_ref[pl.ds(r0, c), hs] = t
        x_ref[pl.ds(r0, c), 2 * GROUP_W:width] = acc[:, 2 * GROUP_W:width]
        return _

    lax.fori_loop(0, n_chunks, conv_chunk, None)

    chains = [(d, h) for d in (0, 1) for h in range(N_HEADS)]
    n_state = N_HEADS * HEAD_DIM

    def step(i, _):
        rows = [pl.ds(pl.multiple_of(n * c, c), c) for n in (i, n_chunks - 1 - i)]
        cums, cum_rows, betas = [], [], []
        for d in (0, 1):
            sm = sm_ref[0, rows[d], :]
            g = gp_ref[0:1, :] * _softplus(sm + gp_ref[1:2, :])
            betas.append(_sigmoid(sm))
            cums.append(_dot_exact_lhs(tril if d == 0 else triu, g))
            cum_rows.append(_dot_exact_rhs(g.T, triu if d == 0 else tril))
        incl = (lower, upper)
        strict = (ri > ci, ri < ci)
        last = (c - 1, 0)

        kfs, vbs, kbs, e_cs, tots, decays, kos, qfs = [], [], [], [], [], [], [], []
        for d, h in chains:
            la, lb = SM_A[d] + h, SM_B[d] + h
            cum_c = cums[d][:, la:la + 1]
            cum_r = cum_rows[d][la:la + 1, :]
            tot = cums[d][last[d]:last[d] + 1, la:la + 1]
            beta_c = betas[d][:, lb:lb + 1]
            kf = x_ref[rows[d], GROUP_W + h * HEAD_DIM:GROUP_W + (h + 1) * HEAD_DIM]
            vf = x_ref[rows[d], 2 * GROUP_W + h * HEAD_DIM:2 * GROUP_W + (h + 1) * HEAD_DIM]
            e_c = jnp.exp(cum_c)
            kfs.append(kf)
            kbs.append(kf * beta_c)
            vbs.append(vf * beta_c)
            e_cs.append(e_c)
            tots.append(tot)
            decays.append(jnp.where(incl[d], jnp.exp(jnp.minimum(cum_c - cum_r, 0.0)), 0.0))
            kos.append(kf * jnp.exp(tot - cum_c))
            if with_out:
                qfs.append(x_ref[rows[d], h * HEAD_DIM:(h + 1) * HEAD_DIM])

        if with_out:
            prods = [_dot_nt(jnp.concatenate([kb, qf], axis=0), kf) for kb, qf, kf in zip(kbs, qfs, kfs)]
            attns = [p[c:, :] * dec for p, dec in zip(prods, decays)]
        else:
            prods = [_dot_nt(kb, kf) for kb, kf in zip(kbs, kfs)]
        lows = [jnp.where(strict[d], p[:c, :] * dec, 0.0) for (d, _), p, dec in zip(chains, prods, decays)]
        ts = _unit_triangular_inverses(lows, ri, ci)
        uws = [_dot(t, jnp.concatenate([vb, kb * e_c], axis=1)) for t, vb, kb, e_c in zip(ts, vbs, kbs, e_cs)]
        sts = [st_ref[j * HEAD_DIM:(j + 1) * HEAD_DIM, :] for j in range(len(chains))]
        if with_out:
            reads = [_dot(jnp.concatenate([uw[:, HEAD_DIM:], qf * e_c], axis=0), st)
                     for uw, qf, e_c, st in zip(uws, qfs, e_cs, sts)]
        else:
            reads = [_dot(uw[:, HEAD_DIM:], st) for uw, st in zip(uws, sts)]
        v_news = [uw[:, :HEAD_DIM] - r[:c, :] for uw, r in zip(uws, reads)]
        if with_out:
            writes = [_dot(jnp.concatenate([a, ko.T], axis=0), vn) for a, ko, vn in zip(attns, kos, v_news)]
        else:
            writes = [_dot_tn(ko, vn) for ko, vn in zip(kos, v_news)]
        for j, (d, h) in enumerate(chains):
            if with_out:
                o_ref[d, rows[d], h * HEAD_DIM:(h + 1) * HEAD_DIM] = reads[j][c:, :] + writes[j][:c, :]
                upd = writes[j][c:, :]
            else:
                upd = writes[j]
            st_ref[j * HEAD_DIM:(j + 1) * HEAD_DIM, :] = sts[j] * jnp.exp(tots[j]) + upd
        return _

    st_ref[0:n_state, :] = s0f_ref[0]
    st_ref[n_state:2 * n_state, :] = s0b_ref[0]
    lax.fori_loop(0, n_chunks, step, None)
    stf_ref[0] = st_ref[0:n_state, :]
    stb_ref[0] = st_ref[n_state:2 * n_state, :]

    if with_out:
        def finish(n, _):
            rows = pl.ds(pl.multiple_of(n * c, c), c)
            for h in range(N_HEADS):
                hs = slice(h * HEAD_DIM, (h + 1) * HEAD_DIM)
                o = o_ref[0, rows, hs] + o_ref[1, rows, hs]
                o = o * lax.rsqrt(jnp.mean(o * o, axis=-1, keepdims=True) + EPS) * nw_ref[...]
                y_ref[0, rows, hs] = (o * _silu(gate_ref[0, rows, hs].astype(F32))).astype(BF16)
            return _

        lax.fori_loop(0, n_chunks, finish, None)


def _gdn(big, small, cw, gp, nw, s0f, s0b, *, with_out):
    b, l, _ = big.shape
    st_shape = jax.ShapeDtypeStruct((b, N_HEADS * HEAD_DIM, HEAD_DIM), F32)
    st_spec = pl.BlockSpec((1, N_HEADS * HEAD_DIM, HEAD_DIM), lambda bi: (bi, 0, 0))
    in_specs = [pl.BlockSpec((1, l, 3 * GROUP_W), lambda bi: (bi, 0, 1)),
                pl.BlockSpec((1, l, GROUP_W), lambda bi: (bi, 0, 6)),
                pl.BlockSpec((1, l, SMALL_W), lambda bi: (bi, 0, 0)),
                pl.BlockSpec((GDN_CONV, 3 * GROUP_W), lambda bi: (0, 0)),
                pl.BlockSpec((8, SMALL_W), lambda bi: (0, 0)),
                pl.BlockSpec((1, HEAD_DIM), lambda bi: (0, 0)),
                st_spec, st_spec]
    out_shape = [st_shape, st_shape]
    out_specs = [st_spec, st_spec]
    scratch = [pltpu.VMEM((l, 3 * GROUP_W), F32), pltpu.VMEM((2 * N_HEADS * HEAD_DIM, HEAD_DIM), F32)]
    if with_out:
        out_shape = [jax.ShapeDtypeStruct((b, l, GROUP_W), BF16)] + out_shape
        out_specs = [pl.BlockSpec((1, l, GROUP_W), lambda bi: (bi, 0, 0))] + out_specs
        scratch = [scratch[0], pltpu.VMEM((2, l, GROUP_W), F32), scratch[1]]
    res = pl.pallas_call(
        functools.partial(_gdn_kernel, seq=l, with_out=with_out),
        out_shape=tuple(out_shape), grid=(b,), in_specs=in_specs, out_specs=tuple(out_specs),
        scratch_shapes=scratch, compiler_params=_cparams(("parallel",)),
        name="gdn_out" if with_out else "gdn_state",
    )(big, big, small, cw, gp, nw.reshape(1, -1), s0f, s0b)
    return res if with_out else (None,) + tuple(res)


def _gdn_gate_params(a_log, dt_bias):
    gp = jnp.zeros((8, SMALL_W), F32)
    gp = gp.at[0, SM_A[0]:SM_A[0] + 2 * N_HEADS].set(-jnp.exp(a_log.astype(F32)).reshape(-1))
    gp = gp.at[1, SM_A[0]:SM_A[0] + 2 * N_HEADS].set(dt_bias.astype(F32).reshape(-1))
    return gp


def _rope(x, cos, sin, lane):
    swapped = jnp.where((lane % 64) < 32, pltpu.roll(x, 96, 1), pltpu.roll(x, 32, 1))
    return x * cos + swapped * sin


def _softmax_sink_pv(s, sink_col, vv):
    m = jnp.maximum(jnp.max(s, axis=-1, keepdims=True), sink_col)
    p = jnp.exp(s - m)
    den = jnp.sum(p, axis=-1, keepdims=True) + jnp.exp(sink_col - m)
    return _dot(p, vv) * (1.0 / den)


def _swa_kernel(*refs, seq, ctx_len, with_ctx):
    if with_ctx:
        (q_ref, k_ref, v_ref, gate_ref, kc_ref, vc_ref, cos_ref, sin_ref, sink_ref, qc_ref, gatec_ref,
         y_ref, yc_ref, kr_ref) = refs
    else:
        (q_ref, k_ref, v_ref, gate_ref, kc_ref, vc_ref, cos_ref, sin_ref, sink_ref,
         y_ref, kr_ref) = refs
    blk = SWA_BLOCK
    nb = seq // blk
    win = 3 * blk
    grp = N_HEADS // SWA_KV_HEADS
    scale = HEAD_DIM ** -0.5
    lane = _iota2((blk, HEAD_DIM), 1)

    def rope_k(n, _):
        rows = pl.ds(pl.multiple_of(n * blk, blk), blk)
        cos = cos_ref[rows, :]
        sin = sin_ref[rows, :]
        for g in range(SWA_KV_HEADS):
            hs = slice(g * HEAD_DIM, (g + 1) * HEAD_DIM)
            kr_ref[rows, hs] = _rope(k_ref[0, rows, hs].astype(F32), cos, sin, lane).astype(BF16)
        return _

    lax.fori_loop(0, nb, rope_k, None)

    rowq = _iota2((grp * blk, win + ctx_len), 0) % blk
    colk = _iota2((grp * blk, win + ctx_len), 1)
    is_ctx = colk >= win
    head_row = _iota2((grp * blk, 1), 0) // blk

    def sink_column(g, n_rows):
        hr = _iota2((grp * n_rows, 1), 0) // n_rows
        col = jnp.zeros((grp * n_rows, 1), F32)
        for j in range(grp):
            col = jnp.where(hr == j, sink_ref[0:1, g * grp + j:g * grp + j + 1], col)
        return col

    def q_block(n, _):
        r0 = pl.multiple_of(n * blk, blk)
        rows = pl.ds(r0, blk)
        k0 = pl.multiple_of(jnp.clip(r0 - blk, 0, seq - win), blk)
        cos = cos_ref[rows, :]
        sin = sin_ref[rows, :]
        valid = is_ctx | (jnp.abs(r0 + rowq - (k0 + colk)) <= SWA_WINDOW)
        for g in range(SWA_KV_HEADS):
            hs = slice(g * HEAD_DIM, (g + 1) * HEAD_DIM)
            qg = jnp.concatenate(
                [_rope(q_ref[0, rows, (g * grp + j) * HEAD_DIM:(g * grp + j + 1) * HEAD_DIM].astype(F32),
                       cos, sin, lane) for j in range(grp)], axis=0).astype(BF16)
            kk = jnp.concatenate([kr_ref[pl.ds(k0, win), hs], kc_ref[0, :, hs]], axis=0)
            vv = jnp.concatenate([v_ref[0, pl.ds(k0, win), hs], vc_ref[0, :, hs]], axis=0)
            s = jnp.where(valid, _dot_nt(qg, kk) * scale, NEG_BIG)
            o = _softmax_sink_pv(s, sink_column(g, blk), vv)
            for j in range(grp):
                cs = slice((g * grp + j) * HEAD_DIM, (g * grp + j + 1) * HEAD_DIM)
                y_ref[0, rows, cs] = (o[j * blk:(j + 1) * blk, :]
                                      * _silu(gate_ref[0, rows, cs].astype(F32))).astype(BF16)
        return _

    lax.fori_loop(0, nb, q_block, None)

    if with_ctx:
        for g in range(SWA_KV_HEADS):
            hs = slice(g * HEAD_DIM, (g + 1) * HEAD_DIM)
            qg = jnp.concatenate([qc_ref[0, :, (g * grp + j) * HEAD_DIM:(g * grp + j + 1) * HEAD_DIM]
                                  for j in range(grp)], axis=0)
            s = _dot_nt(qg, kc_ref[0, :, hs]) * scale
            o = _softmax_sink_pv(s, sink_column(g, ctx_len), vc_ref[0, :, hs])
            for j in range(grp):
                cs = slice((g * grp + j) * HEAD_DIM, (g * grp + j + 1) * HEAD_DIM)
                yc_ref[0, :, cs] = (o[j * ctx_len:(j + 1) * ctx_len, :]
                                    * _silu(gatec_ref[0, :, cs].astype(F32))).astype(BF16)


def _swa(big, bigc, cos, sin, sink, *, with_ctx):
    b, l, _ = big.shape
    lc = bigc.shape[1]
    in_specs = [pl.BlockSpec((1, l, 512), lambda bi: (bi, 0, 11)),
                pl.BlockSpec((1, l, 256), lambda bi: (bi, 0, 24)),
                pl.BlockSpec((1, l, 256), lambda bi: (bi, 0, 25)),
                pl.BlockSpec((1, l, 512), lambda bi: (bi, 0, 13)),
                pl.BlockSpec((1, lc, 256), lambda bi: (bi, 0, 24)),
                pl.BlockSpec((1, lc, 256), lambda bi: (bi, 0, 25)),
                pl.BlockSpec((l, HEAD_DIM), lambda bi: (0, 0)),
                pl.BlockSpec((l, HEAD_DIM), lambda bi: (0, 0)),
                pl.BlockSpec((1, LANES), lambda bi: (0, 0))]
    args = [big, big, big, big, bigc, bigc, cos, sin, sink]
    out_shape = [jax.ShapeDtypeStruct((b, l, GROUP_W), BF16)]
    out_specs = [pl.BlockSpec((1, l, GROUP_W), lambda bi: (bi, 0, 0))]
    if with_ctx:
        in_specs += [pl.BlockSpec((1, lc, 512), lambda bi: (bi, 0, 11)),
                     pl.BlockSpec((1, lc, 512), lambda bi: (bi, 0, 13))]
        args += [bigc, bigc]
        out_shape.append(jax.ShapeDtypeStruct((b, lc, GROUP_W), BF16))
        out_specs.append(pl.BlockSpec((1, lc, GROUP_W), lambda bi: (bi, 0, 0)))
    res = pl.pallas_call(
        functools.partial(_swa_kernel, seq=l, ctx_len=lc, with_ctx=with_ctx),
        out_shape=tuple(out_shape), grid=(b,), in_specs=in_specs, out_specs=tuple(out_specs),
        scratch_shapes=[pltpu.VMEM((l, SWA_KV_HEADS * HEAD_DIM), BF16)],
        compiler_params=_cparams(("parallel",)),
        name="swa_ctx" if with_ctx else "swa",
    )(*args)
    return (res[0], res[1]) if with_ctx else (res[0], None)


def _rope_tables(seq_len):
    rows = seq_len // GRID_W
    row = jnp.repeat(jnp.arange(rows, dtype=F32), GRID_W)
    col = jnp.tile(jnp.arange(GRID_W, dtype=F32), rows)
    axis_dim = HEAD_DIM // 2
    inv_freq = ROPE_BASE ** (-jnp.arange(0, axis_dim, 2, dtype=F32) / axis_dim)
    ang_r = row[:, None] * inv_freq
    ang_c = col[:, None] * inv_freq
    cos = jnp.concatenate([jnp.cos(ang_r), jnp.cos(ang_r), jnp.cos(ang_c), jnp.cos(ang_c)], axis=-1)
    sin = jnp.concatenate([-jnp.sin(ang_r), jnp.sin(ang_r), -jnp.sin(ang_c), jnp.sin(ang_c)], axis=-1)
    return cos, sin


def _pack_w_in(w):
    off, cols = 0, {}
    for name, size in _SPLITS:
        cols[name] = w[:, off:off + size]
        off += size
    big = jnp.concatenate([cols[n] for n in _BIG_ORDER], axis=1).astype(BF16)
    small = jnp.concatenate([cols[n] for n in _SMALL_ORDER], axis=1)
    small = jnp.pad(small, ((0, 0), (0, SMALL_W - small.shape[1]))).astype(BF16)
    return big, small


def kernel(x, c, ctx, c_ctx, ada_w, ada_b, norm_pre, norm_post, w_in, w_out, gla_w_decay, gla_b_decay,
           gla_norm, gdn_conv, gdn_a_log, gdn_dt_bias, gdn_norm, sc_conv, swa_sink):
    b, l, d = x.shape
    cos, sin = _rope_tables(l)
    cc = jnp.zeros((8, d), F32).at[:b].set(c).at[b].set(c_ctx)
    mod = _ada(cc, ada_w, ada_b)
    h, hc = x, ctx
    n_layers = ada_w.shape[0]
    for layer in range(n_layers):
        with_ctx = layer < n_layers - 1
        shift, scale, gate = (mod[layer, :, i * d:(i + 1) * d] for i in range(3))

        def per_batch(t):
            return t[:b, None, :], jnp.broadcast_to(t[b][None, None, :], (b, 1, d))
        (shift_l, shift_c), (scale_l, scale_c), (gate_l, gate_c) = per_batch(shift), per_batch(scale), per_batch(gate)
        w_big, w_small = _pack_w_in(w_in[layer])
        g_pre = norm_pre[layer].reshape(1, d)
        big, small = _inproj(h, scale_l, shift_l, g_pre, w_big, w_small, tm=min(l, 1024))
        bigc, smallc = _inproj(hc, scale_c, shift_c, g_pre, w_big, w_small, tm=hc.shape[1])

        zf = jnp.zeros((b, HEAD_DIM, N_HEADS * GLA_DK), F32)
        yca, stf, stb = _gla(bigc, smallc, gla_w_decay[layer], gla_b_decay[layer], gla_norm[layer], zf, zf,
                             with_out=with_ctx)
        ya, _, _ = _gla(big, small, gla_w_decay[layer], gla_b_decay[layer], gla_norm[layer], stf, stb,
                        with_out=True)

        gp = _gdn_gate_params(gdn_a_log[layer], gdn_dt_bias[layer])
        zg = jnp.zeros((b, N_HEADS * HEAD_DIM, HEAD_DIM), F32)
        ycb, gtf, gtb = _gdn(bigc, smallc, gdn_conv[layer], gp, gdn_norm[layer], zg, zg, with_out=with_ctx)
        yb, _, _ = _gdn(big, small, gdn_conv[layer], gp, gdn_norm[layer], gtf, gtb, with_out=True)

        yc_ = _sconv(big, sc_conv[layer])
        sink = jnp.zeros((1, LANES), F32).at[0, :N_HEADS].set(swa_sink[layer])
        yd, ycd = _swa(big, bigc, cos, sin, sink, with_ctx=with_ctx)

        w_o = w_out[layer].astype(BF16)
        g_post = norm_post[layer].reshape(1, d)
        h_new = _outproj(ya, yb, yc_, yd, w_o, h, gate_l, g_post)
        if with_ctx:
            ycc = _sconv(bigc, sc_conv[layer])
            hc = _outproj(yca, ycb, ycc, ycd, w_o, hc, gate_c, g_post)
        h = h_new
    return h
```

```python
import functools

import jax
import jax.numpy as jnp
from jax import lax
from jax.experimental import pallas as pl
from jax.experimental.pallas import tpu as pltpu

F32 = jnp.float32
BF16 = jnp.bfloat16

D_MODEL = 2048
N_LAYERS = 2
GRID_W = 64
GROUP_W = 512
HEAD_DIM = 128
N_HEADS = 4
EPS = 1e-6
GLA_DK = 64
GLA_RANK = 16
GLA_TAU = 16.0
GDN_CONV = 5
SC_CONV = 3
SWA_KV_HEADS = 2
SWA_WINDOW = 128
ROPE_BASE = 10000.0

_SPLITS = (
    ("gla_q", 256), ("gla_k", 256), ("gla_v", 512), ("gla_lr_f", 16), ("gla_lr_b", 16), ("gla_gate", 512),
    ("gdn_q", 512), ("gdn_k", 512), ("gdn_v", 512),
    ("gdn_a_f", 4), ("gdn_a_b", 4), ("gdn_b_f", 4), ("gdn_b_b", 4), ("gdn_gate", 512),
    ("sc_b", 512), ("sc_c", 512), ("sc_h", 512), ("sc_gate", 512),
    ("swa_q", 512), ("swa_k", 256), ("swa_v", 256), ("swa_gate", 512),
)
_BIG_ORDER = ("gla_q", "gla_k", "gla_v", "gla_gate", "gdn_q", "gdn_k", "gdn_v", "gdn_gate",
              "sc_b", "sc_c", "sc_h", "sc_gate", "swa_q", "swa_k", "swa_v", "swa_gate")
_SMALL_ORDER = ("gla_lr_f", "gla_lr_b", "gdn_a_f", "gdn_a_b", "gdn_b_f", "gdn_b_b")
BIG_W = 7168
SMALL_W = 128
SM_LR = (0, 16)
SM_A = (32, 36)
SM_B = (40, 44)

LANES = 128
VMEM_LIMIT = 56 * 1024 * 1024

CHUNK = 128
SWA_BLOCK = 128

NEG_BIG = -1e30


def _cparams(sem):
    return pltpu.CompilerParams(dimension_semantics=sem, vmem_limit_bytes=VMEM_LIMIT)


def _dot(a, b):
    return lax.dot_general(a.astype(BF16), b.astype(BF16), (((1,), (0,)), ((), ())),
                           preferred_element_type=F32)


def _dot_nt(a, b):
    return lax.dot_general(a.astype(BF16), b.astype(BF16), (((1,), (1,)), ((), ())),
                           preferred_element_type=F32)


def _dot_tn(a, b):
    return lax.dot_general(a.astype(BF16), b.astype(BF16), (((0,), (0,)), ((), ())),
                           preferred_element_type=F32)


def _split(x):
    hi = x.astype(BF16)
    lo = (x - hi.astype(F32)).astype(BF16)
    return hi, lo


def _dot_exact_lhs(a_bf16, x):
    hi, lo = _split(x)
    return _dot(a_bf16, hi) + _dot(a_bf16, lo)


def _dot_exact_rhs(x, b_bf16):
    hi, lo = _split(x)
    return _dot(hi, b_bf16) + _dot(lo, b_bf16)


def _silu(x):
    return x * (1.0 / (1.0 + jnp.exp(-x)))


def _sigmoid(x):
    return 1.0 / (1.0 + jnp.exp(-x))


def _softplus(x):
    return jnp.maximum(x, 0.0) + jnp.log(1.0 + jnp.exp(-jnp.abs(x)))


def _log_sigmoid(x):
    return jnp.minimum(x, 0.0) - jnp.log(1.0 + jnp.exp(-jnp.abs(x)))


def _iota2(shape, dim):
    return lax.broadcasted_iota(jnp.int32, shape, dim)


ADA_K_SPLIT = 4


def _ada_kernel(c_ref, *refs):
    w_refs, (b_ref, o_ref) = refs[:ADA_K_SPLIT], refs[ADA_K_SPLIT:]
    a = _silu(c_ref[...])
    kq = a.shape[1] // ADA_K_SPLIT
    acc = b_ref[0]
    for q, w_ref in enumerate(w_refs):
        acc = acc + _dot(a[:, q * kq:(q + 1) * kq], w_ref[0])
    o_ref[0] = acc


def _ada(cc, ada_w, ada_b):
    n_layers, d, n = ada_w.shape
    tn = 1024
    kq = d // ADA_K_SPLIT
    w_specs = [pl.BlockSpec((1, kq, tn), lambda l, j, q=q: (l, q, j)) for q in range(ADA_K_SPLIT)]
    return pl.pallas_call(
        _ada_kernel,
        out_shape=jax.ShapeDtypeStruct((n_layers, 8, n), F32),
        grid=(n_layers, n // tn),
        in_specs=[pl.BlockSpec((8, d), lambda l, j: (0, 0))] + w_specs
        + [pl.BlockSpec((1, 1, tn), lambda l, j: (l, 0, j))],
        out_specs=pl.BlockSpec((1, 8, tn), lambda l, j: (l, 0, j)),
        compiler_params=_cparams(("parallel", "parallel")),
        name="ada_mod",
    )(cc, *([ada_w] * ADA_K_SPLIT), ada_b.reshape(n_layers, 1, n))


def _inproj_kernel(x_ref, sc_ref, sh_ref, g_ref, wb_ref, ws_ref, ob_ref, os_ref, xn_ref):
    @pl.when(pl.program_id(2) == 0)
    def _():
        rows_per_step = 128

        def norm_rows(i, _):
            rows = pl.ds(pl.multiple_of(i * rows_per_step, rows_per_step), rows_per_step)
            x = x_ref[0, rows, :]
            y = x * lax.rsqrt(jnp.mean(x * x, axis=-1, keepdims=True) + EPS) * g_ref[...]
            xn_ref[rows, :] = (y * (1.0 + sc_ref[0]) + sh_ref[0]).astype(BF16)
            return _

        lax.fori_loop(0, xn_ref.shape[0] // rows_per_step, norm_rows, None)
        os_ref[0] = jnp.dot(xn_ref[...], ws_ref[0], preferred_element_type=F32)

    ob_ref[0] = jnp.dot(xn_ref[...], wb_ref[0], preferred_element_type=F32).astype(BF16)


def _inproj(x, scale, shift, g, w_big, w_small, layer, *, tm, tn=1024):
    b, l, d = x.shape
    return pl.pallas_call(
        _inproj_kernel,
        out_shape=(jax.ShapeDtypeStruct((b, l, BIG_W), BF16), jax.ShapeDtypeStruct((b, l, SMALL_W), F32)),
        grid=(b, l // tm, BIG_W // tn),
        in_specs=[pl.BlockSpec((1, tm, d), lambda bi, i, j: (bi, i, 0)),
                  pl.BlockSpec((1, 1, d), lambda bi, i, j: (bi, 0, 0)),
                  pl.BlockSpec((1, 1, d), lambda bi, i, j: (bi, 0, 0)),
                  pl.BlockSpec((1, d), lambda bi, i, j: (0, 0)),
                  pl.BlockSpec((1, d, tn), lambda bi, i, j: (layer, 0, j)),
                  pl.BlockSpec((1, d, SMALL_W), lambda bi, i, j: (layer, 0, 0))],
        out_specs=(pl.BlockSpec((1, tm, tn), lambda bi, i, j: (bi, i, j)),
                   pl.BlockSpec((1, tm, SMALL_W), lambda bi, i, j: (bi, i, 0))),
        scratch_shapes=[pltpu.VMEM((tm, d), BF16)],
        compiler_params=_cparams(("parallel", "parallel", "arbitrary")),
        name="inproj",
    )(x, scale, shift, g, w_big, w_small)


def _outproj_kernel(ya_ref, yb_ref, yc_ref, yd_ref, w_ref, h_ref, gate_ref, g_ref, o_ref):
    y = jnp.dot(ya_ref[0], w_ref[0, 0:GROUP_W, :], preferred_element_type=F32)
    y += jnp.dot(yb_ref[0], w_ref[0, GROUP_W:2 * GROUP_W, :], preferred_element_type=F32)
    y += jnp.dot(yc_ref[0], w_ref[0, 2 * GROUP_W:3 * GROUP_W, :], preferred_element_type=F32)
    y += jnp.dot(yd_ref[0], w_ref[0, 3 * GROUP_W:4 * GROUP_W, :], preferred_element_type=F32)
    yn = y * lax.rsqrt(jnp.mean(y * y, axis=-1, keepdims=True) + EPS) * g_ref[...]
    o_ref[0] = h_ref[0] + gate_ref[0] * yn


def _outproj(ya, yb, yc, yd, w_out, layer, h, gate, g, *, tm):
    b, l, d = h.shape
    yspec = pl.BlockSpec((1, tm, GROUP_W), lambda bi, i: (bi, i, 0))
    return pl.pallas_call(
        _outproj_kernel,
        out_shape=jax.ShapeDtypeStruct((b, l, d), F32),
        grid=(b, l // tm),
        in_specs=[yspec, yspec, yspec, yspec,
                  pl.BlockSpec((1, d, d), lambda bi, i: (layer, 0, 0)),
                  pl.BlockSpec((1, tm, d), lambda bi, i: (bi, i, 0)),
                  pl.BlockSpec((1, 1, d), lambda bi, i: (bi, 0, 0)),
                  pl.BlockSpec((1, d), lambda bi, i: (0, 0))],
        out_specs=pl.BlockSpec((1, tm, d), lambda bi, i: (bi, i, 0)),
        compiler_params=_cparams(("parallel", "parallel")),
        name="outproj",
    )(ya, yb, yc, yd, w_out, h, gate, g)


def _sconv_kernel(b_ref, c_ref, h_ref, gate_ref, w_ref, o_ref, p_ref, *, seq):
    zeros = jnp.zeros((8, GROUP_W), F32)
    p_ref[0:8, :] = zeros
    p_ref[seq + 8:seq + 16, :] = zeros
    p_ref[8:seq + 8, :] = c_ref[0].astype(F32) * h_ref[0].astype(F32)
    rows = min(seq, 256)
    w = w_ref[...]
    for t0 in range(0, seq, rows):
        conv = (p_ref[t0 + 7:t0 + 7 + rows, :] * w[0:1, :] + p_ref[t0 + 8:t0 + 8 + rows, :] * w[1:2, :]
                + p_ref[t0 + 9:t0 + 9 + rows, :] * w[2:3, :])
        y = b_ref[0, t0:t0 + rows, :].astype(F32) * conv * _silu(gate_ref[0, t0:t0 + rows, :].astype(F32))
        o_ref[0, t0:t0 + rows, :] = y.astype(BF16)


def _sconv(big, w):
    b, l, _ = big.shape
    col0 = 3584 // GROUP_W

    def spec(k):
        return pl.BlockSpec((1, l, GROUP_W), lambda bi, k=k: (bi, 0, col0 + k))
    return pl.pallas_call(
        functools.partial(_sconv_kernel, seq=l),
        out_shape=jax.ShapeDtypeStruct((b, l, GROUP_W), BF16),
        grid=(b,),
        in_specs=[spec(0), spec(1), spec(2), spec(3), pl.BlockSpec((SC_CONV, GROUP_W), lambda bi: (0, 0))],
        out_specs=pl.BlockSpec((1, l, GROUP_W), lambda bi: (bi, 0, 0)),
        scratch_shapes=[pltpu.VMEM((l + 16, GROUP_W), F32)],
        compiler_params=_cparams(("parallel",)),
        name="sconv",
    )(big, big, big, big, w)


def _gla_kernel(q_ref, k_ref, v_ref, gate_ref, sm_ref, wd_ref, bd_ref, nw_ref, s0f_ref, s0b_ref,
                *rest, seq, with_out):
    if with_out:
        y_ref, stf_ref, stb_ref, o_ref, st_ref = rest
    else:
        stf_ref, stb_ref, st_ref = rest
        y_ref = o_ref = None
    c = CHUNK
    n_chunks = seq // c
    ri = _iota2((c, c), 0)
    ci = _iota2((c, c), 1)
    incl = (ri >= ci, ri <= ci)
    tri = tuple(jnp.where(m, 1.0, 0.0).astype(BF16) for m in incl)
    qscale = GLA_DK ** -0.5
    last = (c - 1, 0)
    chains = [(d, h) for d in (0, 1) for h in range(N_HEADS)]

    def step(i, _):
        rows = [pl.ds(pl.multiple_of(n * c, c), c) for n in (i, n_chunks - 1 - i)]
        tots, kos, sts, vs, qps, kps, qis, stbs = [], [], [], [], [], [], [], []
        for d in (0, 1):
            lr = sm_ref[0, rows[d], :][:, SM_LR[d]:SM_LR[d] + GLA_RANK]
            z = _dot(lr, wd_ref[d]) + bd_ref[d]
            la = _log_sigmoid(z) * (1.0 / GLA_TAU)
            cum = _dot_exact_lhs(tri[d], la)
            tot = cum[last[d]:last[d] + 1, :]
            kf = k_ref[0, rows[d], :].astype(F32)
            tots.append(tot)
            kos.append((kf * jnp.exp(tot - cum)).astype(BF16))
            sts.append(st_ref[d])
            vs.append(v_ref[0, rows[d], :])
            if with_out:
                mid = cum[c // 2:c // 2 + 1, :]
                qf = q_ref[0, rows[d], :].astype(F32) * qscale
                qps.append((qf * jnp.exp(cum - mid)).astype(BF16))
                kps.append((kf * jnp.exp(mid - cum)).astype(BF16))
                qis.append((qf * jnp.exp(cum)).astype(BF16))
                stbs.append(sts[d].astype(BF16))

        def dk(h):
            return slice(h * GLA_DK, (h + 1) * GLA_DK)

        def dv(h):
            return slice(h * HEAD_DIM, (h + 1) * HEAD_DIM)

        if with_out:
            attn = [jnp.where(incl[d], _dot_nt(qps[d][:, dk(h)], kps[d][:, dk(h)]), 0.0) for d, h in chains]
            inter = [_dot_nt(qis[d][:, dk(h)], stbs[d][:, dk(h)]) for d, h in chains]
            intra = [_dot(a, vs[d][:, dv(h)]) for a, (d, h) in zip(attn, chains)]
            for (d, h), o1, o2 in zip(chains, inter, intra):
                o_ref[d, rows[d], dv(h)] = o1 + o2
        upd = [_dot_tn(vs[d][:, dv(h)], kos[d][:, dk(h)]) for d, h in chains]
        for d in (0, 1):
            st_ref[d] = sts[d] * jnp.exp(tots[d]) + jnp.concatenate(upd[d * N_HEADS:(d + 1) * N_HEADS], axis=1)
        return _

    st_ref[0] = s0f_ref[0]
    st_ref[1] = s0b_ref[0]
    lax.fori_loop(0, n_chunks, step, None)
    stf_ref[0] = st_ref[0]
    stb_ref[0] = st_ref[1]

    if with_out:
        def finish(n, _):
            rows = pl.ds(pl.multiple_of(n * c, c), c)
            for h in range(N_HEADS):
                hs = slice(h * HEAD_DIM, (h + 1) * HEAD_DIM)
                o = o_ref[0, rows, hs] + o_ref[1, rows, hs]
                o = o * lax.rsqrt(jnp.mean(o * o, axis=-1, keepdims=True) + EPS) * nw_ref[...]
                y_ref[0, rows, hs] = (o * _silu(gate_ref[0, rows, hs].astype(F32))).astype(BF16)
            return _

        lax.fori_loop(0, n_chunks, finish, None)


def _gla(big, small, wd, bd, nw, s0f, s0b, *, with_out):
    b, l, _ = big.shape
    st_shape = jax.ShapeDtypeStruct((b, HEAD_DIM, N_HEADS * GLA_DK), F32)
    st_spec = pl.BlockSpec((1, HEAD_DIM, N_HEADS * GLA_DK), lambda bi: (bi, 0, 0))
    in_specs = [pl.BlockSpec((1, l, 256), lambda bi: (bi, 0, 0)),
                pl.BlockSpec((1, l, 256), lambda bi: (bi, 0, 1)),
                pl.BlockSpec((1, l, 512), lambda bi: (bi, 0, 1)),
                pl.BlockSpec((1, l, 512), lambda bi: (bi, 0, 2)),
                pl.BlockSpec((1, l, SMALL_W), lambda bi: (bi, 0, 0)),
                pl.BlockSpec((2, GLA_RANK, 256), lambda bi: (0, 0, 0)),
                pl.BlockSpec((2, 1, 256), lambda bi: (0, 0, 0)),
                pl.BlockSpec((1, HEAD_DIM), lambda bi: (0, 0)),
                st_spec, st_spec]
    out_shape = [st_shape, st_shape]
    out_specs = [st_spec, st_spec]
    scratch = [pltpu.VMEM((2, HEAD_DIM, N_HEADS * GLA_DK), F32)]
    if with_out:
        out_shape = [jax.ShapeDtypeStruct((b, l, GROUP_W), BF16)] + out_shape
        out_specs = [pl.BlockSpec((1, l, GROUP_W), lambda bi: (bi, 0, 0))] + out_specs
        scratch = [pltpu.VMEM((2, l, GROUP_W), F32)] + scratch
    res = pl.pallas_call(
        functools.partial(_gla_kernel, seq=l, with_out=with_out),
        out_shape=tuple(out_shape), grid=(b,), in_specs=in_specs, out_specs=tuple(out_specs),
        scratch_shapes=scratch, compiler_params=_cparams(("parallel",)),
        name="gla_out" if with_out else "gla_state",
    )(big, big, big, big, small, wd, bd.reshape(2, 1, -1), nw.reshape(1, -1), s0f, s0b)
    return res if with_out else (None,) + tuple(res)


def _unit_triangular_inverses(lows, ri, ci):
    c = lows[0].shape[0]
    eye = jnp.where(ri == ci, 1.0, 0.0)
    pair = (ri // 2) == (ci // 2)
    ts = [eye - jnp.where(pair, low, 0.0) for low in lows]
    s = 2
    while s < c:
        sel = ((ri // (2 * s)) == (ci // (2 * s))) & ((ri // s) != (ci // s))
        tb = [t.astype(BF16) for t in ts]
        ps = [_dot(t, jnp.where(sel, low, 0.0)) for t, low in zip(tb, lows)]
        ts = [t - _dot(p, t16) for t, p, t16 in zip(ts, ps, tb)]
        s *= 2
    return ts


def _gdn_kernel(qkv_ref, gate_ref, sm_ref, cw_ref, gp_ref, nw_ref, s0f_ref, s0b_ref,
                *rest, seq, with_out):
    if with_out:
        y_ref, stf_ref, stb_ref, x_ref, o_ref, st_ref = rest
    else:
        stf_ref, stb_ref, x_ref, st_ref = rest
        y_ref = o_ref = None
    c = CHUNK
    n_chunks = seq // c
    width = 3 * GROUP_W
    ri = _iota2((c, c), 0)
    ci = _iota2((c, c), 1)
    lower = ri >= ci
    upper = ri <= ci
    tril = jnp.where(lower, 1.0, 0.0).astype(BF16)
    triu = jnp.where(upper, 1.0, 0.0).astype(BF16)

    def conv_chunk(n, _):
        r0 = pl.multiple_of(n * c, c)
        prev0 = pl.multiple_of(jnp.maximum(r0 - 16, 0), 16)
        next0 = pl.multiple_of(jnp.minimum(r0 + c, seq - 16), 16)
        prev = jnp.where(n > 0, qkv_ref[0, pl.ds(prev0, 16), :].astype(F32), 0.0)
        nxt = jnp.where(n < n_chunks - 1, qkv_ref[0, pl.ds(next0, 16), :].astype(F32), 0.0)
        xe = jnp.concatenate([prev, qkv_ref[0, pl.ds(r0, c), :].astype(F32), nxt], axis=0)
        acc = xe[14:14 + c, :] * cw_ref[0:1, :]
        for j in range(1, GDN_CONV):
            acc = acc + xe[14 + j:14 + j + c, :] * cw_ref[j:j + 1, :]
        acc = _silu(acc)
        for h in range(2 * N_HEADS):
            hs = slice(h * HEAD_DIM, (h + 1) * HEAD_DIM)
            t = acc[:, hs]
            t = t * lax.rsqrt(jnp.sum(t * t, axis=-1, keepdims=True) + EPS)
            if h < N_HEADS:
                t = t * (HEAD_DIM ** -0.5)
            x_ref[pl.ds(r0, c), hs] = t
        x_ref[pl.ds(r0, c), 2 * GROUP_W:width] = acc[:, 2 * GROUP_W:width]
        return _

    lax.fori_loop(0, n_chunks, conv_chunk, None)

    chains = [(d, h) for d in (0, 1) for h in range(N_HEADS)]
    n_state = N_HEADS * HEAD_DIM

    def step(i, _):
        rows = [pl.ds(pl.multiple_of(n * c, c), c) for n in (i, n_chunks - 1 - i)]
        cums, cum_rows, betas = [], [], []
        for d in (0, 1):
            sm = sm_ref[0, rows[d], :]
            g = gp_ref[0:1, :] * _softplus(sm + gp_ref[1:2, :])
            betas.append(_sigmoid(sm))
            cums.append(_dot_exact_lhs(tril if d == 0 else triu, g))
            cum_rows.append(_dot_exact_rhs(g.T, triu if d == 0 else tril))
        incl = (lower, upper)
        strict = (ri > ci, ri < ci)
        last = (c - 1, 0)

        kfs, vbs, kbs, e_cs, tots, decays, kos, qfs = [], [], [], [], [], [], [], []
        for d, h in chains:
            la, lb = SM_A[d] + h, SM_B[d] + h
            cum_c = cums[d][:, la:la + 1]
            cum_r = cum_rows[d][la:la + 1, :]
            tot = cums[d][last[d]:last[d] + 1, la:la + 1]
            beta_c = betas[d][:, lb:lb + 1]
            kf = x_ref[rows[d], GROUP_W + h * HEAD_DIM:GROUP_W + (h + 1) * HEAD_DIM]
            vf = x_ref[rows[d], 2 * GROUP_W + h * HEAD_DIM:2 * GROUP_W + (h + 1) * HEAD_DIM]
            e_c = jnp.exp(cum_c)
            kfs.append(kf)
            kbs.append(kf * beta_c)
            vbs.append(vf * beta_c)
            e_cs.append(e_c)
            tots.append(tot)
            decays.append(jnp.where(incl[d], jnp.exp(jnp.minimum(cum_c - cum_r, 0.0)), 0.0))
            kos.append(kf * jnp.exp(tot - cum_c))
            if with_out:
                qfs.append(x_ref[rows[d], h * HEAD_DIM:(h + 1) * HEAD_DIM])

        if with_out:
            prods = [_dot_nt(jnp.concatenate([kb, qf], axis=0), kf) for kb, qf, kf in zip(kbs, qfs, kfs)]
            attns = [p[c:, :] * dec for p, dec in zip(prods, decays)]
        else:
            prods = [_dot_nt(kb, kf) for kb, kf in zip(kbs, kfs)]
        lows = [jnp.where(strict[d], p[:c, :] * dec, 0.0) for (d, _), p, dec in zip(chains, prods, decays)]
        ts = _unit_triangular_inverses(lows, ri, ci)
        uws = [_dot(t, jnp.concatenate([vb, kb * e_c], axis=1)) for t, vb, kb, e_c in zip(ts, vbs, kbs, e_cs)]
        sts = [st_ref[j * HEAD_DIM:(j + 1) * HEAD_DIM, :] for j in range(len(chains))]
        if with_out:
            reads = [_dot(jnp.concatenate([uw[:, HEAD_DIM:], qf * e_c], axis=0), st)
                     for uw, qf, e_c, st in zip(uws, qfs, e_cs, sts)]
        else:
            reads = [_dot(uw[:, HEAD_DIM:], st) for uw, st in zip(uws, sts)]
        v_news = [uw[:, :HEAD_DIM] - r[:c, :] for uw, r in zip(uws, reads)]
        if with_out:
            writes = [_dot(jnp.concatenate([a, ko.T], axis=0), vn) for a, ko, vn in zip(attns, kos, v_news)]
        else:
            writes = [_dot_tn(ko, vn) for ko, vn in zip(kos, v_news)]
        for j, (d, h) in enumerate(chains):
            if with_out:
                o_ref[d, rows[d], h * HEAD_DIM:(h + 1) * HEAD_DIM] = reads[j][c:, :] + writes[j][:c, :]
                upd = writes[j][c:, :]
            else:
                upd = writes[j]
            st_ref[j * HEAD_DIM:(j + 1) * HEAD_DIM, :] = sts[j] * jnp.exp(tots[j]) + upd
        return _

    st_ref[0:n_state, :] = s0f_ref[0]
    st_ref[n_state:2 * n_state, :] = s0b_ref[0]
    lax.fori_loop(0, n_chunks, step, None)
    stf_ref[0] = st_ref[0:n_state, :]
    stb_ref[0] = st_ref[n_state:2 * n_state, :]

    if with_out:
        def finish(n, _):
            rows = pl.ds(pl.multiple_of(n * c, c), c)
            for h in range(N_HEADS):
                hs = slice(h * HEAD_DIM, (h + 1) * HEAD_DIM)
                o = o_ref[0, rows, hs] + o_ref[1, rows, hs]
                o = o * lax.rsqrt(jnp.mean(o * o, axis=-1, keepdims=True) + EPS) * nw_ref[...]
                y_ref[0, rows, hs] = (o * _silu(gate_ref[0, rows, hs].astype(F32))).astype(BF16)
            return _

        lax.fori_loop(0, n_chunks, finish, None)


def _gdn(big, small, cw, gp, nw, s0f, s0b, *, with_out):
    b, l, _ = big.shape
    st_shape = jax.ShapeDtypeStruct((b, N_HEADS * HEAD_DIM, HEAD_DIM), F32)
    st_spec = pl.BlockSpec((1, N_HEADS * HEAD_DIM, HEAD_DIM), lambda bi: (bi, 0, 0))
    in_specs = [pl.BlockSpec((1, l, 3 * GROUP_W), lambda bi: (bi, 0, 1)),
                pl.BlockSpec((1, l, GROUP_W), lambda bi: (bi, 0, 6)),
                pl.BlockSpec((1, l, SMALL_W), lambda bi: (bi, 0, 0)),
                pl.BlockSpec((GDN_CONV, 3 * GROUP_W), lambda bi: (0, 0)),
                pl.BlockSpec((8, SMALL_W), lambda bi: (0, 0)),
                pl.BlockSpec((1, HEAD_DIM), lambda bi: (0, 0)),
                st_spec, st_spec]
    out_shape = [st_shape, st_shape]
    out_specs = [st_spec, st_spec]
    scratch = [pltpu.VMEM((l, 3 * GROUP_W), F32), pltpu.VMEM((2 * N_HEADS * HEAD_DIM, HEAD_DIM), F32)]
    if with_out:
        out_shape = [jax.ShapeDtypeStruct((b, l, GROUP_W), BF16)] + out_shape
        out_specs = [pl.BlockSpec((1, l, GROUP_W), lambda bi: (bi, 0, 0))] + out_specs
        scratch = [scratch[0], pltpu.VMEM((2, l, GROUP_W), F32), scratch[1]]
    res = pl.pallas_call(
        functools.partial(_gdn_kernel, seq=l, with_out=with_out),
        out_shape=tuple(out_shape), grid=(b,), in_specs=in_specs, out_specs=tuple(out_specs),
        scratch_shapes=scratch, compiler_params=_cparams(("parallel",)),
        name="gdn_out" if with_out else "gdn_state",
    )(big, big, small, cw, gp, nw.reshape(1, -1), s0f, s0b)
    return res if with_out else (None,) + tuple(res)


def _gdn_gate_params(a_log, dt_bias):
    gp = jnp.zeros((8, SMALL_W), F32)
    gp = gp.at[0, SM_A[0]:SM_A[0] + 2 * N_HEADS].set(-jnp.exp(a_log.astype(F32)).reshape(-1))
    gp = gp.at[1, SM_A[0]:SM_A[0] + 2 * N_HEADS].set(dt_bias.astype(F32).reshape(-1))
    return gp


def _rope(x, cos, sin, lane):
    swapped = jnp.where((lane % 64) < 32, pltpu.roll(x, 96, 1), pltpu.roll(x, 32, 1))
    return x * cos + swapped * sin


def _softmax_sink_pv(s, sink_col, vv):
    m = jnp.maximum(jnp.max(s, axis=-1, keepdims=True), sink_col)
    p = jnp.exp(s - m)
    den = jnp.sum(p, axis=-1, keepdims=True) + jnp.exp(sink_col - m)
    return _dot(p, vv) * (1.0 / den)


def _swa_kernel(*refs, seq, ctx_len, with_ctx):
    if with_ctx:
        (q_ref, k_ref, v_ref, gate_ref, kc_ref, vc_ref, cos_ref, sin_ref, sink_ref, qc_ref, gatec_ref,
         y_ref, yc_ref, kr_ref) = refs
    else:
        (q_ref, k_ref, v_ref, gate_ref, kc_ref, vc_ref, cos_ref, sin_ref, sink_ref,
         y_ref, kr_ref) = refs
    blk = SWA_BLOCK
    nb = seq // blk
    win = 3 * blk
    grp = N_HEADS // SWA_KV_HEADS
    scale = HEAD_DIM ** -0.5
    lane = _iota2((blk, HEAD_DIM), 1)

    def rope_k(n, _):
        rows = pl.ds(pl.multiple_of(n * blk, blk), blk)
        cos = cos_ref[rows, :]
        sin = sin_ref[rows, :]
        for g in range(SWA_KV_HEADS):
            hs = slice(g * HEAD_DIM, (g + 1) * HEAD_DIM)
            kr_ref[rows, hs] = _rope(k_ref[0, rows, hs].astype(F32), cos, sin, lane).astype(BF16)
        return _

    lax.fori_loop(0, nb, rope_k, None)

    rowq = _iota2((grp * blk, win + ctx_len), 0) % blk
    colk = _iota2((grp * blk, win + ctx_len), 1)
    is_ctx = colk >= win
    head_row = _iota2((grp * blk, 1), 0) // blk

    def sink_column(g, n_rows):
        hr = _iota2((grp * n_rows, 1), 0) // n_rows
        col = jnp.zeros((grp * n_rows, 1), F32)
        for j in range(grp):
            col = jnp.where(hr == j, sink_ref[0:1, g * grp + j:g * grp + j + 1], col)
        return col

    def q_block(n, _):
        r0 = pl.multiple_of(n * blk, blk)
        rows = pl.ds(r0, blk)
        k0 = pl.multiple_of(jnp.clip(r0 - blk, 0, seq - win), blk)
        cos = cos_ref[rows, :]
        sin = sin_ref[rows, :]
        valid = is_ctx | (jnp.abs(r0 + rowq - (k0 + colk)) <= SWA_WINDOW)
        for g in range(SWA_KV_HEADS):
            hs = slice(g * HEAD_DIM, (g + 1) * HEAD_DIM)
            qg = jnp.concatenate(
                [_rope(q_ref[0, rows, (g * grp + j) * HEAD_DIM:(g * grp + j + 1) * HEAD_DIM].astype(F32),
                       cos, sin, lane) for j in range(grp)], axis=0).astype(BF16)
            kk = jnp.concatenate([kr_ref[pl.ds(k0, win), hs], kc_ref[0, :, hs]], axis=0)
            vv = jnp.concatenate([v_ref[0, pl.ds(k0, win), hs], vc_ref[0, :, hs]], axis=0)
            s = jnp.where(valid, _dot_nt(qg, kk) * scale, NEG_BIG)
            o = _softmax_sink_pv(s, sink_column(g, blk), vv)
            for j in range(grp):
                cs = slice((g * grp + j) * HEAD_DIM, (g * grp + j + 1) * HEAD_DIM)
                y_ref[0, rows, cs] = (o[j * blk:(j + 1) * blk, :]
                                      * _silu(gate_ref[0, rows, cs].astype(F32))).astype(BF16)
        return _

    lax.fori_loop(0, nb, q_block, None)

    if with_ctx:
        for g in range(SWA_KV_HEADS):
            hs = slice(g * HEAD_DIM, (g + 1) * HEAD_DIM)
            qg = jnp.concatenate([qc_ref[0, :, (g * grp + j) * HEAD_DIM:(g * grp + j + 1) * HEAD_DIM]
                                  for j in range(grp)], axis=0)
            s = _dot_nt(qg, kc_ref[0, :, hs]) * scale
            o = _softmax_sink_pv(s, sink_column(g, ctx_len), vc_ref[0, :, hs])
            for j in range(grp):
                cs = slice((g * grp + j) * HEAD_DIM, (g * grp + j + 1) * HEAD_DIM)
                yc_ref[0, :, cs] = (o[j * ctx_len:(j + 1) * ctx_len, :]
                                    * _silu(gatec_ref[0, :, cs].astype(F32))).astype(BF16)


def _swa(big, bigc, cos, sin, sink, *, with_ctx):
    b, l, _ = big.shape
    lc = bigc.shape[1]
    in_specs = [pl.BlockSpec((1, l, 512), lambda bi: (bi, 0, 11)),
                pl.BlockSpec((1, l, 256), lambda bi: (bi, 0, 24)),
                pl.BlockSpec((1, l, 256), lambda bi: (bi, 0, 25)),
                pl.BlockSpec((1, l, 512), lambda bi: (bi, 0, 13)),
                pl.BlockSpec((1, lc, 256), lambda bi: (bi, 0, 24)),
                pl.BlockSpec((1, lc, 256), lambda bi: (bi, 0, 25)),
                pl.BlockSpec((l, HEAD_DIM), lambda bi: (0, 0)),
                pl.BlockSpec((l, HEAD_DIM), lambda bi: (0, 0)),
                pl.BlockSpec((1, LANES), lambda bi: (0, 0))]
    args = [big, big, big, big, bigc, bigc, cos, sin, sink]
    out_shape = [jax.ShapeDtypeStruct((b, l, GROUP_W), BF16)]
    out_specs = [pl.BlockSpec((1, l, GROUP_W), lambda bi: (bi, 0, 0))]
    if with_ctx:
        in_specs += [pl.BlockSpec((1, lc, 512), lambda bi: (bi, 0, 11)),
                     pl.BlockSpec((1, lc, 512), lambda bi: (bi, 0, 13))]
        args += [bigc, bigc]
        out_shape.append(jax.ShapeDtypeStruct((b, lc, GROUP_W), BF16))
        out_specs.append(pl.BlockSpec((1, lc, GROUP_W), lambda bi: (bi, 0, 0)))
    res = pl.pallas_call(
        functools.partial(_swa_kernel, seq=l, ctx_len=lc, with_ctx=with_ctx),
        out_shape=tuple(out_shape), grid=(b,), in_specs=in_specs, out_specs=tuple(out_specs),
        scratch_shapes=[pltpu.VMEM((l, SWA_KV_HEADS * HEAD_DIM), BF16)],
        compiler_params=_cparams(("parallel",)),
        name="swa_ctx" if with_ctx else "swa",
    )(*args)
    return (res[0], res[1]) if with_ctx else (res[0], None)


def _rope_tables(seq_len):
    rows = seq_len // GRID_W
    row = jnp.repeat(jnp.arange(rows, dtype=F32), GRID_W)
    col = jnp.tile(jnp.arange(GRID_W, dtype=F32), rows)
    axis_dim = HEAD_DIM // 2
    inv_freq = ROPE_BASE ** (-jnp.arange(0, axis_dim, 2, dtype=F32) / axis_dim)
    ang_r = row[:, None] * inv_freq
    ang_c = col[:, None] * inv_freq
    cos = jnp.concatenate([jnp.cos(ang_r), jnp.cos(ang_r), jnp.cos(ang_c), jnp.cos(ang_c)], axis=-1)
    sin = jnp.concatenate([-jnp.sin(ang_r), jnp.sin(ang_r), -jnp.sin(ang_c), jnp.sin(ang_c)], axis=-1)
    return cos, sin


_PACK_RUNS = ((0, 0, 1024), (1024, 1056, 2048), (3072, 3120, 4096))
_LR_COL0 = 1024
_GDN_GATE_COL0 = 3104


def _pack_kernel(w_ref, big_ref, small_ref):
    step = 512
    for dst, src, width in _PACK_RUNS:
        for o in range(0, width, step):
            big_ref[0, :, dst + o:dst + o + step] = w_ref[0, :, src + o:src + o + step].astype(BF16)
    assert _LR_COL0 % LANES == 0 and _GDN_GATE_COL0 % LANES == SM_A[0]
    gate_win0 = _GDN_GATE_COL0 - SM_A[0]
    lane = _iota2((w_ref.shape[1], LANES), 1)
    lr_win = w_ref[0, :, _LR_COL0:_LR_COL0 + LANES]
    gate_win = w_ref[0, :, gate_win0:gate_win0 + LANES]
    small = jnp.where(lane < SM_A[0], lr_win, jnp.where(lane < SM_B[1] + N_HEADS, gate_win, 0.0))
    small_ref[0] = small.astype(BF16)


def _pack_w_in(w_in):
    n_layers, d, n = w_in.shape
    rows = 256
    return pl.pallas_call(
        _pack_kernel,
        out_shape=(jax.ShapeDtypeStruct((n_layers, d, BIG_W), BF16),
                   jax.ShapeDtypeStruct((n_layers, d, SMALL_W), BF16)),
        grid=(n_layers, d // rows),
        in_specs=[pl.BlockSpec((1, rows, n), lambda l, i: (l, i, 0))],
        out_specs=(pl.BlockSpec((1, rows, BIG_W), lambda l, i: (l, i, 0)),
                   pl.BlockSpec((1, rows, SMALL_W), lambda l, i: (l, i, 0))),
        compiler_params=_cparams(("parallel", "parallel")),
        name="pack_w_in",
    )(w_in)


def kernel(x, c, ctx, c_ctx, ada_w, ada_b, norm_pre, norm_post, w_in, w_out, gla_w_decay, gla_b_decay,
           gla_norm, gdn_conv, gdn_a_log, gdn_dt_bias, gdn_norm, sc_conv, swa_sink):
    b, l, d = x.shape
    cos, sin = _rope_tables(l)
    cc = jnp.zeros((8, d), F32).at[:b].set(c).at[b].set(c_ctx)
    mod = _ada(cc, ada_w, ada_b)
    h, hc = x, ctx
    lc = ctx.shape[1]
    n_layers = ada_w.shape[0]
    w_big, w_small = _pack_w_in(w_in)
    w_o = w_out.astype(BF16)
    for layer in range(n_layers):
        with_ctx = layer < n_layers - 1
        shift, scale, gate = (mod[layer, :, i * d:(i + 1) * d] for i in range(3))

        def per_batch(t):
            return t[:b, None, :], t[b][None, None, :]
        (shift_l, shift_c), (scale_l, scale_c), (gate_l, gate_c) = per_batch(shift), per_batch(scale), per_batch(gate)
        g_pre = norm_pre[layer].reshape(1, d)
        big, small = _inproj(h, scale_l, shift_l, g_pre, w_big, w_small, layer, tm=min(l, 1024))
        bigc, smallc = _inproj(hc.reshape(1, b * lc, d), scale_c, shift_c, g_pre, w_big, w_small, layer,
                               tm=min(b * lc, 1024))
        bigc, smallc = bigc.reshape(b, lc, BIG_W), smallc.reshape(b, lc, SMALL_W)

        zf = jnp.zeros((b, HEAD_DIM, N_HEADS * GLA_DK), F32)
        yca, stf, stb = _gla(bigc, smallc, gla_w_decay[layer], gla_b_decay[layer], gla_norm[layer], zf, zf,
                             with_out=with_ctx)
        ya, _, _ = _gla(big, small, gla_w_decay[layer], gla_b_decay[layer], gla_norm[layer], stf, stb,
                        with_out=True)

        gp = _gdn_gate_params(gdn_a_log[layer], gdn_dt_bias[layer])
        zg = jnp.zeros((b, N_HEADS * HEAD_DIM, HEAD_DIM), F32)
        ycb, gtf, gtb = _gdn(bigc, smallc, gdn_conv[layer], gp, gdn_norm[layer], zg, zg, with_out=with_ctx)
        yb, _, _ = _gdn(big, small, gdn_conv[layer], gp, gdn_norm[layer], gtf, gtb, with_out=True)

        yc_ = _sconv(big, sc_conv[layer])
        sink = jnp.zeros((1, LANES), F32).at[0, :N_HEADS].set(swa_sink[layer])
        yd, ycd = _swa(big, bigc, cos, sin, sink, with_ctx=with_ctx)

        g_post = norm_post[layer].reshape(1, d)
        h_new = _outproj(ya, yb, yc_, yd, w_o, layer, h, gate_l, g_post, tm=min(l, 512))
        if with_ctx:
            ycc = _sconv(bigc, sc_conv[layer])
            flat = [t.reshape(1, b * lc, GROUP_W) for t in (yca, ycb, ycc, ycd)]
            hc = _outproj(*flat, w_o, layer, hc.reshape(1, b * lc, d), gate_c, g_post,
                          tm=min(b * lc, 512)).reshape(b, lc, d)
        h = h_new
    return h
```

```python
import functools

import jax
import jax.numpy as jnp
from jax import lax
from jax.experimental import pallas as pl
from jax.experimental.pallas import tpu as pltpu

F32 = jnp.float32
BF16 = jnp.bfloat16

D_MODEL = 2048
N_LAYERS = 2
GRID_W = 64
GROUP_W = 512
HEAD_DIM = 128
N_HEADS = 4
EPS = 1e-6
GLA_DK = 64
GLA_RANK = 16
GLA_TAU = 16.0
GDN_CONV = 5
SC_CONV = 3
SWA_KV_HEADS = 2
SWA_WINDOW = 128
ROPE_BASE = 10000.0

_SPLITS = (
    ("gla_q", 256), ("gla_k", 256), ("gla_v", 512), ("gla_lr_f", 16), ("gla_lr_b", 16), ("gla_gate", 512),
    ("gdn_q", 512), ("gdn_k", 512), ("gdn_v", 512),
    ("gdn_a_f", 4), ("gdn_a_b", 4), ("gdn_b_f", 4), ("gdn_b_b", 4), ("gdn_gate", 512),
    ("sc_b", 512), ("sc_c", 512), ("sc_h", 512), ("sc_gate", 512),
    ("swa_q", 512), ("swa_k", 256), ("swa_v", 256), ("swa_gate", 512),
)
_BIG_ORDER = ("gla_q", "gla_k", "gla_v", "gla_gate", "gdn_q", "gdn_k", "gdn_v", "gdn_gate",
              "sc_b", "sc_c", "sc_h", "sc_gate", "swa_q", "swa_k", "swa_v", "swa_gate")
_SMALL_ORDER = ("gla_lr_f", "gla_lr_b", "gdn_a_f", "gdn_a_b", "gdn_b_f", "gdn_b_b")
BIG_W = 7168
SMALL_W = 128
SM_LR = (0, 16)
SM_A = (32, 36)
SM_B = (40, 44)

LANES = 128
VMEM_LIMIT = 56 * 1024 * 1024

CHUNK = 128
SWA_BLOCK = 128

NEG_BIG = -1e30


def _cparams(sem):
    return pltpu.CompilerParams(dimension_semantics=sem, vmem_limit_bytes=VMEM_LIMIT)


def _dot(a, b):
    return lax.dot_general(a.astype(BF16), b.astype(BF16), (((1,), (0,)), ((), ())),
                           preferred_element_type=F32)


def _dot_nt(a, b):
    return lax.dot_general(a.astype(BF16), b.astype(BF16), (((1,), (1,)), ((), ())),
                           preferred_element_type=F32)


def _dot_tn(a, b):
    return lax.dot_general(a.astype(BF16), b.astype(BF16), (((0,), (0,)), ((), ())),
                           preferred_element_type=F32)


def _split(x):
    hi = x.astype(BF16)
    lo = (x - hi.astype(F32)).astype(BF16)
    return hi, lo


def _dot_exact_lhs(a_bf16, x):
    hi, lo = _split(x)
    return _dot(a_bf16, hi) + _dot(a_bf16, lo)


def _dot_exact_rhs(x, b_bf16):
    hi, lo = _split(x)
    return _dot(hi, b_bf16) + _dot(lo, b_bf16)


def _silu(x):
    return x * (1.0 / (1.0 + jnp.exp(-x)))


def _sigmoid(x):
    return 1.0 / (1.0 + jnp.exp(-x))


def _softplus(x):
    return jnp.maximum(x, 0.0) + jnp.log(1.0 + jnp.exp(-jnp.abs(x)))


def _log_sigmoid(x):
    return jnp.minimum(x, 0.0) - jnp.log(1.0 + jnp.exp(-jnp.abs(x)))


def _iota2(shape, dim):
    return lax.broadcasted_iota(jnp.int32, shape, dim)


ADA_K_SPLIT = 4


def _ada_kernel(c_ref, *refs):
    w_refs, (b_ref, o_ref) = refs[:ADA_K_SPLIT], refs[ADA_K_SPLIT:]
    a = _silu(c_ref[...])
    kq = a.shape[1] // ADA_K_SPLIT
    acc = b_ref[0]
    for q, w_ref in enumerate(w_refs):
        acc = acc + _dot(a[:, q * kq:(q + 1) * kq], w_ref[0])
    o_ref[0] = acc


def _ada(cc, ada_w, ada_b):
    n_layers, d, n = ada_w.shape
    tn = 1024
    kq = d // ADA_K_SPLIT
    w_specs = [pl.BlockSpec((1, kq, tn), lambda l, j, q=q: (l, q, j)) for q in range(ADA_K_SPLIT)]
    return pl.pallas_call(
        _ada_kernel,
        out_shape=jax.ShapeDtypeStruct((n_layers, 8, n), F32),
        grid=(n_layers, n // tn),
        in_specs=[pl.BlockSpec((8, d), lambda l, j: (0, 0))] + w_specs
        + [pl.BlockSpec((1, 1, tn), lambda l, j: (l, 0, j))],
        out_specs=pl.BlockSpec((1, 8, tn), lambda l, j: (l, 0, j)),
        compiler_params=_cparams(("parallel", "parallel")),
        name="ada_mod",
    )(cc, *([ada_w] * ADA_K_SPLIT), ada_b.reshape(n_layers, 1, n))


def _inproj_kernel(x_ref, sc_ref, sh_ref, g_ref, wb_ref, ws_ref, ob_ref, os_ref, xn_ref):
    @pl.when(pl.program_id(2) == 0)
    def _():
        rows_per_step = 128

        def norm_rows(i, _):
            rows = pl.ds(pl.multiple_of(i * rows_per_step, rows_per_step), rows_per_step)
            x = x_ref[0, rows, :]
            y = x * lax.rsqrt(jnp.mean(x * x, axis=-1, keepdims=True) + EPS) * g_ref[...]
            xn_ref[rows, :] = (y * (1.0 + sc_ref[0]) + sh_ref[0]).astype(BF16)
            return _

        lax.fori_loop(0, xn_ref.shape[0] // rows_per_step, norm_rows, None)
        os_ref[0] = jnp.dot(xn_ref[...], ws_ref[0], preferred_element_type=F32)

    ob_ref[0] = jnp.dot(xn_ref[...], wb_ref[0], preferred_element_type=F32).astype(BF16)


def _inproj(x, scale, shift, g, w_big, w_small, layer, *, tm, tn=1024):
    b, l, d = x.shape
    return pl.pallas_call(
        _inproj_kernel,
        out_shape=(jax.ShapeDtypeStruct((b, l, BIG_W), BF16), jax.ShapeDtypeStruct((b, l, SMALL_W), F32)),
        grid=(b, l // tm, BIG_W // tn),
        in_specs=[pl.BlockSpec((1, tm, d), lambda bi, i, j: (bi, i, 0)),
                  pl.BlockSpec((1, 1, d), lambda bi, i, j: (bi, 0, 0)),
                  pl.BlockSpec((1, 1, d), lambda bi, i, j: (bi, 0, 0)),
                  pl.BlockSpec((1, d), lambda bi, i, j: (0, 0)),
                  pl.BlockSpec((1, d, tn), lambda bi, i, j: (layer, 0, j)),
                  pl.BlockSpec((1, d, SMALL_W), lambda bi, i, j: (layer, 0, 0))],
        out_specs=(pl.BlockSpec((1, tm, tn), lambda bi, i, j: (bi, i, j)),
                   pl.BlockSpec((1, tm, SMALL_W), lambda bi, i, j: (bi, i, 0))),
        scratch_shapes=[pltpu.VMEM((tm, d), BF16)],
        compiler_params=_cparams(("parallel", "parallel", "arbitrary")),
        name="inproj",
    )(x, scale, shift, g, w_big, w_small)


def _outproj_kernel(ya_ref, yb_ref, yc_ref, yd_ref, w_ref, hl_ref, hr_ref, gate_ref, g_ref, o_ref):
    y = jnp.dot(ya_ref[0], w_ref[0, 0:GROUP_W, :], preferred_element_type=F32)
    y += jnp.dot(yb_ref[0], w_ref[0, GROUP_W:2 * GROUP_W, :], preferred_element_type=F32)
    y += jnp.dot(yc_ref[0], w_ref[0, 2 * GROUP_W:3 * GROUP_W, :], preferred_element_type=F32)
    y += jnp.dot(yd_ref[0], w_ref[0, 3 * GROUP_W:4 * GROUP_W, :], preferred_element_type=F32)
    yn = gate_ref[0] * (y * lax.rsqrt(jnp.mean(y * y, axis=-1, keepdims=True) + EPS) * g_ref[...])
    half = hl_ref.shape[2]
    o_ref[0, :, 0:half] = hl_ref[0] + yn[:, 0:half]
    o_ref[0, :, half:2 * half] = hr_ref[0] + yn[:, half:2 * half]


def _outproj(ya, yb, yc, yd, w_out, layer, h, gate, g, *, tm):
    b, l, d = h.shape
    yspec = pl.BlockSpec((1, tm, GROUP_W), lambda bi, i: (bi, i, 0))
    return pl.pallas_call(
        _outproj_kernel,
        out_shape=jax.ShapeDtypeStruct((b, l, d), F32),
        grid=(b, l // tm),
        in_specs=[yspec, yspec, yspec, yspec,
                  pl.BlockSpec((1, d, d), lambda bi, i: (layer, 0, 0)),
                  pl.BlockSpec((1, tm, d // 2), lambda bi, i: (bi, i, 0)),
                  pl.BlockSpec((1, tm, d // 2), lambda bi, i: (bi, i, 1)),
                  pl.BlockSpec((1, 1, d), lambda bi, i: (bi, 0, 0)),
                  pl.BlockSpec((1, d), lambda bi, i: (0, 0))],
        out_specs=pl.BlockSpec((1, tm, d), lambda bi, i: (bi, i, 0)),
        compiler_params=_cparams(("parallel", "parallel")),
        name="outproj",
    )(ya, yb, yc, yd, w_out, h, h, gate, g)


def _sconv_kernel(b_ref, c_ref, h_ref, gate_ref, w_ref, o_ref, p_ref, *, seq):
    zeros = jnp.zeros((8, GROUP_W), F32)
    p_ref[0:8, :] = zeros
    p_ref[seq + 8:seq + 16, :] = zeros
    p_ref[8:seq + 8, :] = c_ref[0].astype(F32) * h_ref[0].astype(F32)
    rows = min(seq, 256)
    w = w_ref[...]
    for t0 in range(0, seq, rows):
        conv = (p_ref[t0 + 7:t0 + 7 + rows, :] * w[0:1, :] + p_ref[t0 + 8:t0 + 8 + rows, :] * w[1:2, :]
                + p_ref[t0 + 9:t0 + 9 + rows, :] * w[2:3, :])
        y = b_ref[0, t0:t0 + rows, :].astype(F32) * conv * _silu(gate_ref[0, t0:t0 + rows, :].astype(F32))
        o_ref[0, t0:t0 + rows, :] = y.astype(BF16)


def _sconv(big, w):
    b, l, _ = big.shape
    col0 = 3584 // GROUP_W

    def spec(k):
        return pl.BlockSpec((1, l, GROUP_W), lambda bi, k=k: (bi, 0, col0 + k))
    return pl.pallas_call(
        functools.partial(_sconv_kernel, seq=l),
        out_shape=jax.ShapeDtypeStruct((b, l, GROUP_W), BF16),
        grid=(b,),
        in_specs=[spec(0), spec(1), spec(2), spec(3), pl.BlockSpec((SC_CONV, GROUP_W), lambda bi: (0, 0))],
        out_specs=pl.BlockSpec((1, l, GROUP_W), lambda bi: (bi, 0, 0)),
        scratch_shapes=[pltpu.VMEM((l + 16, GROUP_W), F32)],
        compiler_params=_cparams(("parallel",)),
        name="sconv",
    )(big, big, big, big, w)


def _gla_kernel(q_ref, k_ref, v_ref, gate_ref, sm_ref, wd_ref, bd_ref, nw_ref, s0f_ref, s0b_ref,
                *rest, seq, with_out):
    if with_out:
        y_ref, stf_ref, stb_ref, o_ref, st_ref = rest
    else:
        stf_ref, stb_ref, st_ref = rest
        y_ref = o_ref = None
    c = CHUNK
    n_chunks = seq // c
    ri = _iota2((c, c), 0)
    ci = _iota2((c, c), 1)
    incl = (ri >= ci, ri <= ci)
    tri = tuple(jnp.where(m, 1.0, 0.0).astype(BF16) for m in incl)
    qscale = GLA_DK ** -0.5
    last = (c - 1, 0)
    chains = [(d, h) for d in (0, 1) for h in range(N_HEADS)]

    def step(i, _):
        rows = [pl.ds(pl.multiple_of(n * c, c), c) for n in (i, n_chunks - 1 - i)]
        tots, kos, sts, vs, qps, kps, qis, stbs = [], [], [], [], [], [], [], []
        for d in (0, 1):
            lr = sm_ref[0, rows[d], :][:, SM_LR[d]:SM_LR[d] + GLA_RANK]
            z = _dot(lr, wd_ref[d]) + bd_ref[d]
            la = _log_sigmoid(z) * (1.0 / GLA_TAU)
            cum = _dot_exact_lhs(tri[d], la)
            tot = cum[last[d]:last[d] + 1, :]
            kf = k_ref[0, rows[d], :].astype(F32)
            tots.append(tot)
            kos.append((kf * jnp.exp(tot - cum)).astype(BF16))
            sts.append(st_ref[d])
            vs.append(v_ref[0, rows[d], :])
            if with_out:
                mid = cum[c // 2:c // 2 + 1, :]
                qf = q_ref[0, rows[d], :].astype(F32) * qscale
                qps.append((qf * jnp.exp(cum - mid)).astype(BF16))
                kps.append((kf * jnp.exp(mid - cum)).astype(BF16))
                qis.append((qf * jnp.exp(cum)).astype(BF16))
                stbs.append(sts[d].astype(BF16))

        def dk(h):
            return slice(h * GLA_DK, (h + 1) * GLA_DK)

        def dv(h):
            return slice(h * HEAD_DIM, (h + 1) * HEAD_DIM)

        if with_out:
            attn = [jnp.where(incl[d], _dot_nt(qps[d][:, dk(h)], kps[d][:, dk(h)]), 0.0) for d, h in chains]
            inter = [_dot_nt(qis[d][:, dk(h)], stbs[d][:, dk(h)]) for d, h in chains]
            intra = [_dot(a, vs[d][:, dv(h)]) for a, (d, h) in zip(attn, chains)]
            for (d, h), o1, o2 in zip(chains, inter, intra):
                o_ref[d, rows[d], dv(h)] = o1 + o2
        upd = [_dot_tn(vs[d][:, dv(h)], kos[d][:, dk(h)]) for d, h in chains]
        for d in (0, 1):
            st_ref[d] = sts[d] * jnp.exp(tots[d]) + jnp.concatenate(upd[d * N_HEADS:(d + 1) * N_HEADS], axis=1)
        return _

    st_ref[0] = s0f_ref[0]
    st_ref[1] = s0b_ref[0]
    lax.fori_loop(0, n_chunks, step, None, unroll=2)
    stf_ref[0] = st_ref[0]
    stb_ref[0] = st_ref[1]

    if with_out:
        def finish(n, _):
            rows = pl.ds(pl.multiple_of(n * c, c), c)
            for h in range(N_HEADS):
                hs = slice(h * HEAD_DIM, (h + 1) * HEAD_DIM)
                o = o_ref[0, rows, hs] + o_ref[1, rows, hs]
                o = o * lax.rsqrt(jnp.mean(o * o, axis=-1, keepdims=True) + EPS) * nw_ref[...]
                y_ref[0, rows, hs] = (o * _silu(gate_ref[0, rows, hs].astype(F32))).astype(BF16)
            return _

        lax.fori_loop(0, n_chunks, finish, None)


def _gla(big, small, wd, bd, nw, s0f, s0b, *, with_out):
    b, l, _ = big.shape
    st_shape = jax.ShapeDtypeStruct((b, HEAD_DIM, N_HEADS * GLA_DK), F32)
    st_spec = pl.BlockSpec((1, HEAD_DIM, N_HEADS * GLA_DK), lambda bi: (bi, 0, 0))
    in_specs = [pl.BlockSpec((1, l, 256), lambda bi: (bi, 0, 0)),
                pl.BlockSpec((1, l, 256), lambda bi: (bi, 0, 1)),
                pl.BlockSpec((1, l, 512), lambda bi: (bi, 0, 1)),
                pl.BlockSpec((1, l, 512), lambda bi: (bi, 0, 2)),
                pl.BlockSpec((1, l, SMALL_W), lambda bi: (bi, 0, 0)),
                pl.BlockSpec((2, GLA_RANK, 256), lambda bi: (0, 0, 0)),
                pl.BlockSpec((2, 1, 256), lambda bi: (0, 0, 0)),
                pl.BlockSpec((1, HEAD_DIM), lambda bi: (0, 0)),
                st_spec, st_spec]
    out_shape = [st_shape, st_shape]
    out_specs = [st_spec, st_spec]
    scratch = [pltpu.VMEM((2, HEAD_DIM, N_HEADS * GLA_DK), F32)]
    if with_out:
        out_shape = [jax.ShapeDtypeStruct((b, l, GROUP_W), BF16)] + out_shape
        out_specs = [pl.BlockSpec((1, l, GROUP_W), lambda bi: (bi, 0, 0))] + out_specs
        scratch = [pltpu.VMEM((2, l, GROUP_W), F32)] + scratch
    res = pl.pallas_call(
        functools.partial(_gla_kernel, seq=l, with_out=with_out),
        out_shape=tuple(out_shape), grid=(b,), in_specs=in_specs, out_specs=tuple(out_specs),
        scratch_shapes=scratch, compiler_params=_cparams(("parallel",)),
        name="gla_out" if with_out else "gla_state",
    )(big, big, big, big, small, wd, bd.reshape(2, 1, -1), nw.reshape(1, -1), s0f, s0b)
    return res if with_out else (None,) + tuple(res)


def _unit_triangular_inverses(lows, ri, ci):
    c = lows[0].shape[0]
    eye = jnp.where(ri == ci, 1.0, 0.0)
    pair = (ri // 2) == (ci // 2)
    ts = [eye - jnp.where(pair, low, 0.0) for low in lows]
    s = 2
    while s < c:
        sel = ((ri // (2 * s)) == (ci // (2 * s))) & ((ri // s) != (ci // s))
        tb = [t.astype(BF16) for t in ts]
        ps = [_dot(t, jnp.where(sel, low, 0.0)) for t, low in zip(tb, lows)]
        ts = [t - _dot(p, t16) for t, p, t16 in zip(ts, ps, tb)]
        s *= 2
    return ts


def _gdn_kernel(qkv_ref, gate_ref, sm_ref, cw_ref, gp_ref, nw_ref, s0f_ref, s0b_ref,
                *rest, seq, with_out):
    if with_out:
        y_ref, stf_ref, stb_ref, x_ref, o_ref, st_ref = rest
    else:
        stf_ref, stb_ref, x_ref, st_ref = rest
        y_ref = o_ref = None
    c = CHUNK
    n_chunks = seq // c
    width = 3 * GROUP_W
    ri = _iota2((c, c), 0)
    ci = _iota2((c, c), 1)
    lower = ri >= ci
    upper = ri <= ci
    tril = jnp.where(lower, 1.0, 0.0).astype(BF16)
    triu = jnp.where(upper, 1.0, 0.0).astype(BF16)

    def conv_chunk(n, _):
        r0 = pl.multiple_of(n * c, c)
        prev0 = pl.multiple_of(jnp.maximum(r0 - 16, 0), 16)
        next0 = pl.multiple_of(jnp.minimum(r0 + c, seq - 16), 16)
        prev = jnp.where(n > 0, qkv_ref[0, pl.ds(prev0, 16), :].astype(F32), 0.0)
        nxt = jnp.where(n < n_chunks - 1, qkv_ref[0, pl.ds(next0, 16), :].astype(F32), 0.0)
        xe = jnp.concatenate([prev, qkv_ref[0, pl.ds(r0, c), :].astype(F32), nxt], axis=0)
        acc = xe[14:14 + c, :] * cw_ref[0:1, :]
        for j in range(1, GDN_CONV):
            acc = acc + xe[14 + j:14 + j + c, :] * cw_ref[j:j + 1, :]
        acc = _silu(acc)
        for h in range(2 * N_HEADS):
            hs = slice(h * HEAD_DIM, (h + 1) * HEAD_DIM)
            t = acc[:, hs]
            t = t * lax.rsqrt(jnp.sum(t * t, axis=-1, keepdims=True) + EPS)
            if h < N_HEADS:
                t = t * (HEAD_DIM ** -0.5)
            x_ref[pl.ds(r0, c), hs] = t
        x_ref[pl.ds(r0, c), 2 * GROUP_W:width] = acc[:, 2 * GROUP_W:width]
        return _

    lax.fori_loop(0, n_chunks, conv_chunk, None)

    chains = [(d, h) for d in (0, 1) for h in range(N_HEADS)]
    n_state = N_HEADS * HEAD_DIM

    def step(i, _):
        rows = [pl.ds(pl.multiple_of(n * c, c), c) for n in (i, n_chunks - 1 - i)]
        cums, cum_rows, betas = [], [], []
        for d in (0, 1):
            sm = sm_ref[0, rows[d], :]
            g = gp_ref[0:1, :] * _softplus(sm + gp_ref[1:2, :])
            betas.append(_sigmoid(sm))
            cums.append(_dot_exact_lhs(tril if d == 0 else triu, g))
            cum_rows.append(_dot_exact_rhs(g.T, triu if d == 0 else tril))
        incl = (lower, upper)
        strict = (ri > ci, ri < ci)
        last = (c - 1, 0)

        kfs, vbs, kbs, e_cs, tots, decays, kos, qfs = [], [], [], [], [], [], [], []
        for d, h in chains:
            la, lb = SM_A[d] + h, SM_B[d] + h
            cum_c = cums[d][:, la:la + 1]
            cum_r = cum_rows[d][la:la + 1, :]
            tot = cums[d][last[d]:last[d] + 1, la:la + 1]
            beta_c = betas[d][:, lb:lb + 1]
            kf = x_ref[rows[d], GROUP_W + h * HEAD_DIM:GROUP_W + (h + 1) * HEAD_DIM]
            vf = x_ref[rows[d], 2 * GROUP_W + h * HEAD_DIM:2 * GROUP_W + (h + 1) * HEAD_DIM]
            e_c = jnp.exp(cum_c)
            kfs.append(kf)
            kbs.append(kf * beta_c)
            vbs.append(vf * beta_c)
            e_cs.append(e_c)
            tots.append(tot)
            decays.append(jnp.where(incl[d], jnp.exp(jnp.minimum(cum_c - cum_r, 0.0)), 0.0))
            kos.append(kf * jnp.exp(tot - cum_c))
            if with_out:
                qfs.append(x_ref[rows[d], h * HEAD_DIM:(h + 1) * HEAD_DIM])

        if with_out:
            prods = [_dot_nt(jnp.concatenate([kb, qf], axis=0), kf) for kb, qf, kf in zip(kbs, qfs, kfs)]
            attns = [p[c:, :] * dec for p, dec in zip(prods, decays)]
        else:
            prods = [_dot_nt(kb, kf) for kb, kf in zip(kbs, kfs)]
        lows = [jnp.where(strict[d], p[:c, :] * dec, 0.0) for (d, _), p, dec in zip(chains, prods, decays)]
        ts = _unit_triangular_inverses(lows, ri, ci)
        uws = [_dot(t, jnp.concatenate([vb, kb * e_c], axis=1)) for t, vb, kb, e_c in zip(ts, vbs, kbs, e_cs)]
        sts = [st_ref[j * HEAD_DIM:(j + 1) * HEAD_DIM, :] for j in range(len(chains))]
        if with_out:
            reads = [_dot(jnp.concatenate([uw[:, HEAD_DIM:], qf * e_c], axis=0), st)
                     for uw, qf, e_c, st in zip(uws, qfs, e_cs, sts)]
        else:
            reads = [_dot(uw[:, HEAD_DIM:], st) for uw, st in zip(uws, sts)]
        v_news = [uw[:, :HEAD_DIM] - r[:c, :] for uw, r in zip(uws, reads)]
        if with_out:
            writes = [_dot(jnp.concatenate([a, ko.T], axis=0), vn) for a, ko, vn in zip(attns, kos, v_news)]
        else:
            writes = [_dot_tn(ko, vn) for ko, vn in zip(kos, v_news)]
        for j, (d, h) in enumerate(chains):
            if with_out:
                o_ref[d, rows[d], h * HEAD_DIM:(h + 1) * HEAD_DIM] = reads[j][c:, :] + writes[j][:c, :]
                upd = writes[j][c:, :]
            else:
                upd = writes[j]
            st_ref[j * HEAD_DIM:(j + 1) * HEAD_DIM, :] = sts[j] * jnp.exp(tots[j]) + upd
        return _

    st_ref[0:n_state, :] = s0f_ref[0]
    st_ref[n_state:2 * n_state, :] = s0b_ref[0]
    lax.fori_loop(0, n_chunks, step, None, unroll=2)
    stf_ref[0] = st_ref[0:n_state, :]
    stb_ref[0] = st_ref[n_state:2 * n_state, :]

    if with_out:
        def finish(n, _):
            rows = pl.ds(pl.multiple_of(n * c, c), c)
            for h in range(N_HEADS):
                hs = slice(h * HEAD_DIM, (h + 1) * HEAD_DIM)
                o = o_ref[0, rows, hs] + o_ref[1, rows, hs]
                o = o * lax.rsqrt(jnp.mean(o * o, axis=-1, keepdims=True) + EPS) * nw_ref[...]
                y_ref[0, rows, hs] = (o * _silu(gate_ref[0, rows, hs].astype(F32))).astype(BF16)
            return _

        lax.fori_loop(0, n_chunks, finish, None)


def _gdn(big, small, cw, gp, nw, s0f, s0b, *, with_out):
    b, l, _ = big.shape
    st_shape = jax.ShapeDtypeStruct((b, N_HEADS * HEAD_DIM, HEAD_DIM), F32)
    st_spec = pl.BlockSpec((1, N_HEADS * HEAD_DIM, HEAD_DIM), lambda bi: (bi, 0, 0))
    in_specs = [pl.BlockSpec((1, l, 3 * GROUP_W), lambda bi: (bi, 0, 1)),
                pl.BlockSpec((1, l, GROUP_W), lambda bi: (bi, 0, 6)),
                pl.BlockSpec((1, l, SMALL_W), lambda bi: (bi, 0, 0)),
                pl.BlockSpec((GDN_CONV, 3 * GROUP_W), lambda bi: (0, 0)),
                pl.BlockSpec((8, SMALL_W), lambda bi: (0, 0)),
                pl.BlockSpec((1, HEAD_DIM), lambda bi: (0, 0)),
                st_spec, st_spec]
    out_shape = [st_shape, st_shape]
    out_specs = [st_spec, st_spec]
    scratch = [pltpu.VMEM((l, 3 * GROUP_W), F32), pltpu.VMEM((2 * N_HEADS * HEAD_DIM, HEAD_DIM), F32)]
    if with_out:
        out_shape = [jax.ShapeDtypeStruct((b, l, GROUP_W), BF16)] + out_shape
        out_specs = [pl.BlockSpec((1, l, GROUP_W), lambda bi: (bi, 0, 0))] + out_specs
        scratch = [scratch[0], pltpu.VMEM((2, l, GROUP_W), F32), scratch[1]]
    res = pl.pallas_call(
        functools.partial(_gdn_kernel, seq=l, with_out=with_out),
        out_shape=tuple(out_shape), grid=(b,), in_specs=in_specs, out_specs=tuple(out_specs),
        scratch_shapes=scratch, compiler_params=_cparams(("parallel",)),
        name="gdn_out" if with_out else "gdn_state",
    )(big, big, small, cw, gp, nw.reshape(1, -1), s0f, s0b)
    return res if with_out else (None,) + tuple(res)


def _gdn_gate_params(a_log, dt_bias):
    gp = jnp.zeros((8, SMALL_W), F32)
    gp = gp.at[0, SM_A[0]:SM_A[0] + 2 * N_HEADS].set(-jnp.exp(a_log.astype(F32)).reshape(-1))
    gp = gp.at[1, SM_A[0]:SM_A[0] + 2 * N_HEADS].set(dt_bias.astype(F32).reshape(-1))
    return gp


def _rope(x, cos, sin, lane):
    swapped = jnp.where((lane % 64) < 32, pltpu.roll(x, 96, 1), pltpu.roll(x, 32, 1))
    return x * cos + swapped * sin


def _softmax_sink_pv(s, sink_col, vv):
    m = jnp.maximum(jnp.max(s, axis=-1, keepdims=True), sink_col)
    p = jnp.exp(s - m)
    den = jnp.sum(p, axis=-1, keepdims=True) + jnp.exp(sink_col - m)
    return _dot(p, vv) * (1.0 / den)


def _swa_kernel(*refs, seq, ctx_len, with_ctx):
    if with_ctx:
        (q_ref, k_ref, v_ref, gate_ref, kc_ref, vc_ref, cos_ref, sin_ref, sink_ref, qc_ref, gatec_ref,
         y_ref, yc_ref, kr_ref) = refs
    else:
        (q_ref, k_ref, v_ref, gate_ref, kc_ref, vc_ref, cos_ref, sin_ref, sink_ref,
         y_ref, kr_ref) = refs
    blk = SWA_BLOCK
    nb = seq // blk
    win = 3 * blk
    grp = N_HEADS // SWA_KV_HEADS
    scale = HEAD_DIM ** -0.5
    lane = _iota2((blk, HEAD_DIM), 1)

    def rope_k(n, _):
        rows = pl.ds(pl.multiple_of(n * blk, blk), blk)
        cos = cos_ref[rows, :]
        sin = sin_ref[rows, :]
        for g in range(SWA_KV_HEADS):
            hs = slice(g * HEAD_DIM, (g + 1) * HEAD_DIM)
            kr_ref[rows, hs] = _rope(k_ref[0, rows, hs].astype(F32), cos, sin, lane).astype(BF16)
        return _

    lax.fori_loop(0, nb, rope_k, None)

    rowq = _iota2((grp * blk, win + ctx_len), 0) % blk
    colk = _iota2((grp * blk, win + ctx_len), 1)
    is_ctx = colk >= win
    head_row = _iota2((grp * blk, 1), 0) // blk

    def sink_column(g, n_rows):
        hr = _iota2((grp * n_rows, 1), 0) // n_rows
        col = jnp.zeros((grp * n_rows, 1), F32)
        for j in range(grp):
            col = jnp.where(hr == j, sink_ref[0:1, g * grp + j:g * grp + j + 1], col)
        return col

    def q_block(n, _):
        r0 = pl.multiple_of(n * blk, blk)
        rows = pl.ds(r0, blk)
        k0 = pl.multiple_of(jnp.clip(r0 - blk, 0, seq - win), blk)
        cos = cos_ref[rows, :]
        sin = sin_ref[rows, :]
        valid = is_ctx | (jnp.abs(r0 + rowq - (k0 + colk)) <= SWA_WINDOW)
        for g in range(SWA_KV_HEADS):
            hs = slice(g * HEAD_DIM, (g + 1) * HEAD_DIM)
            qg = jnp.concatenate(
                [_rope(q_ref[0, rows, (g * grp + j) * HEAD_DIM:(g * grp + j + 1) * HEAD_DIM].astype(F32),
                       cos, sin, lane) for j in range(grp)], axis=0).astype(BF16)
            kk = jnp.concatenate([kr_ref[pl.ds(k0, win), hs], kc_ref[0, :, hs]], axis=0)
            vv = jnp.concatenate([v_ref[0, pl.ds(k0, win), hs], vc_ref[0, :, hs]], axis=0)
            s = jnp.where(valid, _dot_nt(qg, kk) * scale, NEG_BIG)
            o = _softmax_sink_pv(s, sink_column(g, blk), vv)
            for j in range(grp):
                cs = slice((g * grp + j) * HEAD_DIM, (g * grp + j + 1) * HEAD_DIM)
                y_ref[0, rows, cs] = (o[j * blk:(j + 1) * blk, :]
                                      * _silu(gate_ref[0, rows, cs].astype(F32))).astype(BF16)
        return _

    lax.fori_loop(0, nb, q_block, None, unroll=2)

    if with_ctx:
        for g in range(SWA_KV_HEADS):
            hs = slice(g * HEAD_DIM, (g + 1) * HEAD_DIM)
            qg = jnp.concatenate([qc_ref[0, :, (g * grp + j) * HEAD_DIM:(g * grp + j + 1) * HEAD_DIM]
                                  for j in range(grp)], axis=0)
            s = _dot_nt(qg, kc_ref[0, :, hs]) * scale
            o = _softmax_sink_pv(s, sink_column(g, ctx_len), vc_ref[0, :, hs])
            for j in range(grp):
                cs = slice((g * grp + j) * HEAD_DIM, (g * grp + j + 1) * HEAD_DIM)
                yc_ref[0, :, cs] = (o[j * ctx_len:(j + 1) * ctx_len, :]
                                    * _silu(gatec_ref[0, :, cs].astype(F32))).astype(BF16)


def _swa(big, bigc, cos, sin, sink, *, with_ctx):
    b, l, _ = big.shape
    lc = bigc.shape[1]
    in_specs = [pl.BlockSpec((1, l, 512), lambda bi: (bi, 0, 11)),
                pl.BlockSpec((1, l, 256), lambda bi: (bi, 0, 24)),
                pl.BlockSpec((1, l, 256), lambda bi: (bi, 0, 25)),
                pl.BlockSpec((1, l, 512), lambda bi: (bi, 0, 13)),
                pl.BlockSpec((1, lc, 256), lambda bi: (bi, 0, 24)),
                pl.BlockSpec((1, lc, 256), lambda bi: (bi, 0, 25)),
                pl.BlockSpec((l, HEAD_DIM), lambda bi: (0, 0)),
                pl.BlockSpec((l, HEAD_DIM), lambda bi: (0, 0)),
                pl.BlockSpec((1, LANES), lambda bi: (0, 0))]
    args = [big, big, big, big, bigc, bigc, cos, sin, sink]
    out_shape = [jax.ShapeDtypeStruct((b, l, GROUP_W), BF16)]
    out_specs = [pl.BlockSpec((1, l, GROUP_W), lambda bi: (bi, 0, 0))]
    if with_ctx:
        in_specs += [pl.BlockSpec((1, lc, 512), lambda bi: (bi, 0, 11)),
                     pl.BlockSpec((1, lc, 512), lambda bi: (bi, 0, 13))]
        args += [bigc, bigc]
        out_shape.append(jax.ShapeDtypeStruct((b, lc, GROUP_W), BF16))
        out_specs.append(pl.BlockSpec((1, lc, GROUP_W), lambda bi: (bi, 0, 0)))
    res = pl.pallas_call(
        functools.partial(_swa_kernel, seq=l, ctx_len=lc, with_ctx=with_ctx),
        out_shape=tuple(out_shape), grid=(b,), in_specs=in_specs, out_specs=tuple(out_specs),
        scratch_shapes=[pltpu.VMEM((l, SWA_KV_HEADS * HEAD_DIM), BF16)],
        compiler_params=_cparams(("parallel",)),
        name="swa_ctx" if with_ctx else "swa",
    )(*args)
    return (res[0], res[1]) if with_ctx else (res[0], None)


def _rope_tables(seq_len):
    rows = seq_len // GRID_W
    row = jnp.repeat(jnp.arange(rows, dtype=F32), GRID_W)
    col = jnp.tile(jnp.arange(GRID_W, dtype=F32), rows)
    axis_dim = HEAD_DIM // 2
    inv_freq = ROPE_BASE ** (-jnp.arange(0, axis_dim, 2, dtype=F32) / axis_dim)
    ang_r = row[:, None] * inv_freq
    ang_c = col[:, None] * inv_freq
    cos = jnp.concatenate([jnp.cos(ang_r), jnp.cos(ang_r), jnp.cos(ang_c), jnp.cos(ang_c)], axis=-1)
    sin = jnp.concatenate([-jnp.sin(ang_r), jnp.sin(ang_r), -jnp.sin(ang_c), jnp.sin(ang_c)], axis=-1)
    return cos, sin


_PACK_RUNS = ((0, 0, 1024), (1024, 1056, 2048), (3072, 3120, 4096))
_LR_COL0 = 1024
_GDN_GATE_COL0 = 3104


def _pack_kernel(wt_ref, big_ref, small_ref):
    step = 512
    for dst, src, width in _PACK_RUNS:
        for o in range(0, width, step):
            big_ref[0, :, dst + o:dst + o + step] = wt_ref[0, src + o:src + o + step, :].T.astype(BF16)
    assert _LR_COL0 % LANES == 0 and _GDN_GATE_COL0 % LANES == SM_A[0]
    gate_win0 = _GDN_GATE_COL0 - SM_A[0]
    lane = _iota2((wt_ref.shape[2], LANES), 1)
    lr_win = wt_ref[0, _LR_COL0:_LR_COL0 + LANES, :].T
    gate_win = wt_ref[0, gate_win0:gate_win0 + LANES, :].T
    small = jnp.where(lane < SM_A[0], lr_win, jnp.where(lane < SM_B[1] + N_HEADS, gate_win, 0.0))
    small_ref[0] = small.astype(BF16)


def _pack_w_in(w_in):
    n_layers, d, n = w_in.shape
    kb = 256
    return pl.pallas_call(
        _pack_kernel,
        out_shape=(jax.ShapeDtypeStruct((n_layers, d, BIG_W), BF16),
                   jax.ShapeDtypeStruct((n_layers, d, SMALL_W), BF16)),
        grid=(n_layers, d // kb),
        in_specs=[pl.BlockSpec((1, n, kb), lambda l, i: (l, 0, i))],
        out_specs=(pl.BlockSpec((1, kb, BIG_W), lambda l, i: (l, i, 0)),
                   pl.BlockSpec((1, kb, SMALL_W), lambda l, i: (l, i, 0))),
        compiler_params=_cparams(("parallel", "parallel")),
        name="pack_w_in",
    )(jnp.swapaxes(w_in, 1, 2))


def kernel(x, c, ctx, c_ctx, ada_w, ada_b, norm_pre, norm_post, w_in, w_out, gla_w_decay, gla_b_decay,
           gla_norm, gdn_conv, gdn_a_log, gdn_dt_bias, gdn_norm, sc_conv, swa_sink):
    b, l, d = x.shape
    cos, sin = _rope_tables(l)
    cc = jnp.zeros((8, d), F32).at[:b].set(c).at[b].set(c_ctx)
    mod = _ada(cc, ada_w, ada_b)
    h, hc = x, ctx
    lc = ctx.shape[1]
    n_layers = ada_w.shape[0]
    w_big, w_small = _pack_w_in(w_in)
    w_o = w_out.astype(BF16)
    for layer in range(n_layers):
        with_ctx = layer < n_layers - 1
        shift, scale, gate = (mod[layer, :, i * d:(i + 1) * d] for i in range(3))

        def per_batch(t):
            return t[:b, None, :], t[b][None, None, :]
        (shift_l, shift_c), (scale_l, scale_c), (gate_l, gate_c) = per_batch(shift), per_batch(scale), per_batch(gate)
        g_pre = norm_pre[layer].reshape(1, d)
        big, small = _inproj(h, scale_l, shift_l, g_pre, w_big, w_small, layer, tm=min(l, 1024))
        bigc, smallc = _inproj(hc.reshape(1, b * lc, d), scale_c, shift_c, g_pre, w_big, w_small, layer,
                               tm=min(b * lc, 1024))
        bigc, smallc = bigc.reshape(b, lc, BIG_W), smallc.reshape(b, lc, SMALL_W)

        zf = jnp.zeros((b, HEAD_DIM, N_HEADS * GLA_DK), F32)
        yca, stf, stb = _gla(bigc, smallc, gla_w_decay[layer], gla_b_decay[layer], gla_norm[layer], zf, zf,
                             with_out=with_ctx)
        ya, _, _ = _gla(big, small, gla_w_decay[layer], gla_b_decay[layer], gla_norm[layer], stf, stb,
                        with_out=True)

        gp = _gdn_gate_params(gdn_a_log[layer], gdn_dt_bias[layer])
        zg = jnp.zeros((b, N_HEADS * HEAD_DIM, HEAD_DIM), F32)
        ycb, gtf, gtb = _gdn(bigc, smallc, gdn_conv[layer], gp, gdn_norm[layer], zg, zg, with_out=with_ctx)
        yb, _, _ = _gdn(big, small, gdn_conv[layer], gp, gdn_norm[layer], gtf, gtb, with_out=True)

        yc_ = _sconv(big, sc_conv[layer])
        sink = jnp.zeros((1, LANES), F32).at[0, :N_HEADS].set(swa_sink[layer])
        yd, ycd = _swa(big, bigc, cos, sin, sink, with_ctx=with_ctx)

        g_post = norm_post[layer].reshape(1, d)
        h_new = _outproj(ya, yb, yc_, yd, w_o, layer, h, gate_l, g_post, tm=min(l, 512))
        if with_ctx:
            ycc = _sconv(bigc, sc_conv[layer])
            flat = [t.reshape(1, b * lc, GROUP_W) for t in (yca, ycb, ycc, ycd)]
            hc = _outproj(*flat, w_o, layer, hc.reshape(1, b * lc, d), gate_c, g_post,
                          tm=min(b * lc, 512)).reshape(b, lc, d)
        h = h_new
    return h
```

```python
import functools

import jax
import jax.numpy as jnp
from jax import lax
from jax.experimental import pallas as pl
from jax.experimental.pallas import tpu as pltpu

F32 = jnp.float32
BF16 = jnp.bfloat16

D_MODEL = 2048
N_LAYERS = 2
GRID_W = 64
GROUP_W = 512
HEAD_DIM = 128
N_HEADS = 4
EPS = 1e-6
GLA_DK = 64
GLA_RANK = 16
GLA_TAU = 16.0
GDN_CONV = 5
SC_CONV = 3
SWA_KV_HEADS = 2
SWA_WINDOW = 128
ROPE_BASE = 10000.0

_SPLITS = (
    ("gla_q", 256), ("gla_k", 256), ("gla_v", 512), ("gla_lr_f", 16), ("gla_lr_b", 16), ("gla_gate", 512),
    ("gdn_q", 512), ("gdn_k", 512), ("gdn_v", 512),
    ("gdn_a_f", 4), ("gdn_a_b", 4), ("gdn_b_f", 4), ("gdn_b_b", 4), ("gdn_gate", 512),
    ("sc_b", 512), ("sc_c", 512), ("sc_h", 512), ("sc_gate", 512),
    ("swa_q", 512), ("swa_k", 256), ("swa_v", 256), ("swa_gate", 512),
)
_BIG_ORDER = ("gla_q", "gla_k", "gla_v", "gla_gate", "gdn_q", "gdn_k", "gdn_v", "gdn_gate",
              "sc_b", "sc_c", "sc_h", "sc_gate", "swa_q", "swa_k", "swa_v", "swa_gate")
_SMALL_ORDER = ("gla_lr_f", "gla_lr_b", "gdn_a_f", "gdn_a_b", "gdn_b_f", "gdn_b_b")
BIG_W = 7168
SMALL_W = 128
SM_LR = (0, 16)
SM_A = (32, 36)
SM_B = (40, 44)

LANES = 128
VMEM_LIMIT = 56 * 1024 * 1024

CHUNK = 128
INV_BASE = 2
SWA_BLOCK = 128

NEG_BIG = -1e30


def _cparams(sem):
    return pltpu.CompilerParams(dimension_semantics=sem, vmem_limit_bytes=VMEM_LIMIT)


def _dot(a, b):
    return lax.dot_general(a.astype(BF16), b.astype(BF16), (((1,), (0,)), ((), ())),
                           preferred_element_type=F32)


def _dot_nt(a, b):
    return lax.dot_general(a.astype(BF16), b.astype(BF16), (((1,), (1,)), ((), ())),
                           preferred_element_type=F32)


def _dot_tn(a, b):
    return lax.dot_general(a.astype(BF16), b.astype(BF16), (((0,), (0,)), ((), ())),
                           preferred_element_type=F32)


def _split(x):
    hi = x.astype(BF16)
    lo = (x - hi.astype(F32)).astype(BF16)
    return hi, lo


def _dot_exact_lhs(a_bf16, x):
    hi, lo = _split(x)
    return _dot(a_bf16, hi) + _dot(a_bf16, lo)


def _dot_exact_rhs(x, b_bf16):
    hi, lo = _split(x)
    return _dot(hi, b_bf16) + _dot(lo, b_bf16)


def _silu(x):
    return x * (1.0 / (1.0 + jnp.exp(-x)))


def _sigmoid(x):
    return 1.0 / (1.0 + jnp.exp(-x))


def _softplus(x):
    return jnp.maximum(x, 0.0) + jnp.log(1.0 + jnp.exp(-jnp.abs(x)))


def _log_sigmoid(x):
    return jnp.minimum(x, 0.0) - jnp.log(1.0 + jnp.exp(-jnp.abs(x)))


def _iota2(shape, dim):
    return lax.broadcasted_iota(jnp.int32, shape, dim)


ADA_K_SPLIT = 4


def _ada_kernel(c_ref, *refs):
    w_refs, (b_ref, o_ref) = refs[:ADA_K_SPLIT], refs[ADA_K_SPLIT:]
    a = _silu(c_ref[...])
    kq = a.shape[1] // ADA_K_SPLIT
    acc = b_ref[0]
    for q, w_ref in enumerate(w_refs):
        acc = acc + _dot(a[:, q * kq:(q + 1) * kq], w_ref[0])
    o_ref[0] = acc


def _ada(cc, ada_w, ada_b):
    n_layers, d, n = ada_w.shape
    tn = 1024
    kq = d // ADA_K_SPLIT
    w_specs = [pl.BlockSpec((1, kq, tn), lambda l, j, q=q: (l, q, j)) for q in range(ADA_K_SPLIT)]
    return pl.pallas_call(
        _ada_kernel,
        out_shape=jax.ShapeDtypeStruct((n_layers, 8, n), F32),
        grid=(n_layers, n // tn),
        in_specs=[pl.BlockSpec((8, d), lambda l, j: (0, 0))] + w_specs
        + [pl.BlockSpec((1, 1, tn), lambda l, j: (l, 0, j))],
        out_specs=pl.BlockSpec((1, 8, tn), lambda l, j: (l, 0, j)),
        compiler_params=_cparams(("parallel", "parallel")),
        name="ada_mod",
    )(cc, *([ada_w] * ADA_K_SPLIT), ada_b.reshape(n_layers, 1, n))


def _inproj_kernel(x_ref, sc_ref, sh_ref, g_ref, wb_ref, ws_ref, ob_ref, os_ref, xn_ref):
    @pl.when(pl.program_id(2) == 0)
    def _():
        rows_per_step = 128

        def norm_rows(i, _):
            rows = pl.ds(pl.multiple_of(i * rows_per_step, rows_per_step), rows_per_step)
            x = x_ref[0, rows, :]
            y = x * lax.rsqrt(jnp.mean(x * x, axis=-1, keepdims=True) + EPS) * g_ref[...]
            xn_ref[rows, :] = (y * (1.0 + sc_ref[0]) + sh_ref[0]).astype(BF16)
            return _

        lax.fori_loop(0, xn_ref.shape[0] // rows_per_step, norm_rows, None)
        os_ref[0] = jnp.dot(xn_ref[...], ws_ref[0], preferred_element_type=F32)

    ob_ref[0] = jnp.dot(xn_ref[...], wb_ref[0], preferred_element_type=F32).astype(BF16)


def _inproj(x, scale, shift, g, w_big, w_small, layer, *, tm, tn=1024):
    b, l, d = x.shape
    return pl.pallas_call(
        _inproj_kernel,
        out_shape=(jax.ShapeDtypeStruct((b, l, BIG_W), BF16), jax.ShapeDtypeStruct((b, l, SMALL_W), F32)),
        grid=(b, l // tm, BIG_W // tn),
        in_specs=[pl.BlockSpec((1, tm, d), lambda bi, i, j: (bi, i, 0)),
                  pl.BlockSpec((1, 1, d), lambda bi, i, j: (bi, 0, 0)),
                  pl.BlockSpec((1, 1, d), lambda bi, i, j: (bi, 0, 0)),
                  pl.BlockSpec((1, d), lambda bi, i, j: (0, 0)),
                  pl.BlockSpec((1, d, tn), lambda bi, i, j: (layer, 0, j)),
                  pl.BlockSpec((1, d, SMALL_W), lambda bi, i, j: (layer, 0, 0))],
        out_specs=(pl.BlockSpec((1, tm, tn), lambda bi, i, j: (bi, i, j)),
                   pl.BlockSpec((1, tm, SMALL_W), lambda bi, i, j: (bi, i, 0))),
        scratch_shapes=[pltpu.VMEM((tm, d), BF16)],
        compiler_params=_cparams(("parallel", "parallel", "arbitrary")),
        name="inproj",
    )(x, scale, shift, g, w_big, w_small)


def _outproj_kernel(ya_ref, yb_ref, yc_ref, yd_ref, w_ref, hl_ref, hr_ref, gate_ref, g_ref, o_ref):
    y = jnp.dot(ya_ref[0], w_ref[0, 0:GROUP_W, :], preferred_element_type=F32)
    y += jnp.dot(yb_ref[0], w_ref[0, GROUP_W:2 * GROUP_W, :], preferred_element_type=F32)
    y += jnp.dot(yc_ref[0], w_ref[0, 2 * GROUP_W:3 * GROUP_W, :], preferred_element_type=F32)
    y += jnp.dot(yd_ref[0], w_ref[0, 3 * GROUP_W:4 * GROUP_W, :], preferred_element_type=F32)
    yn = gate_ref[0] * (y * lax.rsqrt(jnp.mean(y * y, axis=-1, keepdims=True) + EPS) * g_ref[...])
    half = hl_ref.shape[2]
    o_ref[0, :, 0:half] = hl_ref[0] + yn[:, 0:half]
    o_ref[0, :, half:2 * half] = hr_ref[0] + yn[:, half:2 * half]


def _outproj(ya, yb, yc, yd, w_out, layer, h, gate, g, *, tm):
    b, l, d = h.shape
    yspec = pl.BlockSpec((1, tm, GROUP_W), lambda bi, i: (bi, i, 0))
    return pl.pallas_call(
        _outproj_kernel,
        out_shape=jax.ShapeDtypeStruct((b, l, d), F32),
        grid=(b, l // tm),
        in_specs=[yspec, yspec, yspec, yspec,
                  pl.BlockSpec((1, d, d), lambda bi, i: (layer, 0, 0)),
                  pl.BlockSpec((1, tm, d // 2), lambda bi, i: (bi, i, 0)),
                  pl.BlockSpec((1, tm, d // 2), lambda bi, i: (bi, i, 1)),
                  pl.BlockSpec((1, 1, d), lambda bi, i: (bi, 0, 0)),
                  pl.BlockSpec((1, d), lambda bi, i: (0, 0))],
        out_specs=pl.BlockSpec((1, tm, d), lambda bi, i: (bi, i, 0)),
        compiler_params=_cparams(("parallel", "parallel")),
        name="outproj",
    )(ya, yb, yc, yd, w_out, h, h, gate, g)


def _sconv_kernel(b_ref, c_ref, h_ref, gate_ref, w_ref, o_ref, p_ref, *, seq):
    zeros = jnp.zeros((8, GROUP_W), F32)
    p_ref[0:8, :] = zeros
    p_ref[seq + 8:seq + 16, :] = zeros
    p_ref[8:seq + 8, :] = c_ref[0].astype(F32) * h_ref[0].astype(F32)
    rows = min(seq, 256)
    w = w_ref[...]
    for t0 in range(0, seq, rows):
        conv = (p_ref[t0 + 7:t0 + 7 + rows, :] * w[0:1, :] + p_ref[t0 + 8:t0 + 8 + rows, :] * w[1:2, :]
                + p_ref[t0 + 9:t0 + 9 + rows, :] * w[2:3, :])
        y = b_ref[0, t0:t0 + rows, :].astype(F32) * conv * _silu(gate_ref[0, t0:t0 + rows, :].astype(F32))
        o_ref[0, t0:t0 + rows, :] = y.astype(BF16)


def _sconv(big, w):
    b, l, _ = big.shape
    col0 = 3584 // GROUP_W

    def spec(k):
        return pl.BlockSpec((1, l, GROUP_W), lambda bi, k=k: (bi, 0, col0 + k))
    return pl.pallas_call(
        functools.partial(_sconv_kernel, seq=l),
        out_shape=jax.ShapeDtypeStruct((b, l, GROUP_W), BF16),
        grid=(b,),
        in_specs=[spec(0), spec(1), spec(2), spec(3), pl.BlockSpec((SC_CONV, GROUP_W), lambda bi: (0, 0))],
        out_specs=pl.BlockSpec((1, l, GROUP_W), lambda bi: (bi, 0, 0)),
        scratch_shapes=[pltpu.VMEM((l + 16, GROUP_W), F32)],
        compiler_params=_cparams(("parallel",)),
        name="sconv",
    )(big, big, big, big, w)


def _gla_kernel(q_ref, k_ref, v_ref, gate_ref, sm_ref, wd_ref, bd_ref, nw_ref, s0f_ref, s0b_ref,
                *rest, seq, with_out):
    if with_out:
        y_ref, stf_ref, stb_ref, o_ref, st_ref, ko_ref, dec_ref, qp_ref, kp_ref, qi_ref = rest
    else:
        stf_ref, stb_ref, st_ref, ko_ref, dec_ref = rest
        y_ref = o_ref = qp_ref = kp_ref = qi_ref = None
    c = CHUNK
    n_chunks = seq // c
    ri = _iota2((c, c), 0)
    ci = _iota2((c, c), 1)
    incl = (ri >= ci, ri <= ci)
    tri = tuple(jnp.where(m, 1.0, 0.0).astype(BF16) for m in incl)
    qscale = GLA_DK ** -0.5
    last = (c - 1, 0)
    chains = [(d, h) for d in (0, 1) for h in range(N_HEADS)]

    def prep(n, _):
        rows = pl.ds(pl.multiple_of(n * c, c), c)
        sm = sm_ref[0, rows, :]
        kf = k_ref[0, rows, :].astype(F32)
        if with_out:
            qf = q_ref[0, rows, :].astype(F32) * qscale
        for d in (0, 1):
            z = _dot(sm[:, SM_LR[d]:SM_LR[d] + GLA_RANK], wd_ref[d]) + bd_ref[d]
            la = _log_sigmoid(z) * (1.0 / GLA_TAU)
            cum = _dot_exact_lhs(tri[d], la)
            tot = cum[last[d]:last[d] + 1, :]
            ko_ref[d, rows, :] = (kf * jnp.exp(tot - cum)).astype(BF16)
            dec_ref[d, pl.ds(pl.multiple_of(n * 8, 8), 8), :] = jnp.broadcast_to(jnp.exp(tot), (8, tot.shape[1]))
            if with_out:
                mid = cum[c // 2:c // 2 + 1, :]
                qp_ref[d, rows, :] = (qf * jnp.exp(cum - mid)).astype(BF16)
                kp_ref[d, rows, :] = (kf * jnp.exp(mid - cum)).astype(BF16)
                qi_ref[d, rows, :] = (qf * jnp.exp(cum)).astype(BF16)
        return _

    lax.fori_loop(0, n_chunks, prep, None, unroll=min(4, n_chunks))

    def step(i, _):
        chunk_of = (i, n_chunks - 1 - i)
        rows = [pl.ds(pl.multiple_of(n * c, c), c) for n in chunk_of]
        tots, kos, sts, vs, qps, kps, qis, stbs = [], [], [], [], [], [], [], []
        for d in (0, 1):
            tots.append(dec_ref[d, pl.ds(pl.multiple_of(chunk_of[d] * 8, 8), 1), :])
            kos.append(ko_ref[d, rows[d], :])
            sts.append(st_ref[d])
            vs.append(v_ref[0, rows[d], :])
            if with_out:
                qps.append(qp_ref[d, rows[d], :])
                kps.append(kp_ref[d, rows[d], :])
                qis.append(qi_ref[d, rows[d], :])
                stbs.append(sts[d].astype(BF16))

        def dk(h):
            return slice(h * GLA_DK, (h + 1) * GLA_DK)

        def dv(h):
            return slice(h * HEAD_DIM, (h + 1) * HEAD_DIM)

        if with_out:
            attn = [jnp.where(incl[d], _dot_nt(qps[d][:, dk(h)], kps[d][:, dk(h)]), 0.0) for d, h in chains]
            inter = [_dot_nt(qis[d][:, dk(h)], stbs[d][:, dk(h)]) for d, h in chains]
            intra = [_dot(a, vs[d][:, dv(h)]) for a, (d, h) in zip(attn, chains)]
            for (d, h), o1, o2 in zip(chains, inter, intra):
                o_ref[d, rows[d], dv(h)] = o1 + o2
        upd = [_dot_tn(vs[d][:, dv(h)], kos[d][:, dk(h)]) for d, h in chains]
        for d in (0, 1):
            st_ref[d] = sts[d] * tots[d] + jnp.concatenate(upd[d * N_HEADS:(d + 1) * N_HEADS], axis=1)
        return _

    st_ref[0] = s0f_ref[0]
    st_ref[1] = s0b_ref[0]
    lax.fori_loop(0, n_chunks, step, None, unroll=2)
    stf_ref[0] = st_ref[0]
    stb_ref[0] = st_ref[1]

    if with_out:
        def finish(n, _):
            rows = pl.ds(pl.multiple_of(n * c, c), c)
            for h in range(N_HEADS):
                hs = slice(h * HEAD_DIM, (h + 1) * HEAD_DIM)
                o = o_ref[0, rows, hs] + o_ref[1, rows, hs]
                o = o * lax.rsqrt(jnp.mean(o * o, axis=-1, keepdims=True) + EPS) * nw_ref[...]
                y_ref[0, rows, hs] = (o * _silu(gate_ref[0, rows, hs].astype(F32))).astype(BF16)
            return _

        lax.fori_loop(0, n_chunks, finish, None)


def _gla(big, small, wd, bd, nw, s0f, s0b, *, with_out):
    b, l, _ = big.shape
    st_shape = jax.ShapeDtypeStruct((b, HEAD_DIM, N_HEADS * GLA_DK), F32)
    st_spec = pl.BlockSpec((1, HEAD_DIM, N_HEADS * GLA_DK), lambda bi: (bi, 0, 0))
    in_specs = [pl.BlockSpec((1, l, 256), lambda bi: (bi, 0, 0)),
                pl.BlockSpec((1, l, 256), lambda bi: (bi, 0, 1)),
                pl.BlockSpec((1, l, 512), lambda bi: (bi, 0, 1)),
                pl.BlockSpec((1, l, 512), lambda bi: (bi, 0, 2)),
                pl.BlockSpec((1, l, SMALL_W), lambda bi: (bi, 0, 0)),
                pl.BlockSpec((2, GLA_RANK, 256), lambda bi: (0, 0, 0)),
                pl.BlockSpec((2, 1, 256), lambda bi: (0, 0, 0)),
                pl.BlockSpec((1, HEAD_DIM), lambda bi: (0, 0)),
                st_spec, st_spec]
    out_shape = [st_shape, st_shape]
    out_specs = [st_spec, st_spec]
    kw = N_HEADS * GLA_DK
    scratch = [pltpu.VMEM((2, HEAD_DIM, kw), F32),
               pltpu.VMEM((2, l, kw), BF16),
               pltpu.VMEM((2, 8 * (l // CHUNK), kw), F32)]
    if with_out:
        out_shape = [jax.ShapeDtypeStruct((b, l, GROUP_W), BF16)] + out_shape
        out_specs = [pl.BlockSpec((1, l, GROUP_W), lambda bi: (bi, 0, 0))] + out_specs
        scratch = [pltpu.VMEM((2, l, GROUP_W), F32)] + scratch + [pltpu.VMEM((2, l, kw), BF16)] * 3
    res = pl.pallas_call(
        functools.partial(_gla_kernel, seq=l, with_out=with_out),
        out_shape=tuple(out_shape), grid=(b,), in_specs=in_specs, out_specs=tuple(out_specs),
        scratch_shapes=scratch, compiler_params=_cparams(("parallel",)),
        name="gla_out" if with_out else "gla_state",
    )(big, big, big, big, small, wd, bd.reshape(2, 1, -1), nw.reshape(1, -1), s0f, s0b)
    return res if with_out else (None,) + tuple(res)


def _unit_triangular_inverses(lows, ri, ci):
    c = lows[0].shape[0]
    eye = jnp.where(ri == ci, 1.0, 0.0)
    base = (ri // INV_BASE) == (ci // INV_BASE)
    ds = [jnp.where(base, low, 0.0).astype(BF16) for low in lows]
    ts = [eye - d.astype(F32) for d in ds]
    power = 2
    if power < INV_BASE:
        ms = [_dot(d, d) for d in ds]
    while power < INV_BASE:
        mb = [m.astype(BF16) for m in ms]
        ts = [t + _dot(t, m16) for t, m16 in zip(ts, mb)]
        power *= 2
        if power < INV_BASE:
            ms = [_dot(m16, m16) for m16 in mb]
    s = INV_BASE
    while s < c:
        sel = ((ri // (2 * s)) == (ci // (2 * s))) & ((ri // s) != (ci // s))
        tb = [t.astype(BF16) for t in ts]
        ps = [_dot(t, jnp.where(sel, low, 0.0)) for t, low in zip(tb, lows)]
        ts = [t - _dot(p, t16) for t, p, t16 in zip(ts, ps, tb)]
        s *= 2
    return ts


def _gdn_kernel(qkv_ref, gate_ref, sm_ref, cw_ref, gp_ref, nw_ref, s0f_ref, s0b_ref,
                *rest, seq, with_out):
    if with_out:
        y_ref, stf_ref, stb_ref, x_ref, o_ref, st_ref = rest
    else:
        stf_ref, stb_ref, x_ref, st_ref = rest
        y_ref = o_ref = None
    c = CHUNK
    n_chunks = seq // c
    width = 3 * GROUP_W
    ri = _iota2((c, c), 0)
    ci = _iota2((c, c), 1)
    lower = ri >= ci
    upper = ri <= ci
    tril = jnp.where(lower, 1.0, 0.0).astype(BF16)
    triu = jnp.where(upper, 1.0, 0.0).astype(BF16)

    halo = 16
    off_taps = tuple(j for j in range(GDN_CONV) if j != GDN_CONV // 2)
    sr = _iota2((len(off_taps) * c, c + 2 * halo), 0)
    sc_ = _iota2((len(off_taps) * c, c + 2 * halo), 1)
    tap = jnp.zeros_like(sr)
    for idx, j in enumerate(off_taps):
        tap = jnp.where(sr // c == idx, j, tap)
    shift_mat = jnp.where(sc_ == (sr % c) + halo - GDN_CONV // 2 + tap, 1.0, 0.0).astype(BF16)

    def conv_chunk(n, _):
        r0 = pl.multiple_of(n * c, c)
        prev0 = pl.multiple_of(jnp.maximum(r0 - 16, 0), 16)
        next0 = pl.multiple_of(jnp.minimum(r0 + c, seq - 16), 16)
        prev = jnp.where(n > 0, qkv_ref[0, pl.ds(prev0, 16), :].astype(F32), 0.0).astype(BF16)
        nxt = jnp.where(n < n_chunks - 1, qkv_ref[0, pl.ds(next0, 16), :].astype(F32), 0.0).astype(BF16)
        cur = qkv_ref[0, pl.ds(r0, c), :]
        xe = jnp.concatenate([prev, cur, nxt], axis=0)
        shifted = _dot(shift_mat, xe)
        acc = cur.astype(F32) * cw_ref[GDN_CONV // 2:GDN_CONV // 2 + 1, :]
        for idx, j in enumerate(off_taps):
            acc = acc + shifted[idx * c:(idx + 1) * c, :] * cw_ref[j:j + 1, :]
        acc = _silu(acc)
        for h in range(2 * N_HEADS):
            hs = slice(h * HEAD_DIM, (h + 1) * HEAD_DIM)
            t = acc[:, hs]
            t = t * lax.rsqrt(jnp.sum(t * t, axis=-1, keepdims=True) + EPS)
            if h < N_HEADS:
                t = t * (HEAD_DIM ** -0.5)
            x_ref[pl.ds(r0, c), hs] = t
        x_ref[pl.ds(r0, c), 2 * GROUP_W:width] = acc[:, 2 * GROUP_W:width]
        return _

    lax.fori_loop(0, n_chunks, conv_chunk, None)

    chains = [(d, h) for d in (0, 1) for h in range(N_HEADS)]
    n_state = N_HEADS * HEAD_DIM

    def step(i, _):
        rows = [pl.ds(pl.multiple_of(n * c, c), c) for n in (i, n_chunks - 1 - i)]
        cums, cum_rows, betas = [], [], []
        for d in (0, 1):
            sm = sm_ref[0, rows[d], :]
            g = gp_ref[0:1, :] * _softplus(sm + gp_ref[1:2, :])
            betas.append(_sigmoid(sm))
            cums.append(_dot_exact_lhs(tril if d == 0 else triu, g))
            cum_rows.append(_dot_exact_rhs(g.T, triu if d == 0 else tril))
        incl = (lower, upper)
        strict = (ri > ci, ri < ci)
        last = (c - 1, 0)

        kfs, vbs, kbs, e_cs, tots, decays, kos, qfs = [], [], [], [], [], [], [], []
        for d, h in chains:
            la, lb = SM_A[d] + h, SM_B[d] + h
            cum_c = cums[d][:, la:la + 1]
            cum_r = cum_rows[d][la:la + 1, :]
            tot = cums[d][last[d]:last[d] + 1, la:la + 1]
            beta_c = betas[d][:, lb:lb + 1]
            kf = x_ref[rows[d], GROUP_W + h * HEAD_DIM:GROUP_W + (h + 1) * HEAD_DIM]
            vf = x_ref[rows[d], 2 * GROUP_W + h * HEAD_DIM:2 * GROUP_W + (h + 1) * HEAD_DIM]
            e_c = jnp.exp(cum_c)
            kfs.append(kf)
            kbs.append(kf * beta_c)
            vbs.append(vf * beta_c)
            e_cs.append(e_c)
            tots.append(tot)
            decays.append(jnp.where(incl[d], jnp.exp(jnp.minimum(cum_c - cum_r, 0.0)), 0.0))
            kos.append(kf * jnp.exp(tot - cum_c))
            if with_out:
                qfs.append(x_ref[rows[d], h * HEAD_DIM:(h + 1) * HEAD_DIM])

        if with_out:
            prods = [_dot_nt(jnp.concatenate([kb, qf], axis=0), kf) for kb, qf, kf in zip(kbs, qfs, kfs)]
            attns = [p[c:, :] * dec for p, dec in zip(prods, decays)]
        else:
            prods = [_dot_nt(kb, kf) for kb, kf in zip(kbs, kfs)]
        lows = [jnp.where(strict[d], p[:c, :] * dec, 0.0) for (d, _), p, dec in zip(chains, prods, decays)]
        ts = _unit_triangular_inverses(lows, ri, ci)
        uws = [_dot(t, jnp.concatenate([vb, kb * e_c], axis=1)) for t, vb, kb, e_c in zip(ts, vbs, kbs, e_cs)]
        sts = [st_ref[j * HEAD_DIM:(j + 1) * HEAD_DIM, :] for j in range(len(chains))]
        if with_out:
            reads = [_dot(jnp.concatenate([uw[:, HEAD_DIM:], qf * e_c], axis=0), st)
                     for uw, qf, e_c, st in zip(uws, qfs, e_cs, sts)]
        else:
            reads = [_dot(uw[:, HEAD_DIM:], st) for uw, st in zip(uws, sts)]
        v_news = [uw[:, :HEAD_DIM] - r[:c, :] for uw, r in zip(uws, reads)]
        if with_out:
            writes = [_dot(jnp.concatenate([a, ko.T], axis=0), vn) for a, ko, vn in zip(attns, kos, v_news)]
        else:
            writes = [_dot_tn(ko, vn) for ko, vn in zip(kos, v_news)]
        for j, (d, h) in enumerate(chains):
            if with_out:
                o_ref[d, rows[d], h * HEAD_DIM:(h + 1) * HEAD_DIM] = reads[j][c:, :] + writes[j][:c, :]
                upd = writes[j][c:, :]
            else:
                upd = writes[j]
            st_ref[j * HEAD_DIM:(j + 1) * HEAD_DIM, :] = sts[j] * jnp.exp(tots[j]) + upd
        return _

    st_ref[0:n_state, :] = s0f_ref[0]
    st_ref[n_state:2 * n_state, :] = s0b_ref[0]
    lax.fori_loop(0, n_chunks, step, None, unroll=2)
    stf_ref[0] = st_ref[0:n_state, :]
    stb_ref[0] = st_ref[n_state:2 * n_state, :]

    if with_out:
        def finish(n, _):
            rows = pl.ds(pl.multiple_of(n * c, c), c)
            for h in range(N_HEADS):
                hs = slice(h * HEAD_DIM, (h + 1) * HEAD_DIM)
                o = o_ref[0, rows, hs] + o_ref[1, rows, hs]
                o = o * lax.rsqrt(jnp.mean(o * o, axis=-1, keepdims=True) + EPS) * nw_ref[...]
                y_ref[0, rows, hs] = (o * _silu(gate_ref[0, rows, hs].astype(F32))).astype(BF16)
            return _

        lax.fori_loop(0, n_chunks, finish, None)


def _gdn(big, small, cw, gp, nw, s0f, s0b, *, with_out):
    b, l, _ = big.shape
    st_shape = jax.ShapeDtypeStruct((b, N_HEADS * HEAD_DIM, HEAD_DIM), F32)
    st_spec = pl.BlockSpec((1, N_HEADS * HEAD_DIM, HEAD_DIM), lambda bi: (bi, 0, 0))
    in_specs = [pl.BlockSpec((1, l, 3 * GROUP_W), lambda bi: (bi, 0, 1)),
                pl.BlockSpec((1, l, GROUP_W), lambda bi: (bi, 0, 6)),
                pl.BlockSpec((1, l, SMALL_W), lambda bi: (bi, 0, 0)),
                pl.BlockSpec((GDN_CONV, 3 * GROUP_W), lambda bi: (0, 0)),
                pl.BlockSpec((8, SMALL_W), lambda bi: (0, 0)),
                pl.BlockSpec((1, HEAD_DIM), lambda bi: (0, 0)),
                st_spec, st_spec]
    out_shape = [st_shape, st_shape]
    out_specs = [st_spec, st_spec]
    scratch = [pltpu.VMEM((l, 3 * GROUP_W), F32), pltpu.VMEM((2 * N_HEADS * HEAD_DIM, HEAD_DIM), F32)]
    if with_out:
        out_shape = [jax.ShapeDtypeStruct((b, l, GROUP_W), BF16)] + out_shape
        out_specs = [pl.BlockSpec((1, l, GROUP_W), lambda bi: (bi, 0, 0))] + out_specs
        scratch = [scratch[0], pltpu.VMEM((2, l, GROUP_W), F32), scratch[1]]
    res = pl.pallas_call(
        functools.partial(_gdn_kernel, seq=l, with_out=with_out),
        out_shape=tuple(out_shape), grid=(b,), in_specs=in_specs, out_specs=tuple(out_specs),
        scratch_shapes=scratch, compiler_params=_cparams(("parallel",)),
        name="gdn_out" if with_out else "gdn_state",
    )(big, big, small, cw, gp, nw.reshape(1, -1), s0f, s0b)
    return res if with_out else (None,) + tuple(res)


def _gdn_gate_params(a_log, dt_bias):
    gp = jnp.zeros((8, SMALL_W), F32)
    gp = gp.at[0, SM_A[0]:SM_A[0] + 2 * N_HEADS].set(-jnp.exp(a_log.astype(F32)).reshape(-1))
    gp = gp.at[1, SM_A[0]:SM_A[0] + 2 * N_HEADS].set(dt_bias.astype(F32).reshape(-1))
    return gp


def _rope(x, cos, sin, lane):
    swapped = jnp.where((lane % 64) < 32, pltpu.roll(x, 96, 1), pltpu.roll(x, 32, 1))
    return x * cos + swapped * sin


def _softmax_sink_pv(s, sink_col, vv):
    m = jnp.maximum(jnp.max(s, axis=-1, keepdims=True), sink_col)
    p = jnp.exp(s - m)
    den = jnp.sum(p, axis=-1, keepdims=True) + jnp.exp(sink_col - m)
    return _dot(p, vv) * (1.0 / den)


def _swa_kernel(*refs, seq, ctx_len, with_ctx):
    if with_ctx:
        (q_ref, k_ref, v_ref, gate_ref, kc_ref, vc_ref, cos_ref, sin_ref, sink_ref, qc_ref, gatec_ref,
         y_ref, yc_ref, kr_ref, bias_ref) = refs
    else:
        (q_ref, k_ref, v_ref, gate_ref, kc_ref, vc_ref, cos_ref, sin_ref, sink_ref,
         y_ref, kr_ref, bias_ref) = refs
    blk = SWA_BLOCK
    nb = seq // blk
    win = 3 * blk
    grp = N_HEADS // SWA_KV_HEADS
    scale = HEAD_DIM ** -0.5
    lane = _iota2((blk, HEAD_DIM), 1)

    def rope_k(n, _):
        rows = pl.ds(pl.multiple_of(n * blk, blk), blk)
        cos = cos_ref[rows, :]
        sin = sin_ref[rows, :]
        for g in range(SWA_KV_HEADS):
            hs = slice(g * HEAD_DIM, (g + 1) * HEAD_DIM)
            kr_ref[rows, hs] = _rope(k_ref[0, rows, hs].astype(F32), cos, sin, lane).astype(BF16)
        return _

    lax.fori_loop(0, nb, rope_k, None)

    rowq = _iota2((grp * blk, win + ctx_len), 0) % blk
    colk = _iota2((grp * blk, win + ctx_len), 1)
    is_ctx = colk >= win
    for w in range(3):
        visible = is_ctx | (jnp.abs(rowq + w * blk - colk) <= SWA_WINDOW)
        bias_ref[w] = jnp.where(visible, 0.0, NEG_BIG)

    def sink_column(g, n_rows):
        hr = _iota2((grp * n_rows, 1), 0) // n_rows
        col = jnp.zeros((grp * n_rows, 1), F32)
        for j in range(grp):
            col = jnp.where(hr == j, sink_ref[0:1, g * grp + j:g * grp + j + 1], col)
        return col

    def q_block(n, _):
        r0 = pl.multiple_of(n * blk, blk)
        rows = pl.ds(r0, blk)
        k0 = pl.multiple_of(jnp.clip(r0 - blk, 0, seq - win), blk)
        cos = cos_ref[rows, :]
        sin = sin_ref[rows, :]
        bias = bias_ref[(r0 - k0) // blk]
        for g in range(SWA_KV_HEADS):
            hs = slice(g * HEAD_DIM, (g + 1) * HEAD_DIM)
            qg = jnp.concatenate(
                [_rope(q_ref[0, rows, (g * grp + j) * HEAD_DIM:(g * grp + j + 1) * HEAD_DIM].astype(F32),
                       cos, sin, lane) * scale for j in range(grp)], axis=0).astype(BF16)
            kk = jnp.concatenate([kr_ref[pl.ds(k0, win), hs], kc_ref[0, :, hs]], axis=0)
            vv = jnp.concatenate([v_ref[0, pl.ds(k0, win), hs], vc_ref[0, :, hs]], axis=0)
            s = _dot_nt(qg, kk) + bias
            o = _softmax_sink_pv(s, sink_column(g, blk), vv)
            for j in range(grp):
                cs = slice((g * grp + j) * HEAD_DIM, (g * grp + j + 1) * HEAD_DIM)
                y_ref[0, rows, cs] = (o[j * blk:(j + 1) * blk, :]
                                      * _silu(gate_ref[0, rows, cs].astype(F32))).astype(BF16)
        return _

    lax.fori_loop(0, nb, q_block, None, unroll=2)

    if with_ctx:
        for g in range(SWA_KV_HEADS):
            hs = slice(g * HEAD_DIM, (g + 1) * HEAD_DIM)
            qg = jnp.concatenate([qc_ref[0, :, (g * grp + j) * HEAD_DIM:(g * grp + j + 1) * HEAD_DIM]
                                  for j in range(grp)], axis=0)
            s = _dot_nt(qg, kc_ref[0, :, hs]) * scale
            o = _softmax_sink_pv(s, sink_column(g, ctx_len), vc_ref[0, :, hs])
            for j in range(grp):
                cs = slice((g * grp + j) * HEAD_DIM, (g * grp + j + 1) * HEAD_DIM)
                yc_ref[0, :, cs] = (o[j * ctx_len:(j + 1) * ctx_len, :]
                                    * _silu(gatec_ref[0, :, cs].astype(F32))).astype(BF16)


def _swa(big, bigc, cos, sin, sink, *, with_ctx):
    b, l, _ = big.shape
    lc = bigc.shape[1]
    in_specs = [pl.BlockSpec((1, l, 512), lambda bi: (bi, 0, 11)),
                pl.BlockSpec((1, l, 256), lambda bi: (bi, 0, 24)),
                pl.BlockSpec((1, l, 256), lambda bi: (bi, 0, 25)),
                pl.BlockSpec((1, l, 512), lambda bi: (bi, 0, 13)),
                pl.BlockSpec((1, lc, 256), lambda bi: (bi, 0, 24)),
                pl.BlockSpec((1, lc, 256), lambda bi: (bi, 0, 25)),
                pl.BlockSpec((l, HEAD_DIM), lambda bi: (0, 0)),
                pl.BlockSpec((l, HEAD_DIM), lambda bi: (0, 0)),
                pl.BlockSpec((1, LANES), lambda bi: (0, 0))]
    args = [big, big, big, big, bigc, bigc, cos, sin, sink]
    out_shape = [jax.ShapeDtypeStruct((b, l, GROUP_W), BF16)]
    out_specs = [pl.BlockSpec((1, l, GROUP_W), lambda bi: (bi, 0, 0))]
    if with_ctx:
        in_specs += [pl.BlockSpec((1, lc, 512), lambda bi: (bi, 0, 11)),
                     pl.BlockSpec((1, lc, 512), lambda bi: (bi, 0, 13))]
        args += [bigc, bigc]
        out_shape.append(jax.ShapeDtypeStruct((b, lc, GROUP_W), BF16))
        out_specs.append(pl.BlockSpec((1, lc, GROUP_W), lambda bi: (bi, 0, 0)))
    res = pl.pallas_call(
        functools.partial(_swa_kernel, seq=l, ctx_len=lc, with_ctx=with_ctx),
        out_shape=tuple(out_shape), grid=(b,), in_specs=in_specs, out_specs=tuple(out_specs),
        scratch_shapes=[pltpu.VMEM((l, SWA_KV_HEADS * HEAD_DIM), BF16),
                        pltpu.VMEM((3, (N_HEADS // SWA_KV_HEADS) * SWA_BLOCK, 3 * SWA_BLOCK + lc), F32)],
        compiler_params=_cparams(("parallel",)),
        name="swa_ctx" if with_ctx else "swa",
    )(*args)
    return (res[0], res[1]) if with_ctx else (res[0], None)


def _rope_tables(seq_len):
    rows = seq_len // GRID_W
    row = jnp.repeat(jnp.arange(rows, dtype=F32), GRID_W)
    col = jnp.tile(jnp.arange(GRID_W, dtype=F32), rows)
    axis_dim = HEAD_DIM // 2
    inv_freq = ROPE_BASE ** (-jnp.arange(0, axis_dim, 2, dtype=F32) / axis_dim)
    ang_r = row[:, None] * inv_freq
    ang_c = col[:, None] * inv_freq
    cos = jnp.concatenate([jnp.cos(ang_r), jnp.cos(ang_r), jnp.cos(ang_c), jnp.cos(ang_c)], axis=-1)
    sin = jnp.concatenate([-jnp.sin(ang_r), jnp.sin(ang_r), -jnp.sin(ang_c), jnp.sin(ang_c)], axis=-1)
    return cos, sin


_PACK_RUNS = ((0, 0, 1024), (1024, 1056, 2048), (3072, 3120, 4096))
_LR_COL0 = 1024
_GDN_GATE_COL0 = 3104


def _pack_kernel(wt_ref, big_ref, small_ref):
    step = 512
    for dst, src, width in _PACK_RUNS:
        for o in range(0, width, step):
            big_ref[0, :, dst + o:dst + o + step] = wt_ref[0, src + o:src + o + step, :].T.astype(BF16)
    assert _LR_COL0 % LANES == 0 and _GDN_GATE_COL0 % LANES == SM_A[0]
    gate_win0 = _GDN_GATE_COL0 - SM_A[0]
    lane = _iota2((wt_ref.shape[2], LANES), 1)
    lr_win = wt_ref[0, _LR_COL0:_LR_COL0 + LANES, :].T
    gate_win = wt_ref[0, gate_win0:gate_win0 + LANES, :].T
    small = jnp.where(lane < SM_A[0], lr_win, jnp.where(lane < SM_B[1] + N_HEADS, gate_win, 0.0))
    small_ref[0] = small.astype(BF16)


def _pack_w_in(w_in):
    n_layers, d, n = w_in.shape
    kb = 256
    return pl.pallas_call(
        _pack_kernel,
        out_shape=(jax.ShapeDtypeStruct((n_layers, d, BIG_W), BF16),
                   jax.ShapeDtypeStruct((n_layers, d, SMALL_W), BF16)),
        grid=(n_layers, d // kb),
        in_specs=[pl.BlockSpec((1, n, kb), lambda l, i: (l, 0, i))],
        out_specs=(pl.BlockSpec((1, kb, BIG_W), lambda l, i: (l, i, 0)),
                   pl.BlockSpec((1, kb, SMALL_W), lambda l, i: (l, i, 0))),
        compiler_params=_cparams(("parallel", "parallel")),
        name="pack_w_in",
    )(jnp.swapaxes(w_in, 1, 2))


def kernel(x, c, ctx, c_ctx, ada_w, ada_b, norm_pre, norm_post, w_in, w_out, gla_w_decay, gla_b_decay,
           gla_norm, gdn_conv, gdn_a_log, gdn_dt_bias, gdn_norm, sc_conv, swa_sink):
    b, l, d = x.shape
    cos, sin = _rope_tables(l)
    cc = jnp.zeros((8, d), F32).at[:b].set(c).at[b].set(c_ctx)
    mod = _ada(cc, ada_w, ada_b)
    h, hc = x, ctx
    lc = ctx.shape[1]
    n_layers = ada_w.shape[0]
    w_big, w_small = _pack_w_in(w_in)
    w_o = w_out.astype(BF16)
    for layer in range(n_layers):
        with_ctx = layer < n_layers - 1
        shift, scale, gate = (mod[layer, :, i * d:(i + 1) * d] for i in range(3))

        def per_batch(t):
            return t[:b, None, :], t[b][None, None, :]
        (shift_l, shift_c), (scale_l, scale_c), (gate_l, gate_c) = per_batch(shift), per_batch(scale), per_batch(gate)
        g_pre = norm_pre[layer].reshape(1, d)
        big, small = _inproj(h, scale_l, shift_l, g_pre, w_big, w_small, layer, tm=min(l, 1024))
        bigc, smallc = _inproj(hc.reshape(1, b * lc, d), scale_c, shift_c, g_pre, w_big, w_small, layer,
                               tm=min(b * lc, 1024))
        bigc, smallc = bigc.reshape(b, lc, BIG_W), smallc.reshape(b, lc, SMALL_W)

        zf = jnp.zeros((b, HEAD_DIM, N_HEADS * GLA_DK), F32)
        yca, stf, stb = _gla(bigc, smallc, gla_w_decay[layer], gla_b_decay[layer], gla_norm[layer], zf, zf,
                             with_out=with_ctx)
        ya, _, _ = _gla(big, small, gla_w_decay[layer], gla_b_decay[layer], gla_norm[layer], stf, stb,
                        with_out=True)

        gp = _gdn_gate_params(gdn_a_log[layer], gdn_dt_bias[layer])
        zg = jnp.zeros((b, N_HEADS * HEAD_DIM, HEAD_DIM), F32)
        ycb, gtf, gtb = _gdn(bigc, smallc, gdn_conv[layer], gp, gdn_norm[layer], zg, zg, with_out=with_ctx)
        yb, _, _ = _gdn(big, small, gdn_conv[layer], gp, gdn_norm[layer], gtf, gtb, with_out=True)

        yc_ = _sconv(big, sc_conv[layer])
        sink = jnp.zeros((1, LANES), F32).at[0, :N_HEADS].set(swa_sink[layer])
        yd, ycd = _swa(big, bigc, cos, sin, sink, with_ctx=with_ctx)

        g_post = norm_post[layer].reshape(1, d)
        h_new = _outproj(ya, yb, yc_, yd, w_o, layer, h, gate_l, g_post, tm=min(l, 512))
        if with_ctx:
            ycc = _sconv(bigc, sc_conv[layer])
            flat = [t.reshape(1, b * lc, GROUP_W) for t in (yca, ycb, ycc, ycd)]
            hc = _outproj(*flat, w_o, layer, hc.reshape(1, b * lc, d), gate_c, g_post,
                          tm=min(b * lc, 512)).reshape(b, lc, d)
        h = h_new
    return h
```

```python
import functools

import jax
import jax.numpy as jnp
from jax import lax
from jax.experimental import pallas as pl
from jax.experimental.pallas import tpu as pltpu

F32 = jnp.float32
BF16 = jnp.bfloat16

D_MODEL = 2048
N_LAYERS = 2
GRID_W = 64
GROUP_W = 512
HEAD_DIM = 128
N_HEADS = 4
EPS = 1e-6
GLA_DK = 64
GLA_RANK = 16
GLA_TAU = 16.0
GDN_CONV = 5
SC_CONV = 3
SWA_KV_HEADS = 2
SWA_WINDOW = 128
ROPE_BASE = 10000.0

_SPLITS = (
    ("gla_q", 256), ("gla_k", 256), ("gla_v", 512), ("gla_lr_f", 16), ("gla_lr_b", 16), ("gla_gate", 512),
    ("gdn_q", 512), ("gdn_k", 512), ("gdn_v", 512),
    ("gdn_a_f", 4), ("gdn_a_b", 4), ("gdn_b_f", 4), ("gdn_b_b", 4), ("gdn_gate", 512),
    ("sc_b", 512), ("sc_c", 512), ("sc_h", 512), ("sc_gate", 512),
    ("swa_q", 512), ("swa_k", 256), ("swa_v", 256), ("swa_gate", 512),
)
_BIG_ORDER = ("gla_q", "gla_k", "gla_v", "gla_gate", "gdn_q", "gdn_k", "gdn_v", "gdn_gate",
              "sc_b", "sc_c", "sc_h", "sc_gate", "swa_q", "swa_k", "swa_v", "swa_gate")
_SMALL_ORDER = ("gla_lr_f", "gla_lr_b", "gdn_a_f", "gdn_a_b", "gdn_b_f", "gdn_b_b")
BIG_W = 7168
SMALL_W = 128
SM_LR = (0, 16)
SM_A = (32, 36)
SM_B = (40, 44)

LANES = 128
VMEM_LIMIT = 56 * 1024 * 1024

CHUNK = 128
INV_BASE = 2
SWA_BLOCK = 128

NEG_BIG = -1e30


def _cparams(sem):
    return pltpu.CompilerParams(dimension_semantics=sem, vmem_limit_bytes=VMEM_LIMIT)


def _dot(a, b):
    return lax.dot_general(a.astype(BF16), b.astype(BF16), (((1,), (0,)), ((), ())),
                           preferred_element_type=F32)


def _dot_nt(a, b):
    return lax.dot_general(a.astype(BF16), b.astype(BF16), (((1,), (1,)), ((), ())),
                           preferred_element_type=F32)


def _dot_tn(a, b):
    return lax.dot_general(a.astype(BF16), b.astype(BF16), (((0,), (0,)), ((), ())),
                           preferred_element_type=F32)


def _split(x):
    hi = x.astype(BF16)
    lo = (x - hi.astype(F32)).astype(BF16)
    return hi, lo


def _dot_exact_lhs(a_bf16, x):
    hi, lo = _split(x)
    return _dot(a_bf16, hi) + _dot(a_bf16, lo)


def _dot_exact_rhs(x, b_bf16):
    hi, lo = _split(x)
    return _dot(hi, b_bf16) + _dot(lo, b_bf16)


def _silu(x):
    return x * (1.0 / (1.0 + jnp.exp(-x)))


def _sigmoid(x):
    return 1.0 / (1.0 + jnp.exp(-x))


def _softplus(x):
    return jnp.maximum(x, 0.0) + jnp.log(1.0 + jnp.exp(-jnp.abs(x)))


def _log_sigmoid(x):
    return jnp.minimum(x, 0.0) - jnp.log(1.0 + jnp.exp(-jnp.abs(x)))


def _iota2(shape, dim):
    return lax.broadcasted_iota(jnp.int32, shape, dim)


ADA_K_SPLIT = 4


def _ada_kernel(c_ref, *refs):
    w_refs, (b_ref, o_ref) = refs[:ADA_K_SPLIT], refs[ADA_K_SPLIT:]
    a = _silu(c_ref[...])
    kq = a.shape[1] // ADA_K_SPLIT
    acc = b_ref[0]
    for q, w_ref in enumerate(w_refs):
        acc = acc + _dot(a[:, q * kq:(q + 1) * kq], w_ref[0])
    o_ref[0] = acc


def _ada(cc, ada_w, ada_b):
    n_layers, d, n = ada_w.shape
    tn = 1024
    kq = d // ADA_K_SPLIT
    w_specs = [pl.BlockSpec((1, kq, tn), lambda l, j, q=q: (l, q, j)) for q in range(ADA_K_SPLIT)]
    return pl.pallas_call(
        _ada_kernel,
        out_shape=jax.ShapeDtypeStruct((n_layers, 8, n), F32),
        grid=(n_layers, n // tn),
        in_specs=[pl.BlockSpec((8, d), lambda l, j: (0, 0))] + w_specs
        + [pl.BlockSpec((1, 1, tn), lambda l, j: (l, 0, j))],
        out_specs=pl.BlockSpec((1, 8, tn), lambda l, j: (l, 0, j)),
        compiler_params=_cparams(("parallel", "parallel")),
        name="ada_mod",
    )(cc, *([ada_w] * ADA_K_SPLIT), ada_b.reshape(n_layers, 1, n))


def _inproj_kernel(x_ref, sc_ref, sh_ref, g_ref, wb_ref, ws_ref, ob_ref, os_ref, xn_ref):
    @pl.when(pl.program_id(2) == 0)
    def _():
        rows_per_step = 128

        def norm_rows(i, _):
            rows = pl.ds(pl.multiple_of(i * rows_per_step, rows_per_step), rows_per_step)
            x = x_ref[0, rows, :]
            y = x * lax.rsqrt(jnp.mean(x * x, axis=-1, keepdims=True) + EPS) * g_ref[...]
            xn_ref[rows, :] = (y * (1.0 + sc_ref[0]) + sh_ref[0]).astype(BF16)
            return _

        lax.fori_loop(0, xn_ref.shape[0] // rows_per_step, norm_rows, None)
        os_ref[0] = jnp.dot(xn_ref[...], ws_ref[0], preferred_element_type=F32)

    ob_ref[0] = jnp.dot(xn_ref[...], wb_ref[0], preferred_element_type=F32).astype(BF16)


def _inproj(x, scale, shift, g, w_big, w_small, layer, *, tm, tn=1024):
    b, l, d = x.shape
    return pl.pallas_call(
        _inproj_kernel,
        out_shape=(jax.ShapeDtypeStruct((b, l, BIG_W), BF16), jax.ShapeDtypeStruct((b, l, SMALL_W), F32)),
        grid=(b, l // tm, BIG_W // tn),
        in_specs=[pl.BlockSpec((1, tm, d), lambda bi, i, j: (bi, i, 0)),
                  pl.BlockSpec((1, 1, d), lambda bi, i, j: (bi, 0, 0)),
                  pl.BlockSpec((1, 1, d), lambda bi, i, j: (bi, 0, 0)),
                  pl.BlockSpec((1, d), lambda bi, i, j: (0, 0)),
                  pl.BlockSpec((1, d, tn), lambda bi, i, j: (layer, 0, j)),
                  pl.BlockSpec((1, d, SMALL_W), lambda bi, i, j: (layer, 0, 0))],
        out_specs=(pl.BlockSpec((1, tm, tn), lambda bi, i, j: (bi, i, j)),
                   pl.BlockSpec((1, tm, SMALL_W), lambda bi, i, j: (bi, i, 0))),
        scratch_shapes=[pltpu.VMEM((tm, d), BF16)],
        compiler_params=_cparams(("parallel", "parallel", "arbitrary")),
        name="inproj",
    )(x, scale, shift, g, w_big, w_small)


def _group_dot(group, k, w_ref):
    return jnp.dot(group, w_ref[0, k * GROUP_W:(k + 1) * GROUP_W, :], preferred_element_type=F32)


def _outproj_tail(y, hl_ref, hr_ref, gate_ref, g_ref, o_ref):
    yn = gate_ref[0] * (y * lax.rsqrt(jnp.mean(y * y, axis=-1, keepdims=True) + EPS) * g_ref[...])
    half = hl_ref.shape[2]
    o_ref[0, :, 0:half] = hl_ref[0] + yn[:, 0:half]
    o_ref[0, :, half:2 * half] = hr_ref[0] + yn[:, half:2 * half]


def _outproj_kernel(ya_ref, yb_ref, yc_ref, yd_ref, w_ref, hl_ref, hr_ref, gate_ref, g_ref, o_ref):
    y = _group_dot(ya_ref[0], 0, w_ref)
    for k, ref in ((1, yb_ref), (2, yc_ref), (3, yd_ref)):
        y += _group_dot(ref[0], k, w_ref)
    _outproj_tail(y, hl_ref, hr_ref, gate_ref, g_ref, o_ref)


SC_HALO = 16


def _outproj_sconv_kernel(ya_ref, yb_ref, yd_ref, sb_ref, sc_ref, sh_ref, sg_ref, cp_ref, hp_ref, cn_ref, hn_ref,
                          cw_ref, w_ref, hl_ref, hr_ref, gate_ref, g_ref, o_ref):
    y = _group_dot(ya_ref[0], 0, w_ref) + _group_dot(yb_ref[0], 1, w_ref) + _group_dot(yd_ref[0], 3, w_ref)
    i = pl.program_id(1)
    tm = sc_ref.shape[1]
    p = sc_ref[0].astype(F32) * sh_ref[0].astype(F32)
    p_prev = jnp.where(i > 0, cp_ref[0].astype(F32) * hp_ref[0].astype(F32), 0.0)
    p_next = jnp.where(i < pl.num_programs(1) - 1, cn_ref[0].astype(F32) * hn_ref[0].astype(F32), 0.0)
    pe = jnp.concatenate([p_prev, p, p_next], axis=0)
    conv = (pe[SC_HALO - 1:SC_HALO - 1 + tm, :] * cw_ref[0:1, :] + p * cw_ref[1:2, :]
            + pe[SC_HALO + 1:SC_HALO + 1 + tm, :] * cw_ref[2:3, :])
    yc = (sb_ref[0].astype(F32) * conv * _silu(sg_ref[0].astype(F32))).astype(BF16)
    _outproj_tail(y + _group_dot(yc, 2, w_ref), hl_ref, hr_ref, gate_ref, g_ref, o_ref)


def _outproj_sconv(ya, yb, yd, big, cw, w_out, layer, h, gate, g, *, tm):
    b, l, d = h.shape
    assert SC_CONV == 3 and tm % SC_HALO == 0
    col0 = 3584 // GROUP_W
    per = tm // SC_HALO
    n_halo = l // SC_HALO
    yspec = pl.BlockSpec((1, tm, GROUP_W), lambda bi, i: (bi, i, 0))

    def col(k):
        return pl.BlockSpec((1, tm, GROUP_W), lambda bi, i, k=k: (bi, i, col0 + k))

    def prev(k):
        return pl.BlockSpec((1, SC_HALO, GROUP_W), lambda bi, i, k=k: (bi, jnp.maximum(i * per - 1, 0), col0 + k))

    def nxt(k):
        return pl.BlockSpec((1, SC_HALO, GROUP_W),
                            lambda bi, i, k=k: (bi, jnp.minimum((i + 1) * per, n_halo - 1), col0 + k))
    return pl.pallas_call(
        _outproj_sconv_kernel,
        out_shape=jax.ShapeDtypeStruct((b, l, d), F32),
        grid=(b, l // tm),
        in_specs=[yspec, yspec, yspec, col(0), col(1), col(2), col(3), prev(1), prev(2), nxt(1), nxt(2),
                  pl.BlockSpec((SC_CONV, GROUP_W), lambda bi, i: (0, 0)),
                  pl.BlockSpec((1, d, d), lambda bi, i: (layer, 0, 0)),
                  pl.BlockSpec((1, tm, d // 2), lambda bi, i: (bi, i, 0)),
                  pl.BlockSpec((1, tm, d // 2), lambda bi, i: (bi, i, 1)),
                  pl.BlockSpec((1, 1, d), lambda bi, i: (bi, 0, 0)),
                  pl.BlockSpec((1, d), lambda bi, i: (0, 0))],
        out_specs=pl.BlockSpec((1, tm, d), lambda bi, i: (bi, i, 0)),
        compiler_params=_cparams(("parallel", "parallel")),
        name="outproj_sconv",
    )(ya, yb, yd, big, big, big, big, big, big, big, big, cw, w_out, h, h, gate, g)


def _outproj(ya, yb, yc, yd, w_out, layer, h, gate, g, *, tm):
    b, l, d = h.shape
    yspec = pl.BlockSpec((1, tm, GROUP_W), lambda bi, i: (bi, i, 0))
    return pl.pallas_call(
        _outproj_kernel,
        out_shape=jax.ShapeDtypeStruct((b, l, d), F32),
        grid=(b, l // tm),
        in_specs=[yspec, yspec, yspec, yspec,
                  pl.BlockSpec((1, d, d), lambda bi, i: (layer, 0, 0)),
                  pl.BlockSpec((1, tm, d // 2), lambda bi, i: (bi, i, 0)),
                  pl.BlockSpec((1, tm, d // 2), lambda bi, i: (bi, i, 1)),
                  pl.BlockSpec((1, 1, d), lambda bi, i: (bi, 0, 0)),
                  pl.BlockSpec((1, d), lambda bi, i: (0, 0))],
        out_specs=pl.BlockSpec((1, tm, d), lambda bi, i: (bi, i, 0)),
        compiler_params=_cparams(("parallel", "parallel")),
        name="outproj",
    )(ya, yb, yc, yd, w_out, h, h, gate, g)


def _sconv_kernel(b_ref, c_ref, h_ref, gate_ref, w_ref, o_ref, p_ref, *, seq):
    zeros = jnp.zeros((8, GROUP_W), F32)
    p_ref[0:8, :] = zeros
    p_ref[seq + 8:seq + 16, :] = zeros
    p_ref[8:seq + 8, :] = c_ref[0].astype(F32) * h_ref[0].astype(F32)
    rows = min(seq, 256)
    w = w_ref[...]
    for t0 in range(0, seq, rows):
        conv = (p_ref[t0 + 7:t0 + 7 + rows, :] * w[0:1, :] + p_ref[t0 + 8:t0 + 8 + rows, :] * w[1:2, :]
                + p_ref[t0 + 9:t0 + 9 + rows, :] * w[2:3, :])
        y = b_ref[0, t0:t0 + rows, :].astype(F32) * conv * _silu(gate_ref[0, t0:t0 + rows, :].astype(F32))
        o_ref[0, t0:t0 + rows, :] = y.astype(BF16)


def _sconv(big, w):
    b, l, _ = big.shape
    col0 = 3584 // GROUP_W

    def spec(k):
        return pl.BlockSpec((1, l, GROUP_W), lambda bi, k=k: (bi, 0, col0 + k))
    return pl.pallas_call(
        functools.partial(_sconv_kernel, seq=l),
        out_shape=jax.ShapeDtypeStruct((b, l, GROUP_W), BF16),
        grid=(b,),
        in_specs=[spec(0), spec(1), spec(2), spec(3), pl.BlockSpec((SC_CONV, GROUP_W), lambda bi: (0, 0))],
        out_specs=pl.BlockSpec((1, l, GROUP_W), lambda bi: (bi, 0, 0)),
        scratch_shapes=[pltpu.VMEM((l + 16, GROUP_W), F32)],
        compiler_params=_cparams(("parallel",)),
        name="sconv",
    )(big, big, big, big, w)


def _gla_kernel(q_ref, k_ref, v_ref, gate_ref, sm_ref, wd_ref, bd_ref, nw_ref, s0f_ref, s0b_ref,
                *rest, seq, with_out):
    if with_out:
        y_ref, stf_ref, stb_ref, o_ref, st_ref, ko_ref, dec_ref, qp_ref, kp_ref, qi_ref = rest
    else:
        stf_ref, stb_ref, st_ref, ko_ref, dec_ref = rest
        y_ref = o_ref = qp_ref = kp_ref = qi_ref = None
    c = CHUNK
    n_chunks = seq // c
    ri = _iota2((c, c), 0)
    ci = _iota2((c, c), 1)
    incl = (ri >= ci, ri <= ci)
    tri = tuple(jnp.where(m, 1.0, 0.0).astype(BF16) for m in incl)
    qscale = GLA_DK ** -0.5
    last = (c - 1, 0)
    chains = [(d, h) for d in (0, 1) for h in range(N_HEADS)]

    def prep(n, _):
        rows = pl.ds(pl.multiple_of(n * c, c), c)
        sm = sm_ref[0, rows, :]
        kf = k_ref[0, rows, :].astype(F32)
        if with_out:
            qf = q_ref[0, rows, :].astype(F32) * qscale
        for d in (0, 1):
            z = _dot(sm[:, SM_LR[d]:SM_LR[d] + GLA_RANK], wd_ref[d]) + bd_ref[d]
            la = _log_sigmoid(z) * (1.0 / GLA_TAU)
            cum = _dot_exact_lhs(tri[d], la)
            tot = cum[last[d]:last[d] + 1, :]
            ko_ref[d, rows, :] = (kf * jnp.exp(tot - cum)).astype(BF16)
            dec_ref[d, pl.ds(pl.multiple_of(n * 8, 8), 8), :] = jnp.broadcast_to(jnp.exp(tot), (8, tot.shape[1]))
            if with_out:
                mid = cum[c // 2:c // 2 + 1, :]
                qp_ref[d, rows, :] = (qf * jnp.exp(cum - mid)).astype(BF16)
                kp_ref[d, rows, :] = (kf * jnp.exp(mid - cum)).astype(BF16)
                qi_ref[d, rows, :] = (qf * jnp.exp(cum)).astype(BF16)
        return _

    lax.fori_loop(0, n_chunks, prep, None, unroll=min(4, n_chunks))

    def step(i, _):
        chunk_of = (i, n_chunks - 1 - i)
        rows = [pl.ds(pl.multiple_of(n * c, c), c) for n in chunk_of]
        tots, kos, sts, vs, qps, kps, qis, stbs = [], [], [], [], [], [], [], []
        for d in (0, 1):
            tots.append(dec_ref[d, pl.ds(pl.multiple_of(chunk_of[d] * 8, 8), 1), :])
            kos.append(ko_ref[d, rows[d], :])
            sts.append(st_ref[d])
            vs.append(v_ref[0, rows[d], :])
            if with_out:
                qps.append(qp_ref[d, rows[d], :])
                kps.append(kp_ref[d, rows[d], :])
                qis.append(qi_ref[d, rows[d], :])
                stbs.append(sts[d].astype(BF16))

        def dk(h):
            return slice(h * GLA_DK, (h + 1) * GLA_DK)

        def dv(h):
            return slice(h * HEAD_DIM, (h + 1) * HEAD_DIM)

        if with_out:
            attn = [jnp.where(incl[d], _dot_nt(qps[d][:, dk(h)], kps[d][:, dk(h)]), 0.0) for d, h in chains]
            inter = [_dot_nt(qis[d][:, dk(h)], stbs[d][:, dk(h)]) for d, h in chains]
            intra = [_dot(a, vs[d][:, dv(h)]) for a, (d, h) in zip(attn, chains)]
            for (d, h), o1, o2 in zip(chains, inter, intra):
                o_ref[d, rows[d], dv(h)] = o1 + o2
        upd = [_dot_tn(vs[d][:, dv(h)], kos[d][:, dk(h)]) for d, h in chains]
        for d in (0, 1):
            st_ref[d] = sts[d] * tots[d] + jnp.concatenate(upd[d * N_HEADS:(d + 1) * N_HEADS], axis=1)
        return _

    st_ref[0] = s0f_ref[0]
    st_ref[1] = s0b_ref[0]
    lax.fori_loop(0, n_chunks, step, None, unroll=2)
    stf_ref[0] = st_ref[0]
    stb_ref[0] = st_ref[1]

    if with_out:
        def finish(n, _):
            rows = pl.ds(pl.multiple_of(n * c, c), c)
            for h in range(N_HEADS):
                hs = slice(h * HEAD_DIM, (h + 1) * HEAD_DIM)
                o = o_ref[0, rows, hs] + o_ref[1, rows, hs]
                o = o * lax.rsqrt(jnp.mean(o * o, axis=-1, keepdims=True) + EPS) * nw_ref[...]
                y_ref[0, rows, hs] = (o * _silu(gate_ref[0, rows, hs].astype(F32))).astype(BF16)
            return _

        lax.fori_loop(0, n_chunks, finish, None)


def _gla(big, small, wd, bd, nw, s0f, s0b, *, with_out):
    b, l, _ = big.shape
    st_shape = jax.ShapeDtypeStruct((b, HEAD_DIM, N_HEADS * GLA_DK), F32)
    st_spec = pl.BlockSpec((1, HEAD_DIM, N_HEADS * GLA_DK), lambda bi: (bi, 0, 0))
    in_specs = [pl.BlockSpec((1, l, 256), lambda bi: (bi, 0, 0)),
                pl.BlockSpec((1, l, 256), lambda bi: (bi, 0, 1)),
                pl.BlockSpec((1, l, 512), lambda bi: (bi, 0, 1)),
                pl.BlockSpec((1, l, 512), lambda bi: (bi, 0, 2)),
                pl.BlockSpec((1, l, SMALL_W), lambda bi: (bi, 0, 0)),
                pl.BlockSpec((2, GLA_RANK, 256), lambda bi: (0, 0, 0)),
                pl.BlockSpec((2, 1, 256), lambda bi: (0, 0, 0)),
                pl.BlockSpec((1, HEAD_DIM), lambda bi: (0, 0)),
                st_spec, st_spec]
    out_shape = [st_shape, st_shape]
    out_specs = [st_spec, st_spec]
    kw = N_HEADS * GLA_DK
    scratch = [pltpu.VMEM((2, HEAD_DIM, kw), F32),
               pltpu.VMEM((2, l, kw), BF16),
               pltpu.VMEM((2, 8 * (l // CHUNK), kw), F32)]
    if with_out:
        out_shape = [jax.ShapeDtypeStruct((b, l, GROUP_W), BF16)] + out_shape
        out_specs = [pl.BlockSpec((1, l, GROUP_W), lambda bi: (bi, 0, 0))] + out_specs
        scratch = [pltpu.VMEM((2, l, GROUP_W), F32)] + scratch + [pltpu.VMEM((2, l, kw), BF16)] * 3
    res = pl.pallas_call(
        functools.partial(_gla_kernel, seq=l, with_out=with_out),
        out_shape=tuple(out_shape), grid=(b,), in_specs=in_specs, out_specs=tuple(out_specs),
        scratch_shapes=scratch, compiler_params=_cparams(("parallel",)),
        name="gla_out" if with_out else "gla_state",
    )(big, big, big, big, small, wd, bd.reshape(2, 1, -1), nw.reshape(1, -1), s0f, s0b)
    return res if with_out else (None,) + tuple(res)


def _unit_triangular_inverses(lows, ri, ci):
    c = lows[0].shape[0]
    eye = jnp.where(ri == ci, 1.0, 0.0)
    base = (ri // INV_BASE) == (ci // INV_BASE)
    ds = [jnp.where(base, low, 0.0).astype(BF16) for low in lows]
    ts = [eye - d.astype(F32) for d in ds]
    power = 2
    if power < INV_BASE:
        ms = [_dot(d, d) for d in ds]
    while power < INV_BASE:
        mb = [m.astype(BF16) for m in ms]
        ts = [t + _dot(t, m16) for t, m16 in zip(ts, mb)]
        power *= 2
        if power < INV_BASE:
            ms = [_dot(m16, m16) for m16 in mb]
    s = INV_BASE
    while s < c:
        sel = ((ri // (2 * s)) == (ci // (2 * s))) & ((ri // s) != (ci // s))
        tb = [t.astype(BF16) for t in ts]
        ps = [_dot(t, jnp.where(sel, low, 0.0)) for t, low in zip(tb, lows)]
        ts = [t - _dot(p, t16) for t, p, t16 in zip(ts, ps, tb)]
        s *= 2
    return ts


def _gdn_kernel(qkv_ref, gate_ref, sm_ref, cw_ref, gp_ref, nw_ref, s0f_ref, s0b_ref,
                *rest, seq, with_out):
    if with_out:
        y_ref, stf_ref, stb_ref, x_ref, o_ref, st_ref, cumc_ref, cumr_ref = rest
    else:
        stf_ref, stb_ref, x_ref, st_ref, cumc_ref, cumr_ref = rest
        y_ref = o_ref = None
    c = CHUNK
    assert SMALL_W == c
    n_chunks = seq // c
    width = 3 * GROUP_W
    ri = _iota2((c, c), 0)
    ci = _iota2((c, c), 1)
    lower = ri >= ci
    upper = ri <= ci
    tril = jnp.where(lower, 1.0, 0.0).astype(BF16)
    triu = jnp.where(upper, 1.0, 0.0).astype(BF16)

    halo = 16
    off_taps = tuple(j for j in range(GDN_CONV) if j != GDN_CONV // 2)
    sr = _iota2((len(off_taps) * c, c + 2 * halo), 0)
    sc_ = _iota2((len(off_taps) * c, c + 2 * halo), 1)
    tap = jnp.zeros_like(sr)
    for idx, j in enumerate(off_taps):
        tap = jnp.where(sr // c == idx, j, tap)
    shift_mat = jnp.where(sc_ == (sr % c) + halo - GDN_CONV // 2 + tap, 1.0, 0.0).astype(BF16)

    def conv_chunk(n, _):
        r0 = pl.multiple_of(n * c, c)
        prev0 = pl.multiple_of(jnp.maximum(r0 - 16, 0), 16)
        next0 = pl.multiple_of(jnp.minimum(r0 + c, seq - 16), 16)
        prev = jnp.where(n > 0, qkv_ref[0, pl.ds(prev0, 16), :].astype(F32), 0.0).astype(BF16)
        nxt = jnp.where(n < n_chunks - 1, qkv_ref[0, pl.ds(next0, 16), :].astype(F32), 0.0).astype(BF16)
        cur = qkv_ref[0, pl.ds(r0, c), :]
        xe = jnp.concatenate([prev, cur, nxt], axis=0)
        shifted = _dot(shift_mat, xe)
        acc = cur.astype(F32) * cw_ref[GDN_CONV // 2:GDN_CONV // 2 + 1, :]
        for idx, j in enumerate(off_taps):
            acc = acc + shifted[idx * c:(idx + 1) * c, :] * cw_ref[j:j + 1, :]
        acc = _silu(acc)
        for h in range(2 * N_HEADS):
            hs = slice(h * HEAD_DIM, (h + 1) * HEAD_DIM)
            t = acc[:, hs]
            t = t * lax.rsqrt(jnp.sum(t * t, axis=-1, keepdims=True) + EPS)
            if h < N_HEADS:
                t = t * (HEAD_DIM ** -0.5)
            x_ref[pl.ds(r0, c), hs] = t
        x_ref[pl.ds(r0, c), 2 * GROUP_W:width] = acc[:, 2 * GROUP_W:width]
        g = gp_ref[0:1, :] * _softplus(sm_ref[0, pl.ds(r0, c), :] + gp_ref[1:2, :])
        gt = g.T
        for d in (0, 1):
            cumc_ref[d, pl.ds(r0, c), :] = _dot_exact_lhs(tril if d == 0 else triu, g)
            cumr_ref[d, pl.ds(r0, c), :] = _dot_exact_rhs(gt, triu if d == 0 else tril)
        return _

    lax.fori_loop(0, n_chunks, conv_chunk, None)

    chains = [(d, h) for d in (0, 1) for h in range(N_HEADS)]
    n_state = N_HEADS * HEAD_DIM

    def step(i, _):
        rows = [pl.ds(pl.multiple_of(n * c, c), c) for n in (i, n_chunks - 1 - i)]
        cums = [cumc_ref[d, rows[d], :] for d in (0, 1)]
        cum_rows = [cumr_ref[d, rows[d], :] for d in (0, 1)]
        betas = [_sigmoid(sm_ref[0, rows[d], :]) for d in (0, 1)]
        incl = (lower, upper)
        strict = (ri > ci, ri < ci)
        last = (c - 1, 0)

        kfs, vbs, kbs, e_cs, tots, decays, kos, qfs = [], [], [], [], [], [], [], []
        for d, h in chains:
            la, lb = SM_A[d] + h, SM_B[d] + h
            cum_c = cums[d][:, la:la + 1]
            cum_r = cum_rows[d][la:la + 1, :]
            tot = cums[d][last[d]:last[d] + 1, la:la + 1]
            beta_c = betas[d][:, lb:lb + 1]
            kf = x_ref[rows[d], GROUP_W + h * HEAD_DIM:GROUP_W + (h + 1) * HEAD_DIM]
            vf = x_ref[rows[d], 2 * GROUP_W + h * HEAD_DIM:2 * GROUP_W + (h + 1) * HEAD_DIM]
            e_c = jnp.exp(cum_c)
            kfs.append(kf)
            kbs.append(kf * beta_c)
            vbs.append(vf * beta_c)
            e_cs.append(e_c)
            tots.append(tot)
            decays.append(jnp.where(incl[d], jnp.exp(jnp.minimum(cum_c - cum_r, 0.0)), 0.0))
            kos.append(kf * jnp.exp(tot - cum_c))
            if with_out:
                qfs.append(x_ref[rows[d], h * HEAD_DIM:(h + 1) * HEAD_DIM])

        if with_out:
            prods = [_dot_nt(jnp.concatenate([kb, qf], axis=0), kf) for kb, qf, kf in zip(kbs, qfs, kfs)]
            attns = [p[c:, :] * dec for p, dec in zip(prods, decays)]
        else:
            prods = [_dot_nt(kb, kf) for kb, kf in zip(kbs, kfs)]
        lows = [jnp.where(strict[d], p[:c, :] * dec, 0.0) for (d, _), p, dec in zip(chains, prods, decays)]
        ts = _unit_triangular_inverses(lows, ri, ci)
        uws = [_dot(t, jnp.concatenate([vb, kb * e_c], axis=1)) for t, vb, kb, e_c in zip(ts, vbs, kbs, e_cs)]
        sts = [st_ref[j * HEAD_DIM:(j + 1) * HEAD_DIM, :] for j in range(len(chains))]
        if with_out:
            reads = [_dot(jnp.concatenate([uw[:, HEAD_DIM:], qf * e_c], axis=0), st)
                     for uw, qf, e_c, st in zip(uws, qfs, e_cs, sts)]
        else:
            reads = [_dot(uw[:, HEAD_DIM:], st) for uw, st in zip(uws, sts)]
        v_news = [uw[:, :HEAD_DIM] - r[:c, :] for uw, r in zip(uws, reads)]
        if with_out:
            writes = [_dot(jnp.concatenate([a, ko.T], axis=0), vn) for a, ko, vn in zip(attns, kos, v_news)]
        else:
            writes = [_dot_tn(ko, vn) for ko, vn in zip(kos, v_news)]
        for j, (d, h) in enumerate(chains):
            if with_out:
                o_ref[d, rows[d], h * HEAD_DIM:(h + 1) * HEAD_DIM] = reads[j][c:, :] + writes[j][:c, :]
                upd = writes[j][c:, :]
            else:
                upd = writes[j]
            st_ref[j * HEAD_DIM:(j + 1) * HEAD_DIM, :] = sts[j] * jnp.exp(tots[j]) + upd
        return _

    st_ref[0:n_state, :] = s0f_ref[0]
    st_ref[n_state:2 * n_state, :] = s0b_ref[0]
    lax.fori_loop(0, n_chunks, step, None, unroll=2)
    stf_ref[0] = st_ref[0:n_state, :]
    stb_ref[0] = st_ref[n_state:2 * n_state, :]

    if with_out:
        def finish(n, _):
            rows = pl.ds(pl.multiple_of(n * c, c), c)
            for h in range(N_HEADS):
                hs = slice(h * HEAD_DIM, (h + 1) * HEAD_DIM)
                o = o_ref[0, rows, hs] + o_ref[1, rows, hs]
                o = o * lax.rsqrt(jnp.mean(o * o, axis=-1, keepdims=True) + EPS) * nw_ref[...]
                y_ref[0, rows, hs] = (o * _silu(gate_ref[0, rows, hs].astype(F32))).astype(BF16)
            return _

        lax.fori_loop(0, n_chunks, finish, None)


def _gdn(big, small, cw, gp, nw, s0f, s0b, *, with_out):
    b, l, _ = big.shape
    st_shape = jax.ShapeDtypeStruct((b, N_HEADS * HEAD_DIM, HEAD_DIM), F32)
    st_spec = pl.BlockSpec((1, N_HEADS * HEAD_DIM, HEAD_DIM), lambda bi: (bi, 0, 0))
    in_specs = [pl.BlockSpec((1, l, 3 * GROUP_W), lambda bi: (bi, 0, 1)),
                pl.BlockSpec((1, l, GROUP_W), lambda bi: (bi, 0, 6)),
                pl.BlockSpec((1, l, SMALL_W), lambda bi: (bi, 0, 0)),
                pl.BlockSpec((GDN_CONV, 3 * GROUP_W), lambda bi: (0, 0)),
                pl.BlockSpec((8, SMALL_W), lambda bi: (0, 0)),
                pl.BlockSpec((1, HEAD_DIM), lambda bi: (0, 0)),
                st_spec, st_spec]
    out_shape = [st_shape, st_shape]
    out_specs = [st_spec, st_spec]
    scratch = [pltpu.VMEM((l, 3 * GROUP_W), F32), pltpu.VMEM((2 * N_HEADS * HEAD_DIM, HEAD_DIM), F32)]
    if with_out:
        out_shape = [jax.ShapeDtypeStruct((b, l, GROUP_W), BF16)] + out_shape
        out_specs = [pl.BlockSpec((1, l, GROUP_W), lambda bi: (bi, 0, 0))] + out_specs
        scratch = [scratch[0], pltpu.VMEM((2, l, GROUP_W), F32), scratch[1]]
    scratch += [pltpu.VMEM((2, l, SMALL_W), F32), pltpu.VMEM((2, l, CHUNK), F32)]
    res = pl.pallas_call(
        functools.partial(_gdn_kernel, seq=l, with_out=with_out),
        out_shape=tuple(out_shape), grid=(b,), in_specs=in_specs, out_specs=tuple(out_specs),
        scratch_shapes=scratch, compiler_params=_cparams(("parallel",)),
        name="gdn_out" if with_out else "gdn_state",
    )(big, big, small, cw, gp, nw.reshape(1, -1), s0f, s0b)
    return res if with_out else (None,) + tuple(res)


def _gdn_gate_params(a_log, dt_bias):
    gp = jnp.zeros((8, SMALL_W), F32)
    gp = gp.at[0, SM_A[0]:SM_A[0] + 2 * N_HEADS].set(-jnp.exp(a_log.astype(F32)).reshape(-1))
    gp = gp.at[1, SM_A[0]:SM_A[0] + 2 * N_HEADS].set(dt_bias.astype(F32).reshape(-1))
    return gp


def _rope(x, cos, sin, lane):
    swapped = jnp.where((lane % 64) < 32, pltpu.roll(x, 96, 1), pltpu.roll(x, 32, 1))
    return x * cos + swapped * sin


def _softmax_sink_pv(s, sink_col, vv):
    m = jnp.maximum(jnp.max(s, axis=-1, keepdims=True), sink_col)
    p = jnp.exp(s - m)
    den = jnp.sum(p, axis=-1, keepdims=True) + jnp.exp(sink_col - m)
    return _dot(p, vv) * (1.0 / den)


def _swa_kernel(*refs, seq, ctx_len, with_ctx):
    if with_ctx:
        (q_ref, k_ref, v_ref, gate_ref, kc_ref, vc_ref, cos_ref, sin_ref, sink_ref, qc_ref, gatec_ref,
         y_ref, yc_ref, kr_ref, bias_ref) = refs
    else:
        (q_ref, k_ref, v_ref, gate_ref, kc_ref, vc_ref, cos_ref, sin_ref, sink_ref,
         y_ref, kr_ref, bias_ref) = refs
    blk = SWA_BLOCK
    nb = seq // blk
    win = 3 * blk
    grp = N_HEADS // SWA_KV_HEADS
    scale = HEAD_DIM ** -0.5
    lane = _iota2((blk, HEAD_DIM), 1)

    def rope_k(n, _):
        rows = pl.ds(pl.multiple_of(n * blk, blk), blk)
        cos = cos_ref[rows, :]
        sin = sin_ref[rows, :]
        for g in range(SWA_KV_HEADS):
            hs = slice(g * HEAD_DIM, (g + 1) * HEAD_DIM)
            kr_ref[rows, hs] = _rope(k_ref[0, rows, hs].astype(F32), cos, sin, lane).astype(BF16)
        return _

    lax.fori_loop(0, nb, rope_k, None)

    rowq = _iota2((grp * blk, win + ctx_len), 0) % blk
    colk = _iota2((grp * blk, win + ctx_len), 1)
    is_ctx = colk >= win
    for w in range(3):
        visible = is_ctx | (jnp.abs(rowq + w * blk - colk) <= SWA_WINDOW)
        bias_ref[w] = jnp.where(visible, 0.0, NEG_BIG)

    def sink_column(g, n_rows):
        hr = _iota2((grp * n_rows, 1), 0) // n_rows
        col = jnp.zeros((grp * n_rows, 1), F32)
        for j in range(grp):
            col = jnp.where(hr == j, sink_ref[0:1, g * grp + j:g * grp + j + 1], col)
        return col

    def q_block(n, _):
        r0 = pl.multiple_of(n * blk, blk)
        rows = pl.ds(r0, blk)
        k0 = pl.multiple_of(jnp.clip(r0 - blk, 0, seq - win), blk)
        cos = cos_ref[rows, :]
        sin = sin_ref[rows, :]
        bias = bias_ref[(r0 - k0) // blk]
        for g in range(SWA_KV_HEADS):
            hs = slice(g * HEAD_DIM, (g + 1) * HEAD_DIM)
            qg = jnp.concatenate(
                [_rope(q_ref[0, rows, (g * grp + j) * HEAD_DIM:(g * grp + j + 1) * HEAD_DIM].astype(F32),
                       cos, sin, lane) * scale for j in range(grp)], axis=0).astype(BF16)
            kk = jnp.concatenate([kr_ref[pl.ds(k0, win), hs], kc_ref[0, :, hs]], axis=0)
            vv = jnp.concatenate([v_ref[0, pl.ds(k0, win), hs], vc_ref[0, :, hs]], axis=0)
            s = _dot_nt(qg, kk) + bias
            o = _softmax_sink_pv(s, sink_column(g, blk), vv)
            for j in range(grp):
                cs = slice((g * grp + j) * HEAD_DIM, (g * grp + j + 1) * HEAD_DIM)
                y_ref[0, rows, cs] = (o[j * blk:(j + 1) * blk, :]
                                      * _silu(gate_ref[0, rows, cs].astype(F32))).astype(BF16)
        return _

    lax.fori_loop(0, nb, q_block, None, unroll=2)

    if with_ctx:
        for g in range(SWA_KV_HEADS):
            hs = slice(g * HEAD_DIM, (g + 1) * HEAD_DIM)
            qg = jnp.concatenate([qc_ref[0, :, (g * grp + j) * HEAD_DIM:(g * grp + j + 1) * HEAD_DIM]
                                  for j in range(grp)], axis=0)
            s = _dot_nt(qg, kc_ref[0, :, hs]) * scale
            o = _softmax_sink_pv(s, sink_column(g, ctx_len), vc_ref[0, :, hs])
            for j in range(grp):
                cs = slice((g * grp + j) * HEAD_DIM, (g * grp + j + 1) * HEAD_DIM)
                yc_ref[0, :, cs] = (o[j * ctx_len:(j + 1) * ctx_len, :]
                                    * _silu(gatec_ref[0, :, cs].astype(F32))).astype(BF16)


def _swa(big, bigc, cos, sin, sink, *, with_ctx):
    b, l, _ = big.shape
    lc = bigc.shape[1]
    in_specs = [pl.BlockSpec((1, l, 512), lambda bi: (bi, 0, 11)),
                pl.BlockSpec((1, l, 256), lambda bi: (bi, 0, 24)),
                pl.BlockSpec((1, l, 256), lambda bi: (bi, 0, 25)),
                pl.BlockSpec((1, l, 512), lambda bi: (bi, 0, 13)),
                pl.BlockSpec((1, lc, 256), lambda bi: (bi, 0, 24)),
                pl.BlockSpec((1, lc, 256), lambda bi: (bi, 0, 25)),
                pl.BlockSpec((l, HEAD_DIM), lambda bi: (0, 0)),
                pl.BlockSpec((l, HEAD_DIM), lambda bi: (0, 0)),
                pl.BlockSpec((1, LANES), lambda bi: (0, 0))]
    args = [big, big, big, big, bigc, bigc, cos, sin, sink]
    out_shape = [jax.ShapeDtypeStruct((b, l, GROUP_W), BF16)]
    out_specs = [pl.BlockSpec((1, l, GROUP_W), lambda bi: (bi, 0, 0))]
    if with_ctx:
        in_specs += [pl.BlockSpec((1, lc, 512), lambda bi: (bi, 0, 11)),
                     pl.BlockSpec((1, lc, 512), lambda bi: (bi, 0, 13))]
        args += [bigc, bigc]
        out_shape.append(jax.ShapeDtypeStruct((b, lc, GROUP_W), BF16))
        out_specs.append(pl.BlockSpec((1, lc, GROUP_W), lambda bi: (bi, 0, 0)))
    res = pl.pallas_call(
        functools.partial(_swa_kernel, seq=l, ctx_len=lc, with_ctx=with_ctx),
        out_shape=tuple(out_shape), grid=(b,), in_specs=in_specs, out_specs=tuple(out_specs),
        scratch_shapes=[pltpu.VMEM((l, SWA_KV_HEADS * HEAD_DIM), BF16),
                        pltpu.VMEM((3, (N_HEADS // SWA_KV_HEADS) * SWA_BLOCK, 3 * SWA_BLOCK + lc), F32)],
        compiler_params=_cparams(("parallel",)),
        name="swa_ctx" if with_ctx else "swa",
    )(*args)
    return (res[0], res[1]) if with_ctx else (res[0], None)


def _rope_tables(seq_len):
    rows = seq_len // GRID_W
    row = jnp.repeat(jnp.arange(rows, dtype=F32), GRID_W)
    col = jnp.tile(jnp.arange(GRID_W, dtype=F32), rows)
    axis_dim = HEAD_DIM // 2
    inv_freq = ROPE_BASE ** (-jnp.arange(0, axis_dim, 2, dtype=F32) / axis_dim)
    ang_r = row[:, None] * inv_freq
    ang_c = col[:, None] * inv_freq
    cos = jnp.concatenate([jnp.cos(ang_r), jnp.cos(ang_r), jnp.cos(ang_c), jnp.cos(ang_c)], axis=-1)
    sin = jnp.concatenate([-jnp.sin(ang_r), jnp.sin(ang_r), -jnp.sin(ang_c), jnp.sin(ang_c)], axis=-1)
    return cos, sin


_PACK_RUNS = ((0, 0, 1024), (1024, 1056, 2048), (3072, 3120, 4096))
_LR_COL0 = 1024
_GDN_GATE_COL0 = 3104


def _pack_kernel(wt_ref, big_ref, small_ref):
    step = 512
    for dst, src, width in _PACK_RUNS:
        for o in range(0, width, step):
            big_ref[0, :, dst + o:dst + o + step] = wt_ref[0, src + o:src + o + step, :].T.astype(BF16)
    assert _LR_COL0 % LANES == 0 and _GDN_GATE_COL0 % LANES == SM_A[0]
    gate_win0 = _GDN_GATE_COL0 - SM_A[0]
    lane = _iota2((wt_ref.shape[2], LANES), 1)
    lr_win = wt_ref[0, _LR_COL0:_LR_COL0 + LANES, :].T
    gate_win = wt_ref[0, gate_win0:gate_win0 + LANES, :].T
    small = jnp.where(lane < SM_A[0], lr_win, jnp.where(lane < SM_B[1] + N_HEADS, gate_win, 0.0))
    small_ref[0] = small.astype(BF16)


def _pack_w_in(w_in):
    n_layers, d, n = w_in.shape
    kb = 256
    return pl.pallas_call(
        _pack_kernel,
        out_shape=(jax.ShapeDtypeStruct((n_layers, d, BIG_W), BF16),
                   jax.ShapeDtypeStruct((n_layers, d, SMALL_W), BF16)),
        grid=(n_layers, d // kb),
        in_specs=[pl.BlockSpec((1, n, kb), lambda l, i: (l, 0, i))],
        out_specs=(pl.BlockSpec((1, kb, BIG_W), lambda l, i: (l, i, 0)),
                   pl.BlockSpec((1, kb, SMALL_W), lambda l, i: (l, i, 0))),
        compiler_params=_cparams(("parallel", "parallel")),
        name="pack_w_in",
    )(jnp.swapaxes(w_in, 1, 2))


def kernel(x, c, ctx, c_ctx, ada_w, ada_b, norm_pre, norm_post, w_in, w_out, gla_w_decay, gla_b_decay,
           gla_norm, gdn_conv, gdn_a_log, gdn_dt_bias, gdn_norm, sc_conv, swa_sink):
    b, l, d = x.shape
    cos, sin = _rope_tables(l)
    cc = jnp.zeros((8, d), F32).at[:b].set(c).at[b].set(c_ctx)
    mod = _ada(cc, ada_w, ada_b)
    h, hc = x, ctx
    lc = ctx.shape[1]
    n_layers = ada_w.shape[0]
    w_big, w_small = _pack_w_in(w_in)
    w_o = w_out.astype(BF16)
    for layer in range(n_layers):
        with_ctx = layer < n_layers - 1
        shift, scale, gate = (mod[layer, :, i * d:(i + 1) * d] for i in range(3))

        def per_batch(t):
            return t[:b, None, :], t[b][None, None, :]
        (shift_l, shift_c), (scale_l, scale_c), (gate_l, gate_c) = per_batch(shift), per_batch(scale), per_batch(gate)
        g_pre = norm_pre[layer].reshape(1, d)
        big, small = _inproj(h, scale_l, shift_l, g_pre, w_big, w_small, layer, tm=min(l, 1024))
        bigc, smallc = _inproj(hc.reshape(1, b * lc, d), scale_c, shift_c, g_pre, w_big, w_small, layer,
                               tm=min(b * lc, 1024))
        bigc, smallc = bigc.reshape(b, lc, BIG_W), smallc.reshape(b, lc, SMALL_W)

        zf = jnp.zeros((b, HEAD_DIM, N_HEADS * GLA_DK), F32)
        yca, stf, stb = _gla(bigc, smallc, gla_w_decay[layer], gla_b_decay[layer], gla_norm[layer], zf, zf,
                             with_out=with_ctx)
        ya, _, _ = _gla(big, small, gla_w_decay[layer], gla_b_decay[layer], gla_norm[layer], stf, stb,
                        with_out=True)

        gp = _gdn_gate_params(gdn_a_log[layer], gdn_dt_bias[layer])
        zg = jnp.zeros((b, N_HEADS * HEAD_DIM, HEAD_DIM), F32)
        ycb, gtf, gtb = _gdn(bigc, smallc, gdn_conv[layer], gp, gdn_norm[layer], zg, zg, with_out=with_ctx)
        yb, _, _ = _gdn(big, small, gdn_conv[layer], gp, gdn_norm[layer], gtf, gtb, with_out=True)

        sink = jnp.zeros((1, LANES), F32).at[0, :N_HEADS].set(swa_sink[layer])
        yd, ycd = _swa(big, bigc, cos, sin, sink, with_ctx=with_ctx)

        g_post = norm_post[layer].reshape(1, d)
        h_new = _outproj_sconv(ya, yb, yd, big, sc_conv[layer], w_o, layer, h, gate_l, g_post, tm=min(l, 512))
        if with_ctx:
            ycc = _sconv(bigc, sc_conv[layer])
            flat = [t.reshape(1, b * lc, GROUP_W) for t in (yca, ycb, ycc, ycd)]
            hc = _outproj(*flat, w_o, layer, hc.reshape(1, b * lc, d), gate_c, g_post,
                          tm=min(b * lc, 512)).reshape(b, lc, d)
        h = h_new
    return h
```

```python
import functools

import jax
import jax.numpy as jnp
from jax import lax
from jax.experimental import pallas as pl
from jax.experimental.pallas import tpu as pltpu

F32 = jnp.float32
BF16 = jnp.bfloat16

D_MODEL = 2048
N_LAYERS = 2
GRID_W = 64
GROUP_W = 512
HEAD_DIM = 128
N_HEADS = 4
EPS = 1e-6
GLA_DK = 64
GLA_RANK = 16
GLA_TAU = 16.0
GDN_CONV = 5
SC_CONV = 3
SWA_KV_HEADS = 2
SWA_WINDOW = 128
ROPE_BASE = 10000.0

_SPLITS = (
    ("gla_q", 256), ("gla_k", 256), ("gla_v", 512), ("gla_lr_f", 16), ("gla_lr_b", 16), ("gla_gate", 512),
    ("gdn_q", 512), ("gdn_k", 512), ("gdn_v", 512),
    ("gdn_a_f", 4), ("gdn_a_b", 4), ("gdn_b_f", 4), ("gdn_b_b", 4), ("gdn_gate", 512),
    ("sc_b", 512), ("sc_c", 512), ("sc_h", 512), ("sc_gate", 512),
    ("swa_q", 512), ("swa_k", 256), ("swa_v", 256), ("swa_gate", 512),
)
BIG_W = 7168
SMALL_W = 128
SM_LR = (0, 16)
SM_A = (32, 36)
SM_B = (40, 44)

LANES = 128
VMEM_LIMIT = 56 * 1024 * 1024

CHUNK = 128
SWA_BLOCK = 128
INPROJ_TM = 1024
INPROJ_TN = 1024
OUTPROJ_TM = 512
ADA_TN = 1024
PACK_KB = 256

NEG_BIG = -1e30


def _cparams(sem):
    return pltpu.CompilerParams(dimension_semantics=sem, vmem_limit_bytes=VMEM_LIMIT)


def _dot(a, b):
    return lax.dot_general(a.astype(BF16), b.astype(BF16), (((1,), (0,)), ((), ())),
                           preferred_element_type=F32)


def _dot_nt(a, b):
    return lax.dot_general(a.astype(BF16), b.astype(BF16), (((1,), (1,)), ((), ())),
                           preferred_element_type=F32)


def _dot_tn(a, b):
    return lax.dot_general(a.astype(BF16), b.astype(BF16), (((0,), (0,)), ((), ())),
                           preferred_element_type=F32)


def _split(x):
    hi = x.astype(BF16)
    lo = (x - hi.astype(F32)).astype(BF16)
    return hi, lo


def _dot_exact_lhs(a_bf16, x):
    hi, lo = _split(x)
    return _dot(a_bf16, hi) + _dot(a_bf16, lo)


def _dot_exact_rhs(x, b_bf16):
    hi, lo = _split(x)
    return _dot(hi, b_bf16) + _dot(lo, b_bf16)


def _silu(x):
    return x * (1.0 / (1.0 + jnp.exp(-x)))


def _sigmoid(x):
    return 1.0 / (1.0 + jnp.exp(-x))


def _softplus(x):
    return jnp.maximum(x, 0.0) + jnp.log(1.0 + jnp.exp(-jnp.abs(x)))


def _log_sigmoid(x):
    return jnp.minimum(x, 0.0) - jnp.log(1.0 + jnp.exp(-jnp.abs(x)))


def _iota2(shape, dim):
    return lax.broadcasted_iota(jnp.int32, shape, dim)


ADA_K_SPLIT = 4


def _ada_kernel(c_ref, *refs):
    w_refs, (b_ref, o_ref) = refs[:ADA_K_SPLIT], refs[ADA_K_SPLIT:]
    a = _silu(c_ref[...])
    kq = a.shape[1] // ADA_K_SPLIT
    acc = b_ref[0]
    for q, w_ref in enumerate(w_refs):
        acc = acc + _dot(a[:, q * kq:(q + 1) * kq], w_ref[0])
    o_ref[0] = acc


def _ada(cc, ada_w, ada_b):
    n_layers, d, n = ada_w.shape
    tn = ADA_TN
    kq = d // ADA_K_SPLIT
    w_specs = [pl.BlockSpec((1, kq, tn), lambda l, j, q=q: (l, q, j)) for q in range(ADA_K_SPLIT)]
    return pl.pallas_call(
        _ada_kernel,
        out_shape=jax.ShapeDtypeStruct((n_layers, 8, n), F32),
        grid=(n_layers, n // tn),
        in_specs=[pl.BlockSpec((8, d), lambda l, j: (0, 0))] + w_specs
        + [pl.BlockSpec((1, 1, tn), lambda l, j: (l, 0, j))],
        out_specs=pl.BlockSpec((1, 8, tn), lambda l, j: (l, 0, j)),
        compiler_params=_cparams(("parallel", "parallel")),
        name="ada_mod",
    )(cc, *([ada_w] * ADA_K_SPLIT), ada_b.reshape(n_layers, 1, n))


def _inproj_kernel(x_ref, sc_ref, sh_ref, g_ref, wb_ref, ws_ref, ob_ref, os_ref, xn_ref):
    @pl.when(pl.program_id(2) == 0)
    def _():
        rows_per_step = 128

        def norm_rows(i, _):
            rows = pl.ds(pl.multiple_of(i * rows_per_step, rows_per_step), rows_per_step)
            x = x_ref[0, rows, :]
            y = x * lax.rsqrt(jnp.mean(x * x, axis=-1, keepdims=True) + EPS) * g_ref[...]
            xn_ref[rows, :] = (y * (1.0 + sc_ref[0]) + sh_ref[0]).astype(BF16)
            return _

        lax.fori_loop(0, xn_ref.shape[0] // rows_per_step, norm_rows, None)
        os_ref[0] = jnp.dot(xn_ref[...], ws_ref[0], preferred_element_type=F32)

    ob_ref[0] = jnp.dot(xn_ref[...], wb_ref[0], preferred_element_type=F32).astype(BF16)


def _inproj(x, scale, shift, g, w_big, w_small, layer):
    b, l, d = x.shape
    tm, tn = min(l, INPROJ_TM), INPROJ_TN
    return pl.pallas_call(
        _inproj_kernel,
        out_shape=(jax.ShapeDtypeStruct((b, l, BIG_W), BF16), jax.ShapeDtypeStruct((b, l, SMALL_W), F32)),
        grid=(b, l // tm, BIG_W // tn),
        in_specs=[pl.BlockSpec((1, tm, d), lambda bi, i, j: (bi, i, 0)),
                  pl.BlockSpec((1, 1, d), lambda bi, i, j: (bi, 0, 0)),
                  pl.BlockSpec((1, 1, d), lambda bi, i, j: (bi, 0, 0)),
                  pl.BlockSpec((1, d), lambda bi, i, j: (0, 0)),
                  pl.BlockSpec((1, d, tn), lambda bi, i, j: (layer, 0, j)),
                  pl.BlockSpec((1, d, SMALL_W), lambda bi, i, j: (layer, 0, 0))],
        out_specs=(pl.BlockSpec((1, tm, tn), lambda bi, i, j: (bi, i, j)),
                   pl.BlockSpec((1, tm, SMALL_W), lambda bi, i, j: (bi, i, 0))),
        scratch_shapes=[pltpu.VMEM((tm, d), BF16)],
        compiler_params=_cparams(("parallel", "parallel", "arbitrary")),
        name="inproj",
    )(x, scale, shift, g, w_big, w_small)


def _group_dot(group, k, w_ref):
    return jnp.dot(group, w_ref[0, k * GROUP_W:(k + 1) * GROUP_W, :], preferred_element_type=F32)


def _outproj_tail(y, hl_ref, hr_ref, gate_ref, g_ref, o_ref):
    yn = gate_ref[0] * (y * lax.rsqrt(jnp.mean(y * y, axis=-1, keepdims=True) + EPS) * g_ref[...])
    half = hl_ref.shape[2]
    o_ref[0, :, 0:half] = hl_ref[0] + yn[:, 0:half]
    o_ref[0, :, half:2 * half] = hr_ref[0] + yn[:, half:2 * half]


def _outproj_kernel(ya_ref, yb_ref, yc_ref, yd_ref, w_ref, hl_ref, hr_ref, gate_ref, g_ref, o_ref):
    y = _group_dot(ya_ref[0], 0, w_ref)
    for k, ref in ((1, yb_ref), (2, yc_ref), (3, yd_ref)):
        y += _group_dot(ref[0], k, w_ref)
    _outproj_tail(y, hl_ref, hr_ref, gate_ref, g_ref, o_ref)


SC_HALO = 16


def _outproj_sconv_kernel(ya_ref, yb_ref, yd_ref, sb_ref, sc_ref, sh_ref, sg_ref, cp_ref, hp_ref, cn_ref, hn_ref,
                          cw_ref, w_ref, hl_ref, hr_ref, gate_ref, g_ref, o_ref):
    y = _group_dot(ya_ref[0], 0, w_ref) + _group_dot(yb_ref[0], 1, w_ref) + _group_dot(yd_ref[0], 3, w_ref)
    i = pl.program_id(1)
    tm = sc_ref.shape[1]
    p = sc_ref[0].astype(F32) * sh_ref[0].astype(F32)
    p_prev = jnp.where(i > 0, cp_ref[0].astype(F32) * hp_ref[0].astype(F32), 0.0)
    p_next = jnp.where(i < pl.num_programs(1) - 1, cn_ref[0].astype(F32) * hn_ref[0].astype(F32), 0.0)
    pe = jnp.concatenate([p_prev, p, p_next], axis=0)
    conv = (pe[SC_HALO - 1:SC_HALO - 1 + tm, :] * cw_ref[0:1, :] + p * cw_ref[1:2, :]
            + pe[SC_HALO + 1:SC_HALO + 1 + tm, :] * cw_ref[2:3, :])
    yc = (sb_ref[0].astype(F32) * conv * _silu(sg_ref[0].astype(F32))).astype(BF16)
    _outproj_tail(y + _group_dot(yc, 2, w_ref), hl_ref, hr_ref, gate_ref, g_ref, o_ref)


def _outproj_sconv(ya, yb, yd, big, cw, w_out, layer, h, gate, g):
    b, l, d = h.shape
    tm = min(l, OUTPROJ_TM)
    assert SC_CONV == 3 and tm % SC_HALO == 0
    col0 = 3584 // GROUP_W
    per = tm // SC_HALO
    n_halo = l // SC_HALO
    yspec = pl.BlockSpec((1, tm, GROUP_W), lambda bi, i: (bi, i, 0))

    def col(k):
        return pl.BlockSpec((1, tm, GROUP_W), lambda bi, i, k=k: (bi, i, col0 + k))

    def prev(k):
        return pl.BlockSpec((1, SC_HALO, GROUP_W), lambda bi, i, k=k: (bi, jnp.maximum(i * per - 1, 0), col0 + k))

    def nxt(k):
        return pl.BlockSpec((1, SC_HALO, GROUP_W),
                            lambda bi, i, k=k: (bi, jnp.minimum((i + 1) * per, n_halo - 1), col0 + k))
    return pl.pallas_call(
        _outproj_sconv_kernel,
        out_shape=jax.ShapeDtypeStruct((b, l, d), F32),
        grid=(b, l // tm),
        in_specs=[yspec, yspec, yspec, col(0), col(1), col(2), col(3), prev(1), prev(2), nxt(1), nxt(2),
                  pl.BlockSpec((SC_CONV, GROUP_W), lambda bi, i: (0, 0)),
                  pl.BlockSpec((1, d, d), lambda bi, i: (layer, 0, 0)),
                  pl.BlockSpec((1, tm, d // 2), lambda bi, i: (bi, i, 0)),
                  pl.BlockSpec((1, tm, d // 2), lambda bi, i: (bi, i, 1)),
                  pl.BlockSpec((1, 1, d), lambda bi, i: (bi, 0, 0)),
                  pl.BlockSpec((1, d), lambda bi, i: (0, 0))],
        out_specs=pl.BlockSpec((1, tm, d), lambda bi, i: (bi, i, 0)),
        compiler_params=_cparams(("parallel", "parallel")),
        name="outproj_sconv",
    )(ya, yb, yd, big, big, big, big, big, big, big, big, cw, w_out, h, h, gate, g)


def _outproj(ya, yb, yc, yd, w_out, layer, h, gate, g):
    b, l, d = h.shape
    tm = min(l, OUTPROJ_TM)
    yspec = pl.BlockSpec((1, tm, GROUP_W), lambda bi, i: (bi, i, 0))
    return pl.pallas_call(
        _outproj_kernel,
        out_shape=jax.ShapeDtypeStruct((b, l, d), F32),
        grid=(b, l // tm),
        in_specs=[yspec, yspec, yspec, yspec,
                  pl.BlockSpec((1, d, d), lambda bi, i: (layer, 0, 0)),
                  pl.BlockSpec((1, tm, d // 2), lambda bi, i: (bi, i, 0)),
                  pl.BlockSpec((1, tm, d // 2), lambda bi, i: (bi, i, 1)),
                  pl.BlockSpec((1, 1, d), lambda bi, i: (bi, 0, 0)),
                  pl.BlockSpec((1, d), lambda bi, i: (0, 0))],
        out_specs=pl.BlockSpec((1, tm, d), lambda bi, i: (bi, i, 0)),
        compiler_params=_cparams(("parallel", "parallel")),
        name="outproj",
    )(ya, yb, yc, yd, w_out, h, h, gate, g)


def _sconv_kernel(b_ref, c_ref, h_ref, gate_ref, w_ref, o_ref, p_ref, *, seq):
    zeros = jnp.zeros((8, GROUP_W), F32)
    p_ref[0:8, :] = zeros
    p_ref[seq + 8:seq + 16, :] = zeros
    p_ref[8:seq + 8, :] = c_ref[0].astype(F32) * h_ref[0].astype(F32)
    rows = min(seq, 256)
    w = w_ref[...]
    for t0 in range(0, seq, rows):
        conv = (p_ref[t0 + 7:t0 + 7 + rows, :] * w[0:1, :] + p_ref[t0 + 8:t0 + 8 + rows, :] * w[1:2, :]
                + p_ref[t0 + 9:t0 + 9 + rows, :] * w[2:3, :])
        y = b_ref[0, t0:t0 + rows, :].astype(F32) * conv * _silu(gate_ref[0, t0:t0 + rows, :].astype(F32))
        o_ref[0, t0:t0 + rows, :] = y.astype(BF16)


def _sconv(big, w):
    b, l, _ = big.shape
    col0 = 3584 // GROUP_W

    def spec(k):
        return pl.BlockSpec((1, l, GROUP_W), lambda bi, k=k: (bi, 0, col0 + k))
    return pl.pallas_call(
        functools.partial(_sconv_kernel, seq=l),
        out_shape=jax.ShapeDtypeStruct((b, l, GROUP_W), BF16),
        grid=(b,),
        in_specs=[spec(0), spec(1), spec(2), spec(3), pl.BlockSpec((SC_CONV, GROUP_W), lambda bi: (0, 0))],
        out_specs=pl.BlockSpec((1, l, GROUP_W), lambda bi: (bi, 0, 0)),
        scratch_shapes=[pltpu.VMEM((l + 16, GROUP_W), F32)],
        compiler_params=_cparams(("parallel",)),
        name="sconv",
    )(big, big, big, big, w)


def _gla_kernel(q_ref, k_ref, v_ref, gate_ref, sm_ref, wd_ref, bd_ref, nw_ref, s0f_ref, s0b_ref,
                *rest, seq, with_out):
    if with_out:
        y_ref, stf_ref, stb_ref, o_ref, st_ref, ko_ref, dec_ref, qp_ref, kp_ref, qi_ref = rest
    else:
        stf_ref, stb_ref, st_ref, ko_ref, dec_ref = rest
        y_ref = o_ref = qp_ref = kp_ref = qi_ref = None
    c = CHUNK
    n_chunks = seq // c
    ri = _iota2((c, c), 0)
    ci = _iota2((c, c), 1)
    incl = (ri >= ci, ri <= ci)
    tri = tuple(jnp.where(m, 1.0, 0.0).astype(BF16) for m in incl)
    qscale = GLA_DK ** -0.5
    last = (c - 1, 0)
    kw = N_HEADS * GLA_DK
    assert c == HEAD_DIM
    r4 = _iota2((N_HEADS * c, c), 0) % c
    c4 = _iota2((N_HEADS * c, c), 1)
    incl4 = (r4 >= c4, r4 <= c4)
    lane_head = _iota2((c, kw), 1) // GLA_DK
    row_head = _iota2((kw, HEAD_DIM), 0) // GLA_DK
    zero16 = jnp.zeros((c, kw), BF16)

    def prep(n, _):
        rows = pl.ds(pl.multiple_of(n * c, c), c)
        sm = sm_ref[0, rows, :]
        kf = k_ref[0, rows, :].astype(F32)
        if with_out:
            qf = q_ref[0, rows, :].astype(F32) * qscale
        for d in (0, 1):
            z = _dot(sm[:, SM_LR[d]:SM_LR[d] + GLA_RANK], wd_ref[d]) + bd_ref[d]
            la = _log_sigmoid(z) * (1.0 / GLA_TAU)
            cum = _dot_exact_lhs(tri[d], la)
            tot = cum[last[d]:last[d] + 1, :]
            krows = pl.ds(pl.multiple_of(n * kw, kw), kw)
            dec_ref[d, krows, :] = jnp.broadcast_to(jnp.exp(tot), (HEAD_DIM, kw)).T
            if with_out:
                mid = cum[c // 2:c // 2 + 1, :]
                qp = qf * jnp.exp(cum - mid)
                kp = kf * jnp.exp(mid - cum)
                qp_ref[d, rows, :] = qp.astype(BF16)
                kp_ref[d, rows, :] = kp.astype(BF16)
                qi_ref[d, rows, :] = (qp * jnp.exp(mid)).astype(BF16)
                ko = kp * jnp.exp(tot - mid)
            else:
                ko = kf * jnp.exp(tot - cum)
            ko_ref[d, krows, :] = ko.T.astype(BF16)
        return _

    lax.fori_loop(0, n_chunks, prep, None, unroll=min(4, n_chunks))

    def step(i, _):
        chunk_of = (i, n_chunks - 1 - i)
        rows = [pl.ds(pl.multiple_of(n * c, c), c) for n in chunk_of]
        krows = [pl.ds(pl.multiple_of(n * kw, kw), kw) for n in chunk_of]
        dirs = (0, 1)
        sts = [st_ref[d] for d in dirs]
        vs = [v_ref[0, rows[d], :] for d in dirs]

        def dv(h):
            return slice(h * HEAD_DIM, (h + 1) * HEAD_DIM)

        if with_out:
            qp4 = [jnp.concatenate([jnp.where(lane_head == h, qp_ref[d, rows[d], :], zero16)
                                    for h in range(N_HEADS)], axis=0) for d in dirs]
            attn4 = [jnp.where(incl4[d], _dot_nt(qp4[d], kp_ref[d, rows[d], :]), 0.0) for d in dirs]
            s_bd = [jnp.concatenate([jnp.where(row_head == h, sts[d], 0.0).astype(BF16)
                                     for h in range(N_HEADS)], axis=1) for d in dirs]
            inter = [_dot(qi_ref[d, rows[d], :], s_bd[d]) for d in dirs]
            intra = [[_dot(attn4[d][h * c:(h + 1) * c, :], vs[d][:, dv(h)]) for h in range(N_HEADS)]
                     for d in dirs]
            for d in dirs:
                o_ref[d, rows[d], :] = inter[d] + jnp.concatenate(intra[d], axis=1)
        upd = [_dot(ko_ref[d, krows[d], :], vs[d]) for d in dirs]
        for d in dirs:
            diag = [upd[d][h * GLA_DK:(h + 1) * GLA_DK, dv(h)] for h in range(N_HEADS)]
            st_ref[d] = sts[d] * dec_ref[d, krows[d], :] + jnp.concatenate(diag, axis=0)
        return _

    st_ref[0] = s0f_ref[0]
    st_ref[1] = s0b_ref[0]
    lax.fori_loop(0, n_chunks, step, None, unroll=2)
    stf_ref[0] = st_ref[0]
    stb_ref[0] = st_ref[1]

    if with_out:
        def finish(n, _):
            rows = pl.ds(pl.multiple_of(n * c, c), c)
            for h in range(N_HEADS):
                hs = slice(h * HEAD_DIM, (h + 1) * HEAD_DIM)
                o = o_ref[0, rows, hs] + o_ref[1, rows, hs]
                o = o * lax.rsqrt(jnp.mean(o * o, axis=-1, keepdims=True) + EPS) * nw_ref[...]
                y_ref[0, rows, hs] = (o * _silu(gate_ref[0, rows, hs].astype(F32))).astype(BF16)
            return _

        lax.fori_loop(0, n_chunks, finish, None)


def _gla(big, small, wd, bd, nw, s0f, s0b, *, with_out):
    b, l, _ = big.shape
    st_shape = jax.ShapeDtypeStruct((b, N_HEADS * GLA_DK, HEAD_DIM), F32)
    st_spec = pl.BlockSpec((1, N_HEADS * GLA_DK, HEAD_DIM), lambda bi: (bi, 0, 0))
    in_specs = [pl.BlockSpec((1, l, 256), lambda bi: (bi, 0, 0)),
                pl.BlockSpec((1, l, 256), lambda bi: (bi, 0, 1)),
                pl.BlockSpec((1, l, 512), lambda bi: (bi, 0, 1)),
                pl.BlockSpec((1, l, 512), lambda bi: (bi, 0, 2)),
                pl.BlockSpec((1, l, SMALL_W), lambda bi: (bi, 0, 0)),
                pl.BlockSpec((2, GLA_RANK, 256), lambda bi: (0, 0, 0)),
                pl.BlockSpec((2, 1, 256), lambda bi: (0, 0, 0)),
                pl.BlockSpec((1, HEAD_DIM), lambda bi: (0, 0)),
                st_spec, st_spec]
    out_shape = [st_shape, st_shape]
    out_specs = [st_spec, st_spec]
    kw = N_HEADS * GLA_DK
    n_chunks = l // CHUNK
    scratch = [pltpu.VMEM((2, kw, HEAD_DIM), F32),
               pltpu.VMEM((2, n_chunks * kw, CHUNK), BF16),
               pltpu.VMEM((2, n_chunks * kw, HEAD_DIM), F32)]
    if with_out:
        out_shape = [jax.ShapeDtypeStruct((b, l, GROUP_W), BF16)] + out_shape
        out_specs = [pl.BlockSpec((1, l, GROUP_W), lambda bi: (bi, 0, 0))] + out_specs
        scratch = [pltpu.VMEM((2, l, GROUP_W), F32)] + scratch + [pltpu.VMEM((2, l, kw), BF16)] * 3
    res = pl.pallas_call(
        functools.partial(_gla_kernel, seq=l, with_out=with_out),
        out_shape=tuple(out_shape), grid=(b,), in_specs=in_specs, out_specs=tuple(out_specs),
        scratch_shapes=scratch, compiler_params=_cparams(("parallel",)),
        name="gla_out" if with_out else "gla_state",
    )(big, big, big, big, small, wd, bd.reshape(2, 1, -1), nw.reshape(1, -1), s0f, s0b)
    return res if with_out else (None,) + tuple(res)


def _unit_triangular_inverses(lows, ri, ci):
    c = lows[0].shape[0]
    eye = jnp.where(ri == ci, 1.0, 0.0)
    pair = (ri // 2) == (ci // 2)
    ts = [eye - jnp.where(pair, low, 0.0) for low in lows]
    s = 2
    while s < c:
        sel = ((ri // (2 * s)) == (ci // (2 * s))) & ((ri // s) != (ci // s))
        tb = [t.astype(BF16) for t in ts]
        ps = [_dot(t, jnp.where(sel, low, 0.0)) for t, low in zip(tb, lows)]
        ts = [t - _dot(p, t16) for t, p, t16 in zip(ts, ps, tb)]
        s *= 2
    return ts


def _gdn_kernel(qkv_ref, gate_ref, sm_ref, cw_ref, gp_ref, nw_ref, s0f_ref, s0b_ref,
                *rest, seq, with_out):
    if with_out:
        y_ref, stf_ref, stb_ref, x_ref, o_ref, st_ref, cumc_ref, cumr_ref = rest
    else:
        stf_ref, stb_ref, x_ref, st_ref, cumc_ref, cumr_ref = rest
        y_ref = o_ref = None
    c = CHUNK
    assert SMALL_W == c
    n_chunks = seq // c
    width = 3 * GROUP_W
    ri = _iota2((c, c), 0)
    ci = _iota2((c, c), 1)
    lower = ri >= ci
    upper = ri <= ci
    tril = jnp.where(lower, 1.0, 0.0).astype(BF16)
    triu = jnp.where(upper, 1.0, 0.0).astype(BF16)

    halo = 16
    off_taps = tuple(j for j in range(GDN_CONV) if j != GDN_CONV // 2)
    sr = _iota2((len(off_taps) * c, c + 2 * halo), 0)
    sc_ = _iota2((len(off_taps) * c, c + 2 * halo), 1)
    tap = jnp.zeros_like(sr)
    for idx, j in enumerate(off_taps):
        tap = jnp.where(sr // c == idx, j, tap)
    shift_mat = jnp.where(sc_ == (sr % c) + halo - GDN_CONV // 2 + tap, 1.0, 0.0).astype(BF16)

    def conv_chunk(n, _):
        r0 = pl.multiple_of(n * c, c)
        prev0 = pl.multiple_of(jnp.maximum(r0 - 16, 0), 16)
        next0 = pl.multiple_of(jnp.minimum(r0 + c, seq - 16), 16)
        prev = jnp.where(n > 0, qkv_ref[0, pl.ds(prev0, 16), :].astype(F32), 0.0).astype(BF16)
        nxt = jnp.where(n < n_chunks - 1, qkv_ref[0, pl.ds(next0, 16), :].astype(F32), 0.0).astype(BF16)
        cur = qkv_ref[0, pl.ds(r0, c), :]
        xe = jnp.concatenate([prev, cur, nxt], axis=0)
        shifted = _dot(shift_mat, xe)
        acc = cur.astype(F32) * cw_ref[GDN_CONV // 2:GDN_CONV // 2 + 1, :]
        for idx, j in enumerate(off_taps):
            acc = acc + shifted[idx * c:(idx + 1) * c, :] * cw_ref[j:j + 1, :]
        acc = _silu(acc)
        for h in range(2 * N_HEADS):
            hs = slice(h * HEAD_DIM, (h + 1) * HEAD_DIM)
            t = acc[:, hs]
            t = t * lax.rsqrt(jnp.sum(t * t, axis=-1, keepdims=True) + EPS)
            if h < N_HEADS:
                t = t * (HEAD_DIM ** -0.5)
            x_ref[pl.ds(r0, c), hs] = t
        x_ref[pl.ds(r0, c), 2 * GROUP_W:width] = acc[:, 2 * GROUP_W:width]
        g = gp_ref[0:1, :] * _softplus(sm_ref[0, pl.ds(r0, c), :] + gp_ref[1:2, :])
        gt = g.T
        for d in (0, 1):
            cumc_ref[d, pl.ds(r0, c), :] = _dot_exact_lhs(tril if d == 0 else triu, g)
            cumr_ref[d, pl.ds(r0, c), :] = _dot_exact_rhs(gt, triu if d == 0 else tril)
        return _

    lax.fori_loop(0, n_chunks, conv_chunk, None)

    chains = [(d, h) for d in (0, 1) for h in range(N_HEADS)]
    n_state = N_HEADS * HEAD_DIM

    def step(i, _):
        rows = [pl.ds(pl.multiple_of(n * c, c), c) for n in (i, n_chunks - 1 - i)]
        cums = [cumc_ref[d, rows[d], :] for d in (0, 1)]
        cum_rows = [cumr_ref[d, rows[d], :] for d in (0, 1)]
        betas = [_sigmoid(sm_ref[0, rows[d], :]) for d in (0, 1)]
        incl = (lower, upper)
        strict = (ri > ci, ri < ci)
        last = (c - 1, 0)

        kfs, vbs, kbs, e_cs, tots, decays, kos, qfs = [], [], [], [], [], [], [], []
        for d, h in chains:
            la, lb = SM_A[d] + h, SM_B[d] + h
            cum_c = cums[d][:, la:la + 1]
            cum_r = cum_rows[d][la:la + 1, :]
            tot = cums[d][last[d]:last[d] + 1, la:la + 1]
            beta_c = betas[d][:, lb:lb + 1]
            kf = x_ref[rows[d], GROUP_W + h * HEAD_DIM:GROUP_W + (h + 1) * HEAD_DIM]
            vf = x_ref[rows[d], 2 * GROUP_W + h * HEAD_DIM:2 * GROUP_W + (h + 1) * HEAD_DIM]
            e_c = jnp.exp(cum_c)
            kfs.append(kf)
            kbs.append(kf * beta_c)
            vbs.append(vf * beta_c)
            e_cs.append(e_c)
            tots.append(tot)
            decays.append(jnp.where(incl[d], jnp.exp(jnp.minimum(cum_c - cum_r, 0.0)), 0.0))
            kos.append(kf * jnp.exp(tot - cum_c))
            if with_out:
                qfs.append(x_ref[rows[d], h * HEAD_DIM:(h + 1) * HEAD_DIM])

        if with_out:
            prods = [_dot_nt(jnp.concatenate([kb, qf], axis=0), kf) for kb, qf, kf in zip(kbs, qfs, kfs)]
            attns = [p[c:, :] * dec for p, dec in zip(prods, decays)]
        else:
            prods = [_dot_nt(kb, kf) for kb, kf in zip(kbs, kfs)]
        lows = [jnp.where(strict[d], p[:c, :] * dec, 0.0) for (d, _), p, dec in zip(chains, prods, decays)]
        ts = _unit_triangular_inverses(lows, ri, ci)
        uws = [_dot(t, jnp.concatenate([vb, kb * e_c], axis=1)) for t, vb, kb, e_c in zip(ts, vbs, kbs, e_cs)]
        sts = [st_ref[j * HEAD_DIM:(j + 1) * HEAD_DIM, :] for j in range(len(chains))]
        if with_out:
            reads = [_dot(jnp.concatenate([uw[:, HEAD_DIM:], qf * e_c], axis=0), st)
                     for uw, qf, e_c, st in zip(uws, qfs, e_cs, sts)]
        else:
            reads = [_dot(uw[:, HEAD_DIM:], st) for uw, st in zip(uws, sts)]
        v_news = [uw[:, :HEAD_DIM] - r[:c, :] for uw, r in zip(uws, reads)]
        if with_out:
            writes = [_dot(jnp.concatenate([a, ko.T], axis=0), vn) for a, ko, vn in zip(attns, kos, v_news)]
        else:
            writes = [_dot_tn(ko, vn) for ko, vn in zip(kos, v_news)]
        for j, (d, h) in enumerate(chains):
            if with_out:
                o_ref[d, rows[d], h * HEAD_DIM:(h + 1) * HEAD_DIM] = reads[j][c:, :] + writes[j][:c, :]
                upd = writes[j][c:, :]
            else:
                upd = writes[j]
            st_ref[j * HEAD_DIM:(j + 1) * HEAD_DIM, :] = sts[j] * jnp.exp(tots[j]) + upd
        return _

    st_ref[0:n_state, :] = s0f_ref[0]
    st_ref[n_state:2 * n_state, :] = s0b_ref[0]
    lax.fori_loop(0, n_chunks, step, None, unroll=2)
    stf_ref[0] = st_ref[0:n_state, :]
    stb_ref[0] = st_ref[n_state:2 * n_state, :]

    if with_out:
        def finish(n, _):
            rows = pl.ds(pl.multiple_of(n * c, c), c)
            for h in range(N_HEADS):
                hs = slice(h * HEAD_DIM, (h + 1) * HEAD_DIM)
                o = o_ref[0, rows, hs] + o_ref[1, rows, hs]
                o = o * lax.rsqrt(jnp.mean(o * o, axis=-1, keepdims=True) + EPS) * nw_ref[...]
                y_ref[0, rows, hs] = (o * _silu(gate_ref[0, rows, hs].astype(F32))).astype(BF16)
            return _

        lax.fori_loop(0, n_chunks, finish, None)


def _gdn(big, small, cw, gp, nw, s0f, s0b, *, with_out):
    b, l, _ = big.shape
    st_shape = jax.ShapeDtypeStruct((b, N_HEADS * HEAD_DIM, HEAD_DIM), F32)
    st_spec = pl.BlockSpec((1, N_HEADS * HEAD_DIM, HEAD_DIM), lambda bi: (bi, 0, 0))
    in_specs = [pl.BlockSpec((1, l, 3 * GROUP_W), lambda bi: (bi, 0, 1)),
                pl.BlockSpec((1, l, GROUP_W), lambda bi: (bi, 0, 6)),
                pl.BlockSpec((1, l, SMALL_W), lambda bi: (bi, 0, 0)),
                pl.BlockSpec((GDN_CONV, 3 * GROUP_W), lambda bi: (0, 0)),
                pl.BlockSpec((8, SMALL_W), lambda bi: (0, 0)),
                pl.BlockSpec((1, HEAD_DIM), lambda bi: (0, 0)),
                st_spec, st_spec]
    out_shape = [st_shape, st_shape]
    out_specs = [st_spec, st_spec]
    scratch = [pltpu.VMEM((l, 3 * GROUP_W), F32), pltpu.VMEM((2 * N_HEADS * HEAD_DIM, HEAD_DIM), F32)]
    if with_out:
        out_shape = [jax.ShapeDtypeStruct((b, l, GROUP_W), BF16)] + out_shape
        out_specs = [pl.BlockSpec((1, l, GROUP_W), lambda bi: (bi, 0, 0))] + out_specs
        scratch = [scratch[0], pltpu.VMEM((2, l, GROUP_W), F32), scratch[1]]
    scratch += [pltpu.VMEM((2, l, SMALL_W), F32), pltpu.VMEM((2, l, CHUNK), F32)]
    res = pl.pallas_call(
        functools.partial(_gdn_kernel, seq=l, with_out=with_out),
        out_shape=tuple(out_shape), grid=(b,), in_specs=in_specs, out_specs=tuple(out_specs),
        scratch_shapes=scratch, compiler_params=_cparams(("parallel",)),
        name="gdn_out" if with_out else "gdn_state",
    )(big, big, small, cw, gp, nw.reshape(1, -1), s0f, s0b)
    return res if with_out else (None,) + tuple(res)


def _gdn_gate_params(a_log, dt_bias):
    gp = jnp.zeros((8, SMALL_W), F32)
    gp = gp.at[0, SM_A[0]:SM_A[0] + 2 * N_HEADS].set(-jnp.exp(a_log.astype(F32)).reshape(-1))
    gp = gp.at[1, SM_A[0]:SM_A[0] + 2 * N_HEADS].set(dt_bias.astype(F32).reshape(-1))
    return gp


def _rope(x, cos, sin, lane):
    swapped = jnp.where((lane % 64) < 32, pltpu.roll(x, 96, 1), pltpu.roll(x, 32, 1))
    return x * cos + swapped * sin


def _softmax_sink_pv(s, sink_col, vv):
    m = jnp.maximum(jnp.max(s, axis=-1, keepdims=True), sink_col)
    p = jnp.exp(s - m)
    den = jnp.sum(p, axis=-1, keepdims=True) + jnp.exp(sink_col - m)
    return _dot(p, vv) * (1.0 / den)


def _swa_kernel(*refs, seq, ctx_len, with_ctx):
    if with_ctx:
        (q_ref, k_ref, v_ref, gate_ref, kc_ref, vc_ref, cos_ref, sin_ref, sink_ref, qc_ref, gatec_ref,
         y_ref, yc_ref, kr_ref, bias_ref) = refs
    else:
        (q_ref, k_ref, v_ref, gate_ref, kc_ref, vc_ref, cos_ref, sin_ref, sink_ref,
         y_ref, kr_ref, bias_ref) = refs
    blk = SWA_BLOCK
    nb = seq // blk
    win = 3 * blk
    grp = N_HEADS // SWA_KV_HEADS
    scale = HEAD_DIM ** -0.5
    lane = _iota2((blk, HEAD_DIM), 1)

    def rope_k(n, _):
        rows = pl.ds(pl.multiple_of(n * blk, blk), blk)
        cos = cos_ref[rows, :]
        sin = sin_ref[rows, :]
        for g in range(SWA_KV_HEADS):
            hs = slice(g * HEAD_DIM, (g + 1) * HEAD_DIM)
            kr_ref[rows, hs] = _rope(k_ref[0, rows, hs].astype(F32), cos, sin, lane).astype(BF16)
        return _

    lax.fori_loop(0, nb, rope_k, None)

    rowq = _iota2((grp * blk, win + ctx_len), 0) % blk
    colk = _iota2((grp * blk, win + ctx_len), 1)
    is_ctx = colk >= win
    for w in range(3):
        visible = is_ctx | (jnp.abs(rowq + w * blk - colk) <= SWA_WINDOW)
        bias_ref[w] = jnp.where(visible, 0.0, NEG_BIG)

    def sink_column(g, n_rows):
        hr = _iota2((grp * n_rows, 1), 0) // n_rows
        col = jnp.zeros((grp * n_rows, 1), F32)
        for j in range(grp):
            col = jnp.where(hr == j, sink_ref[0:1, g * grp + j:g * grp + j + 1], col)
        return col

    def q_block(n, _):
        r0 = pl.multiple_of(n * blk, blk)
        rows = pl.ds(r0, blk)
        k0 = pl.multiple_of(jnp.clip(r0 - blk, 0, seq - win), blk)
        cos = cos_ref[rows, :]
        sin = sin_ref[rows, :]
        bias = bias_ref[(r0 - k0) // blk]
        for g in range(SWA_KV_HEADS):
            hs = slice(g * HEAD_DIM, (g + 1) * HEAD_DIM)
            qg = jnp.concatenate(
                [_rope(q_ref[0, rows, (g * grp + j) * HEAD_DIM:(g * grp + j + 1) * HEAD_DIM].astype(F32),
                       cos, sin, lane) * scale for j in range(grp)], axis=0).astype(BF16)
            kk = jnp.concatenate([kr_ref[pl.ds(k0, win), hs], kc_ref[0, :, hs]], axis=0)
            vv = jnp.concatenate([v_ref[0, pl.ds(k0, win), hs], vc_ref[0, :, hs]], axis=0)
            s = _dot_nt(qg, kk) + bias
            o = _softmax_sink_pv(s, sink_column(g, blk), vv)
            for j in range(grp):
                cs = slice((g * grp + j) * HEAD_DIM, (g * grp + j + 1) * HEAD_DIM)
                y_ref[0, rows, cs] = (o[j * blk:(j + 1) * blk, :]
                                      * _silu(gate_ref[0, rows, cs].astype(F32))).astype(BF16)
        return _

    lax.fori_loop(0, nb, q_block, None, unroll=2)

    if with_ctx:
        for g in range(SWA_KV_HEADS):
            hs = slice(g * HEAD_DIM, (g + 1) * HEAD_DIM)
            qg = jnp.concatenate([qc_ref[0, :, (g * grp + j) * HEAD_DIM:(g * grp + j + 1) * HEAD_DIM]
                                  for j in range(grp)], axis=0)
            s = _dot_nt(qg, kc_ref[0, :, hs]) * scale
            o = _softmax_sink_pv(s, sink_column(g, ctx_len), vc_ref[0, :, hs])
            for j in range(grp):
                cs = slice((g * grp + j) * HEAD_DIM, (g * grp + j + 1) * HEAD_DIM)
                yc_ref[0, :, cs] = (o[j * ctx_len:(j + 1) * ctx_len, :]
                                    * _silu(gatec_ref[0, :, cs].astype(F32))).astype(BF16)


def _swa(big, bigc, cos, sin, sink, *, with_ctx):
    b, l, _ = big.shape
    lc = bigc.shape[1]
    in_specs = [pl.BlockSpec((1, l, 512), lambda bi: (bi, 0, 11)),
                pl.BlockSpec((1, l, 256), lambda bi: (bi, 0, 24)),
                pl.BlockSpec((1, l, 256), lambda bi: (bi, 0, 25)),
                pl.BlockSpec((1, l, 512), lambda bi: (bi, 0, 13)),
                pl.BlockSpec((1, lc, 256), lambda bi: (bi, 0, 24)),
                pl.BlockSpec((1, lc, 256), lambda bi: (bi, 0, 25)),
                pl.BlockSpec((l, HEAD_DIM), lambda bi: (0, 0)),
                pl.BlockSpec((l, HEAD_DIM), lambda bi: (0, 0)),
                pl.BlockSpec((1, LANES), lambda bi: (0, 0))]
    args = [big, big, big, big, bigc, bigc, cos, sin, sink]
    out_shape = [jax.ShapeDtypeStruct((b, l, GROUP_W), BF16)]
    out_specs = [pl.BlockSpec((1, l, GROUP_W), lambda bi: (bi, 0, 0))]
    if with_ctx:
        in_specs += [pl.BlockSpec((1, lc, 512), lambda bi: (bi, 0, 11)),
                     pl.BlockSpec((1, lc, 512), lambda bi: (bi, 0, 13))]
        args += [bigc, bigc]
        out_shape.append(jax.ShapeDtypeStruct((b, lc, GROUP_W), BF16))
        out_specs.append(pl.BlockSpec((1, lc, GROUP_W), lambda bi: (bi, 0, 0)))
    res = pl.pallas_call(
        functools.partial(_swa_kernel, seq=l, ctx_len=lc, with_ctx=with_ctx),
        out_shape=tuple(out_shape), grid=(b,), in_specs=in_specs, out_specs=tuple(out_specs),
        scratch_shapes=[pltpu.VMEM((l, SWA_KV_HEADS * HEAD_DIM), BF16),
                        pltpu.VMEM((3, (N_HEADS // SWA_KV_HEADS) * SWA_BLOCK, 3 * SWA_BLOCK + lc), F32)],
        compiler_params=_cparams(("parallel",)),
        name="swa_ctx" if with_ctx else "swa",
    )(*args)
    return (res[0], res[1]) if with_ctx else (res[0], None)


def _rope_tables(seq_len):
    rows = seq_len // GRID_W
    row = jnp.repeat(jnp.arange(rows, dtype=F32), GRID_W)
    col = jnp.tile(jnp.arange(GRID_W, dtype=F32), rows)
    axis_dim = HEAD_DIM // 2
    inv_freq = ROPE_BASE ** (-jnp.arange(0, axis_dim, 2, dtype=F32) / axis_dim)
    ang_r = row[:, None] * inv_freq
    ang_c = col[:, None] * inv_freq
    cos = jnp.concatenate([jnp.cos(ang_r), jnp.cos(ang_r), jnp.cos(ang_c), jnp.cos(ang_c)], axis=-1)
    sin = jnp.concatenate([-jnp.sin(ang_r), jnp.sin(ang_r), -jnp.sin(ang_c), jnp.sin(ang_c)], axis=-1)
    return cos, sin


_PACK_RUNS = ((0, 0, 1024), (1024, 1056, 2048), (3072, 3120, 4096))
_LR_COL0 = 1024
_GDN_GATE_COL0 = 3104


def _pack_kernel(wt_ref, big_ref, small_ref):
    step = 512
    for dst, src, width in _PACK_RUNS:
        for o in range(0, width, step):
            big_ref[0, :, dst + o:dst + o + step] = wt_ref[0, src + o:src + o + step, :].T.astype(BF16)
    assert _LR_COL0 % LANES == 0 and _GDN_GATE_COL0 % LANES == SM_A[0]
    gate_win0 = _GDN_GATE_COL0 - SM_A[0]
    lane = _iota2((wt_ref.shape[2], LANES), 1)
    lr_win = wt_ref[0, _LR_COL0:_LR_COL0 + LANES, :].T
    gate_win = wt_ref[0, gate_win0:gate_win0 + LANES, :].T
    small = jnp.where(lane < SM_A[0], lr_win, jnp.where(lane < SM_B[1] + N_HEADS, gate_win, 0.0))
    small_ref[0] = small.astype(BF16)


def _pack_w_in(w_in):
    n_layers, d, n = w_in.shape
    kb = PACK_KB
    return pl.pallas_call(
        _pack_kernel,
        out_shape=(jax.ShapeDtypeStruct((n_layers, d, BIG_W), BF16),
                   jax.ShapeDtypeStruct((n_layers, d, SMALL_W), BF16)),
        grid=(n_layers, d // kb),
        in_specs=[pl.BlockSpec((1, n, kb), lambda l, i: (l, 0, i))],
        out_specs=(pl.BlockSpec((1, kb, BIG_W), lambda l, i: (l, i, 0)),
                   pl.BlockSpec((1, kb, SMALL_W), lambda l, i: (l, i, 0))),
        compiler_params=_cparams(("parallel", "parallel")),
        name="pack_w_in",
    )(jnp.swapaxes(w_in, 1, 2))


def kernel(x, c, ctx, c_ctx, ada_w, ada_b, norm_pre, norm_post, w_in, w_out, gla_w_decay, gla_b_decay,
           gla_norm, gdn_conv, gdn_a_log, gdn_dt_bias, gdn_norm, sc_conv, swa_sink):
    b, l, d = x.shape
    cos, sin = _rope_tables(l)
    cc = jnp.zeros((8, d), F32).at[:b].set(c).at[b].set(c_ctx)
    mod = _ada(cc, ada_w, ada_b)
    h, hc = x, ctx
    lc = ctx.shape[1]
    n_layers = ada_w.shape[0]
    w_big, w_small = _pack_w_in(w_in)
    w_o = w_out.astype(BF16)
    for layer in range(n_layers):
        with_ctx = layer < n_layers - 1
        shift, scale, gate = (mod[layer, :, i * d:(i + 1) * d] for i in range(3))

        def per_batch(t):
            return t[:b, None, :], t[b][None, None, :]
        (shift_l, shift_c), (scale_l, scale_c), (gate_l, gate_c) = per_batch(shift), per_batch(scale), per_batch(gate)
        g_pre = norm_pre[layer].reshape(1, d)
        big, small = _inproj(h, scale_l, shift_l, g_pre, w_big, w_small, layer)
        bigc, smallc = _inproj(hc.reshape(1, b * lc, d), scale_c, shift_c, g_pre, w_big, w_small, layer)
        bigc, smallc = bigc.reshape(b, lc, BIG_W), smallc.reshape(b, lc, SMALL_W)

        zf = jnp.zeros((b, N_HEADS * GLA_DK, HEAD_DIM), F32)
        yca, stf, stb = _gla(bigc, smallc, gla_w_decay[layer], gla_b_decay[layer], gla_norm[layer], zf, zf,
                             with_out=with_ctx)
        ya, _, _ = _gla(big, small, gla_w_decay[layer], gla_b_decay[layer], gla_norm[layer], stf, stb,
                        with_out=True)

        gp = _gdn_gate_params(gdn_a_log[layer], gdn_dt_bias[layer])
        zg = jnp.zeros((b, N_HEADS * HEAD_DIM, HEAD_DIM), F32)
        ycb, gtf, gtb = _gdn(bigc, smallc, gdn_conv[layer], gp, gdn_norm[layer], zg, zg, with_out=with_ctx)
        yb, _, _ = _gdn(big, small, gdn_conv[layer], gp, gdn_norm[layer], gtf, gtb, with_out=True)

        sink = jnp.zeros((1, LANES), F32).at[0, :N_HEADS].set(swa_sink[layer])
        yd, ycd = _swa(big, bigc, cos, sin, sink, with_ctx=with_ctx)

        g_post = norm_post[layer].reshape(1, d)
        h_new = _outproj_sconv(ya, yb, yd, big, sc_conv[layer], w_o, layer, h, gate_l, g_post)
        if with_ctx:
            ycc = _sconv(bigc, sc_conv[layer])
            flat = [t.reshape(1, b * lc, GROUP_W) for t in (yca, ycb, ycc, ycd)]
            hc = _outproj(*flat, w_o, layer, hc.reshape(1, b * lc, d), gate_c, g_post).reshape(b, lc, d)
        h = h_new
    return h
```

```python
import functools

import jax
import jax.numpy as jnp
from jax import lax
from jax.experimental import pallas as pl
from jax.experimental.pallas import tpu as pltpu

F32 = jnp.float32
BF16 = jnp.bfloat16

D_MODEL = 2048
N_LAYERS = 2
GRID_W = 64
GROUP_W = 512
HEAD_DIM = 128
N_HEADS = 4
EPS = 1e-6
GLA_DK = 64
GLA_RANK = 16
GLA_TAU = 16.0
GDN_CONV = 5
SC_CONV = 3
SWA_KV_HEADS = 2
SWA_WINDOW = 128
ROPE_BASE = 10000.0

_SPLITS = (
    ("gla_q", 256), ("gla_k", 256), ("gla_v", 512), ("gla_lr_f", 16), ("gla_lr_b", 16), ("gla_gate", 512),
    ("gdn_q", 512), ("gdn_k", 512), ("gdn_v", 512),
    ("gdn_a_f", 4), ("gdn_a_b", 4), ("gdn_b_f", 4), ("gdn_b_b", 4), ("gdn_gate", 512),
    ("sc_b", 512), ("sc_c", 512), ("sc_h", 512), ("sc_gate", 512),
    ("swa_q", 512), ("swa_k", 256), ("swa_v", 256), ("swa_gate", 512),
)
BIG_W = 7168
SMALL_W = 128
SM_LR = (0, 16)
SM_A = (32, 36)
SM_B = (40, 44)

LANES = 128
VMEM_LIMIT = 56 * 1024 * 1024

CHUNK = 128
SWA_BLOCK = 128
INPROJ_TM = 1024
INPROJ_TN = 1024
OUTPROJ_TM = 512
ADA_TN = 1024
PACK_KB = 256

NEG_BIG = -1e30


def _cparams(sem):
    return pltpu.CompilerParams(dimension_semantics=sem, vmem_limit_bytes=VMEM_LIMIT)


def _dot(a, b):
    return lax.dot_general(a.astype(BF16), b.astype(BF16), (((1,), (0,)), ((), ())),
                           preferred_element_type=F32)


def _dot_nt(a, b):
    return lax.dot_general(a.astype(BF16), b.astype(BF16), (((1,), (1,)), ((), ())),
                           preferred_element_type=F32)


def _dot_tn(a, b):
    return lax.dot_general(a.astype(BF16), b.astype(BF16), (((0,), (0,)), ((), ())),
                           preferred_element_type=F32)


def _split(x):
    hi = x.astype(BF16)
    lo = (x - hi.astype(F32)).astype(BF16)
    return hi, lo


def _dot_exact_lhs(a_bf16, x):
    hi, lo = _split(x)
    return _dot(a_bf16, hi) + _dot(a_bf16, lo)


def _dot_exact_rhs(x, b_bf16):
    hi, lo = _split(x)
    return _dot(hi, b_bf16) + _dot(lo, b_bf16)


def _sigmoid(x):
    return 0.5 * jnp.tanh(0.5 * x) + 0.5


def _silu(x):
    return x * _sigmoid(x)


def _softplus(x):
    return jnp.maximum(x, 0.0) + jnp.log(1.0 + jnp.exp(-jnp.abs(x)))


def _log_sigmoid(x):
    return jnp.minimum(x, 0.0) - jnp.log(1.0 + jnp.exp(-jnp.abs(x)))


def _iota2(shape, dim):
    return lax.broadcasted_iota(jnp.int32, shape, dim)


ADA_K_SPLIT = 4


def _ada_kernel(c_ref, *refs):
    w_refs, (b_ref, o_ref) = refs[:ADA_K_SPLIT], refs[ADA_K_SPLIT:]
    a = _silu(c_ref[...])
    kq = a.shape[1] // ADA_K_SPLIT
    acc = b_ref[0]
    for q, w_ref in enumerate(w_refs):
        acc = acc + _dot(a[:, q * kq:(q + 1) * kq], w_ref[0])
    o_ref[0] = acc


def _ada(cc, ada_w, ada_b):
    n_layers, d, n = ada_w.shape
    tn = ADA_TN
    kq = d // ADA_K_SPLIT
    w_specs = [pl.BlockSpec((1, kq, tn), lambda l, j, q=q: (l, q, j)) for q in range(ADA_K_SPLIT)]
    return pl.pallas_call(
        _ada_kernel,
        out_shape=jax.ShapeDtypeStruct((n_layers, 8, n), F32),
        grid=(n_layers, n // tn),
        in_specs=[pl.BlockSpec((8, d), lambda l, j: (0, 0))] + w_specs
        + [pl.BlockSpec((1, 1, tn), lambda l, j: (l, 0, j))],
        out_specs=pl.BlockSpec((1, 8, tn), lambda l, j: (l, 0, j)),
        compiler_params=_cparams(("parallel", "parallel")),
        name="ada_mod",
    )(cc, *([ada_w] * ADA_K_SPLIT), ada_b.reshape(n_layers, 1, n))


def _inproj_kernel(x_ref, sc_ref, sh_ref, g_ref, wb_ref, ws_ref, ob_ref, os_ref, xn_ref):
    @pl.when(pl.program_id(2) == 0)
    def _():
        rows_per_step = 128
        gain = g_ref[...] * (1.0 + sc_ref[0])

        def norm_rows(i, _):
            rows = pl.ds(pl.multiple_of(i * rows_per_step, rows_per_step), rows_per_step)
            x = x_ref[0, rows, :]
            y = x * lax.rsqrt(jnp.mean(x * x, axis=-1, keepdims=True) + EPS)
            xn_ref[rows, :] = (y * gain + sh_ref[0]).astype(BF16)
            return _

        lax.fori_loop(0, xn_ref.shape[0] // rows_per_step, norm_rows, None)
        os_ref[0] = jnp.dot(xn_ref[...], ws_ref[0], preferred_element_type=F32)

    ob_ref[0] = jnp.dot(xn_ref[...], wb_ref[0], preferred_element_type=F32).astype(BF16)


def _inproj(x, scale, shift, g, w_big, w_small, layer):
    b, l, d = x.shape
    tm, tn = min(l, INPROJ_TM), INPROJ_TN
    return pl.pallas_call(
        _inproj_kernel,
        out_shape=(jax.ShapeDtypeStruct((b, l, BIG_W), BF16), jax.ShapeDtypeStruct((b, l, SMALL_W), F32)),
        grid=(b, l // tm, BIG_W // tn),
        in_specs=[pl.BlockSpec((1, tm, d), lambda bi, i, j: (bi, i, 0)),
                  pl.BlockSpec((1, 1, d), lambda bi, i, j: (bi, 0, 0)),
                  pl.BlockSpec((1, 1, d), lambda bi, i, j: (bi, 0, 0)),
                  pl.BlockSpec((1, d), lambda bi, i, j: (0, 0)),
                  pl.BlockSpec((1, d, tn), lambda bi, i, j: (layer, 0, j)),
                  pl.BlockSpec((1, d, SMALL_W), lambda bi, i, j: (layer, 0, 0))],
        out_specs=(pl.BlockSpec((1, tm, tn), lambda bi, i, j: (bi, i, j)),
                   pl.BlockSpec((1, tm, SMALL_W), lambda bi, i, j: (bi, i, 0))),
        scratch_shapes=[pltpu.VMEM((tm, d), BF16)],
        compiler_params=_cparams(("parallel", "parallel", "arbitrary")),
        name="inproj",
    )(x, scale, shift, g, w_big, w_small)


def _group_dot(group, k, w_ref):
    return jnp.dot(group, w_ref[0, k * GROUP_W:(k + 1) * GROUP_W, :], preferred_element_type=F32)


def _outproj_tail(y, hl_ref, hr_ref, gate_ref, g_ref, o_ref):
    yn = gate_ref[0] * (y * lax.rsqrt(jnp.mean(y * y, axis=-1, keepdims=True) + EPS) * g_ref[...])
    half = hl_ref.shape[2]
    o_ref[0, :, 0:half] = hl_ref[0] + yn[:, 0:half]
    o_ref[0, :, half:2 * half] = hr_ref[0] + yn[:, half:2 * half]


def _outproj_kernel(ya_ref, yb_ref, yc_ref, yd_ref, w_ref, hl_ref, hr_ref, gate_ref, g_ref, o_ref):
    y = _group_dot(ya_ref[0], 0, w_ref)
    for k, ref in ((1, yb_ref), (2, yc_ref), (3, yd_ref)):
        y += _group_dot(ref[0], k, w_ref)
    _outproj_tail(y, hl_ref, hr_ref, gate_ref, g_ref, o_ref)


SC_HALO = 16


def _outproj_sconv_kernel(ya_ref, yb_ref, yd_ref, sb_ref, sc_ref, sh_ref, sg_ref, cp_ref, hp_ref, cn_ref, hn_ref,
                          cw_ref, w_ref, hl_ref, hr_ref, gate_ref, g_ref, o_ref):
    y = _group_dot(ya_ref[0], 0, w_ref) + _group_dot(yb_ref[0], 1, w_ref) + _group_dot(yd_ref[0], 3, w_ref)
    i = pl.program_id(1)
    tm = sc_ref.shape[1]
    p = sc_ref[0].astype(F32) * sh_ref[0].astype(F32)
    p_prev = jnp.where(i > 0, cp_ref[0].astype(F32) * hp_ref[0].astype(F32), 0.0)
    p_next = jnp.where(i < pl.num_programs(1) - 1, cn_ref[0].astype(F32) * hn_ref[0].astype(F32), 0.0)
    pe = jnp.concatenate([p_prev, p, p_next], axis=0)
    conv = (pe[SC_HALO - 1:SC_HALO - 1 + tm, :] * cw_ref[0:1, :] + p * cw_ref[1:2, :]
            + pe[SC_HALO + 1:SC_HALO + 1 + tm, :] * cw_ref[2:3, :])
    yc = (sb_ref[0].astype(F32) * conv * _silu(sg_ref[0].astype(F32))).astype(BF16)
    _outproj_tail(y + _group_dot(yc, 2, w_ref), hl_ref, hr_ref, gate_ref, g_ref, o_ref)


def _outproj_sconv(ya, yb, yd, big, cw, w_out, layer, h, gate, g):
    b, l, d = h.shape
    tm = min(l, OUTPROJ_TM)
    assert SC_CONV == 3 and tm % SC_HALO == 0
    col0 = 3584 // GROUP_W
    per = tm // SC_HALO
    n_halo = l // SC_HALO
    yspec = pl.BlockSpec((1, tm, GROUP_W), lambda bi, i: (bi, i, 0))

    def col(k):
        return pl.BlockSpec((1, tm, GROUP_W), lambda bi, i, k=k: (bi, i, col0 + k))

    def prev(k):
        return pl.BlockSpec((1, SC_HALO, GROUP_W), lambda bi, i, k=k: (bi, jnp.maximum(i * per - 1, 0), col0 + k))

    def nxt(k):
        return pl.BlockSpec((1, SC_HALO, GROUP_W),
                            lambda bi, i, k=k: (bi, jnp.minimum((i + 1) * per, n_halo - 1), col0 + k))
    return pl.pallas_call(
        _outproj_sconv_kernel,
        out_shape=jax.ShapeDtypeStruct((b, l, d), F32),
        grid=(b, l // tm),
        in_specs=[yspec, yspec, yspec, col(0), col(1), col(2), col(3), prev(1), prev(2), nxt(1), nxt(2),
                  pl.BlockSpec((SC_CONV, GROUP_W), lambda bi, i: (0, 0)),
                  pl.BlockSpec((1, d, d), lambda bi, i: (layer, 0, 0)),
                  pl.BlockSpec((1, tm, d // 2), lambda bi, i: (bi, i, 0)),
                  pl.BlockSpec((1, tm, d // 2), lambda bi, i: (bi, i, 1)),
                  pl.BlockSpec((1, 1, d), lambda bi, i: (bi, 0, 0)),
                  pl.BlockSpec((1, d), lambda bi, i: (0, 0))],
        out_specs=pl.BlockSpec((1, tm, d), lambda bi, i: (bi, i, 0)),
        compiler_params=_cparams(("parallel", "parallel")),
        name="outproj_sconv",
    )(ya, yb, yd, big, big, big, big, big, big, big, big, cw, w_out, h, h, gate, g)


def _outproj(ya, yb, yc, yd, w_out, layer, h, gate, g):
    b, l, d = h.shape
    tm = min(l, OUTPROJ_TM)
    yspec = pl.BlockSpec((1, tm, GROUP_W), lambda bi, i: (bi, i, 0))
    return pl.pallas_call(
        _outproj_kernel,
        out_shape=jax.ShapeDtypeStruct((b, l, d), F32),
        grid=(b, l // tm),
        in_specs=[yspec, yspec, yspec, yspec,
                  pl.BlockSpec((1, d, d), lambda bi, i: (layer, 0, 0)),
                  pl.BlockSpec((1, tm, d // 2), lambda bi, i: (bi, i, 0)),
                  pl.BlockSpec((1, tm, d // 2), lambda bi, i: (bi, i, 1)),
                  pl.BlockSpec((1, 1, d), lambda bi, i: (bi, 0, 0)),
                  pl.BlockSpec((1, d), lambda bi, i: (0, 0))],
        out_specs=pl.BlockSpec((1, tm, d), lambda bi, i: (bi, i, 0)),
        compiler_params=_cparams(("parallel", "parallel")),
        name="outproj",
    )(ya, yb, yc, yd, w_out, h, h, gate, g)


def _sconv_kernel(b_ref, c_ref, h_ref, gate_ref, w_ref, o_ref, p_ref, *, seq):
    zeros = jnp.zeros((8, GROUP_W), F32)
    p_ref[0:8, :] = zeros
    p_ref[seq + 8:seq + 16, :] = zeros
    p_ref[8:seq + 8, :] = c_ref[0].astype(F32) * h_ref[0].astype(F32)
    rows = min(seq, 256)
    w = w_ref[...]
    for t0 in range(0, seq, rows):
        conv = (p_ref[t0 + 7:t0 + 7 + rows, :] * w[0:1, :] + p_ref[t0 + 8:t0 + 8 + rows, :] * w[1:2, :]
                + p_ref[t0 + 9:t0 + 9 + rows, :] * w[2:3, :])
        y = b_ref[0, t0:t0 + rows, :].astype(F32) * conv * _silu(gate_ref[0, t0:t0 + rows, :].astype(F32))
        o_ref[0, t0:t0 + rows, :] = y.astype(BF16)


def _sconv(big, w):
    b, l, _ = big.shape
    col0 = 3584 // GROUP_W

    def spec(k):
        return pl.BlockSpec((1, l, GROUP_W), lambda bi, k=k: (bi, 0, col0 + k))
    return pl.pallas_call(
        functools.partial(_sconv_kernel, seq=l),
        out_shape=jax.ShapeDtypeStruct((b, l, GROUP_W), BF16),
        grid=(b,),
        in_specs=[spec(0), spec(1), spec(2), spec(3), pl.BlockSpec((SC_CONV, GROUP_W), lambda bi: (0, 0))],
        out_specs=pl.BlockSpec((1, l, GROUP_W), lambda bi: (bi, 0, 0)),
        scratch_shapes=[pltpu.VMEM((l + 16, GROUP_W), F32)],
        compiler_params=_cparams(("parallel",)),
        name="sconv",
    )(big, big, big, big, w)


def _gla_kernel(q_ref, k_ref, v_ref, gate_ref, sm_ref, wd_ref, bd_ref, nw_ref, s0f_ref, s0b_ref,
                *rest, seq, with_out):
    if with_out:
        y_ref, stf_ref, stb_ref, o_ref, st_ref, ko_ref, dec_ref, qp_ref, kp_ref, qi_ref = rest
    else:
        stf_ref, stb_ref, st_ref, ko_ref, dec_ref = rest
        y_ref = o_ref = qp_ref = kp_ref = qi_ref = None
    c = CHUNK
    n_chunks = seq // c
    ri = _iota2((c, c), 0)
    ci = _iota2((c, c), 1)
    incl = (ri >= ci, ri <= ci)
    tri = tuple(jnp.where(m, 1.0, 0.0).astype(BF16) for m in incl)
    qscale = GLA_DK ** -0.5
    last = (c - 1, 0)
    kw = N_HEADS * GLA_DK
    assert c == HEAD_DIM
    r4 = _iota2((N_HEADS * c, c), 0) % c
    c4 = _iota2((N_HEADS * c, c), 1)
    incl4 = (r4 >= c4, r4 <= c4)
    lane_head = _iota2((c, kw), 1) // GLA_DK
    row_head = _iota2((kw, HEAD_DIM), 0) // GLA_DK
    zero16 = jnp.zeros((c, kw), BF16)

    def prep(n, _):
        rows = pl.ds(pl.multiple_of(n * c, c), c)
        sm = sm_ref[0, rows, :]
        kf = k_ref[0, rows, :].astype(F32)
        if with_out:
            qf = q_ref[0, rows, :].astype(F32) * qscale
        for d in (0, 1):
            z = _dot(sm[:, SM_LR[d]:SM_LR[d] + GLA_RANK], wd_ref[d]) + bd_ref[d]
            la = _log_sigmoid(z) * (1.0 / GLA_TAU)
            cum = _dot_exact_lhs(tri[d], la)
            tot = cum[last[d]:last[d] + 1, :]
            krows = pl.ds(pl.multiple_of(n * kw, kw), kw)
            dec_ref[d, krows, :] = jnp.broadcast_to(jnp.exp(tot), (HEAD_DIM, kw)).T
            if with_out:
                mid = cum[c // 2:c // 2 + 1, :]
                qp = qf * jnp.exp(cum - mid)
                kp = kf * jnp.exp(mid - cum)
                qp_ref[d, rows, :] = qp.astype(BF16)
                kp_ref[d, rows, :] = kp.astype(BF16)
                qi_ref[d, rows, :] = (qp * jnp.exp(mid)).astype(BF16)
                ko = kp * jnp.exp(tot - mid)
            else:
                ko = kf * jnp.exp(tot - cum)
            ko_ref[d, krows, :] = ko.T.astype(BF16)
        return _

    lax.fori_loop(0, n_chunks, prep, None, unroll=min(4, n_chunks))

    def step(i, _):
        chunk_of = (i, n_chunks - 1 - i)
        rows = [pl.ds(pl.multiple_of(n * c, c), c) for n in chunk_of]
        krows = [pl.ds(pl.multiple_of(n * kw, kw), kw) for n in chunk_of]
        dirs = (0, 1)
        sts = [st_ref[d] for d in dirs]
        vs = [v_ref[0, rows[d], :] for d in dirs]

        def dv(h):
            return slice(h * HEAD_DIM, (h + 1) * HEAD_DIM)

        if with_out:
            qp4 = [jnp.concatenate([jnp.where(lane_head == h, qp_ref[d, rows[d], :], zero16)
                                    for h in range(N_HEADS)], axis=0) for d in dirs]
            attn4 = [jnp.where(incl4[d], _dot_nt(qp4[d], kp_ref[d, rows[d], :]), 0.0) for d in dirs]
            s_bd = [jnp.concatenate([jnp.where(row_head == h, sts[d], 0.0).astype(BF16)
                                     for h in range(N_HEADS)], axis=1) for d in dirs]
            inter = [_dot(qi_ref[d, rows[d], :], s_bd[d]) for d in dirs]
            intra = [[_dot(attn4[d][h * c:(h + 1) * c, :], vs[d][:, dv(h)]) for h in range(N_HEADS)]
                     for d in dirs]
            for d in dirs:
                o_ref[d, rows[d], :] = inter[d] + jnp.concatenate(intra[d], axis=1)
        upd = [_dot(ko_ref[d, krows[d], :], vs[d]) for d in dirs]
        for d in dirs:
            diag = [upd[d][h * GLA_DK:(h + 1) * GLA_DK, dv(h)] for h in range(N_HEADS)]
            st_ref[d] = sts[d] * dec_ref[d, krows[d], :] + jnp.concatenate(diag, axis=0)
        return _

    st_ref[0] = s0f_ref[0]
    st_ref[1] = s0b_ref[0]
    lax.fori_loop(0, n_chunks, step, None, unroll=2)
    stf_ref[0] = st_ref[0]
    stb_ref[0] = st_ref[1]

    if with_out:
        def finish(n, _):
            rows = pl.ds(pl.multiple_of(n * c, c), c)
            for h in range(N_HEADS):
                hs = slice(h * HEAD_DIM, (h + 1) * HEAD_DIM)
                o = o_ref[0, rows, hs] + o_ref[1, rows, hs]
                o = o * lax.rsqrt(jnp.mean(o * o, axis=-1, keepdims=True) + EPS) * nw_ref[...]
                y_ref[0, rows, hs] = (o * _silu(gate_ref[0, rows, hs].astype(F32))).astype(BF16)
            return _

        lax.fori_loop(0, n_chunks, finish, None)


def _gla(big, small, wd, bd, nw, s0f, s0b, *, with_out):
    b, l, _ = big.shape
    st_shape = jax.ShapeDtypeStruct((b, N_HEADS * GLA_DK, HEAD_DIM), F32)
    st_spec = pl.BlockSpec((1, N_HEADS * GLA_DK, HEAD_DIM), lambda bi: (bi, 0, 0))
    in_specs = [pl.BlockSpec((1, l, 256), lambda bi: (bi, 0, 0)),
                pl.BlockSpec((1, l, 256), lambda bi: (bi, 0, 1)),
                pl.BlockSpec((1, l, 512), lambda bi: (bi, 0, 1)),
                pl.BlockSpec((1, l, 512), lambda bi: (bi, 0, 2)),
                pl.BlockSpec((1, l, SMALL_W), lambda bi: (bi, 0, 0)),
                pl.BlockSpec((2, GLA_RANK, 256), lambda bi: (0, 0, 0)),
                pl.BlockSpec((2, 1, 256), lambda bi: (0, 0, 0)),
                pl.BlockSpec((1, HEAD_DIM), lambda bi: (0, 0)),
                st_spec, st_spec]
    out_shape = [st_shape, st_shape]
    out_specs = [st_spec, st_spec]
    kw = N_HEADS * GLA_DK
    n_chunks = l // CHUNK
    scratch = [pltpu.VMEM((2, kw, HEAD_DIM), F32),
               pltpu.VMEM((2, n_chunks * kw, CHUNK), BF16),
               pltpu.VMEM((2, n_chunks * kw, HEAD_DIM), F32)]
    if with_out:
        out_shape = [jax.ShapeDtypeStruct((b, l, GROUP_W), BF16)] + out_shape
        out_specs = [pl.BlockSpec((1, l, GROUP_W), lambda bi: (bi, 0, 0))] + out_specs
        scratch = [pltpu.VMEM((2, l, GROUP_W), F32)] + scratch + [pltpu.VMEM((2, l, kw), BF16)] * 3
    res = pl.pallas_call(
        functools.partial(_gla_kernel, seq=l, with_out=with_out),
        out_shape=tuple(out_shape), grid=(b,), in_specs=in_specs, out_specs=tuple(out_specs),
        scratch_shapes=scratch, compiler_params=_cparams(("parallel",)),
        name="gla_out" if with_out else "gla_state",
    )(big, big, big, big, small, wd, bd.reshape(2, 1, -1), nw.reshape(1, -1), s0f, s0b)
    return res if with_out else (None,) + tuple(res)


def _unit_triangular_inverses(lows, ri, ci):
    c = lows[0].shape[0]
    eye = jnp.where(ri == ci, 1.0, 0.0)
    pair = (ri // 2) == (ci // 2)
    ts = [eye - jnp.where(pair, low, 0.0) for low in lows]
    s = 2
    while s < c:
        sel = ((ri // (2 * s)) == (ci // (2 * s))) & ((ri // s) != (ci // s))
        tb = [t.astype(BF16) for t in ts]
        ps = [_dot(t, jnp.where(sel, low, 0.0)) for t, low in zip(tb, lows)]
        ts = [t - _dot(p, t16) for t, p, t16 in zip(ts, ps, tb)]
        s *= 2
    return ts


def _gdn_kernel(qkv_ref, gate_ref, sm_ref, cw_ref, gp_ref, nw_ref, s0f_ref, s0b_ref,
                *rest, seq, with_out):
    if with_out:
        y_ref, stf_ref, stb_ref, x_ref, o_ref, st_ref, cumc_ref, cumr_ref = rest
    else:
        stf_ref, stb_ref, x_ref, st_ref, cumc_ref, cumr_ref = rest
        y_ref = o_ref = None
    c = CHUNK
    assert SMALL_W == c
    n_chunks = seq // c
    width = 3 * GROUP_W
    ri = _iota2((c, c), 0)
    ci = _iota2((c, c), 1)
    lower = ri >= ci
    upper = ri <= ci
    tril = jnp.where(lower, 1.0, 0.0).astype(BF16)
    triu = jnp.where(upper, 1.0, 0.0).astype(BF16)

    halo = 16
    off_taps = tuple(j for j in range(GDN_CONV) if j != GDN_CONV // 2)
    sr = _iota2((len(off_taps) * c, c + 2 * halo), 0)
    sc_ = _iota2((len(off_taps) * c, c + 2 * halo), 1)
    tap = jnp.zeros_like(sr)
    for idx, j in enumerate(off_taps):
        tap = jnp.where(sr // c == idx, j, tap)
    shift_mat = jnp.where(sc_ == (sr % c) + halo - GDN_CONV // 2 + tap, 1.0, 0.0).astype(BF16)

    def conv_chunk(n, _):
        r0 = pl.multiple_of(n * c, c)
        prev0 = pl.multiple_of(jnp.maximum(r0 - 16, 0), 16)
        next0 = pl.multiple_of(jnp.minimum(r0 + c, seq - 16), 16)
        prev = jnp.where(n > 0, qkv_ref[0, pl.ds(prev0, 16), :].astype(F32), 0.0).astype(BF16)
        nxt = jnp.where(n < n_chunks - 1, qkv_ref[0, pl.ds(next0, 16), :].astype(F32), 0.0).astype(BF16)
        cur = qkv_ref[0, pl.ds(r0, c), :]
        xe = jnp.concatenate([prev, cur, nxt], axis=0)
        shifted = _dot(shift_mat, xe)
        acc = cur.astype(F32) * cw_ref[GDN_CONV // 2:GDN_CONV // 2 + 1, :]
        for idx, j in enumerate(off_taps):
            acc = acc + shifted[idx * c:(idx + 1) * c, :] * cw_ref[j:j + 1, :]
        acc = _silu(acc)
        for h in range(2 * N_HEADS):
            hs = slice(h * HEAD_DIM, (h + 1) * HEAD_DIM)
            t = acc[:, hs]
            t = t * lax.rsqrt(jnp.sum(t * t, axis=-1, keepdims=True) + EPS)
            if h < N_HEADS:
                t = t * (HEAD_DIM ** -0.5)
            x_ref[pl.ds(r0, c), hs] = t
        x_ref[pl.ds(r0, c), 2 * GROUP_W:width] = acc[:, 2 * GROUP_W:width]
        g = gp_ref[0:1, :] * _softplus(sm_ref[0, pl.ds(r0, c), :] + gp_ref[1:2, :])
        gt = g.T
        for d in (0, 1):
            cumc_ref[d, pl.ds(r0, c), :] = _dot_exact_lhs(tril if d == 0 else triu, g)
            cumr_ref[d, pl.ds(r0, c), :] = _dot_exact_rhs(gt, triu if d == 0 else tril)
        return _

    lax.fori_loop(0, n_chunks, conv_chunk, None)

    chains = [(d, h) for d in (0, 1) for h in range(N_HEADS)]
    n_state = N_HEADS * HEAD_DIM

    def step(i, _):
        rows = [pl.ds(pl.multiple_of(n * c, c), c) for n in (i, n_chunks - 1 - i)]
        cums = [cumc_ref[d, rows[d], :] for d in (0, 1)]
        cum_rows = [cumr_ref[d, rows[d], :] for d in (0, 1)]
        betas = [_sigmoid(sm_ref[0, rows[d], :]) for d in (0, 1)]
        incl = (lower, upper)
        strict = (ri > ci, ri < ci)
        last = (c - 1, 0)

        kfs, vbs, kbs, e_cs, tots, decays, kos, qfs = [], [], [], [], [], [], [], []
        for d, h in chains:
            la, lb = SM_A[d] + h, SM_B[d] + h
            cum_c = cums[d][:, la:la + 1]
            cum_r = cum_rows[d][la:la + 1, :]
            tot = cums[d][last[d]:last[d] + 1, la:la + 1]
            beta_c = betas[d][:, lb:lb + 1]
            kf = x_ref[rows[d], GROUP_W + h * HEAD_DIM:GROUP_W + (h + 1) * HEAD_DIM]
            vf = x_ref[rows[d], 2 * GROUP_W + h * HEAD_DIM:2 * GROUP_W + (h + 1) * HEAD_DIM]
            e_c = jnp.exp(cum_c)
            kfs.append(kf)
            kbs.append(kf * beta_c)
            vbs.append(vf * beta_c)
            e_cs.append(e_c)
            tots.append(tot)
            decays.append(jnp.where(incl[d], jnp.exp(jnp.minimum(cum_c - cum_r, 0.0)), 0.0))
            kos.append(kf * jnp.exp(tot - cum_c))
            if with_out:
                qfs.append(x_ref[rows[d], h * HEAD_DIM:(h + 1) * HEAD_DIM])

        if with_out:
            prods = [_dot_nt(jnp.concatenate([kb, qf], axis=0), kf) for kb, qf, kf in zip(kbs, qfs, kfs)]
            attns = [p[c:, :] * dec for p, dec in zip(prods, decays)]
        else:
            prods = [_dot_nt(kb, kf) for kb, kf in zip(kbs, kfs)]
        lows = [jnp.where(strict[d], p[:c, :] * dec, 0.0) for (d, _), p, dec in zip(chains, prods, decays)]
        ts = _unit_triangular_inverses(lows, ri, ci)
        uws = [_dot(t, jnp.concatenate([vb, kb * e_c], axis=1)) for t, vb, kb, e_c in zip(ts, vbs, kbs, e_cs)]
        sts = [st_ref[j * HEAD_DIM:(j + 1) * HEAD_DIM, :] for j in range(len(chains))]
        if with_out:
            reads = [_dot(jnp.concatenate([uw[:, HEAD_DIM:], qf * e_c], axis=0), st)
                     for uw, qf, e_c, st in zip(uws, qfs, e_cs, sts)]
        else:
            reads = [_dot(uw[:, HEAD_DIM:], st) for uw, st in zip(uws, sts)]
        v_news = [uw[:, :HEAD_DIM] - r[:c, :] for uw, r in zip(uws, reads)]
        if with_out:
            writes = [_dot(jnp.concatenate([a, ko.T], axis=0), vn) for a, ko, vn in zip(attns, kos, v_news)]
        else:
            writes = [_dot_tn(ko, vn) for ko, vn in zip(kos, v_news)]
        for j, (d, h) in enumerate(chains):
            if with_out:
                o_ref[d, rows[d], h * HEAD_DIM:(h + 1) * HEAD_DIM] = reads[j][c:, :] + writes[j][:c, :]
                upd = writes[j][c:, :]
            else:
                upd = writes[j]
            st_ref[j * HEAD_DIM:(j + 1) * HEAD_DIM, :] = sts[j] * jnp.exp(tots[j]) + upd
        return _

    st_ref[0:n_state, :] = s0f_ref[0]
    st_ref[n_state:2 * n_state, :] = s0b_ref[0]
    lax.fori_loop(0, n_chunks, step, None, unroll=2)
    stf_ref[0] = st_ref[0:n_state, :]
    stb_ref[0] = st_ref[n_state:2 * n_state, :]

    if with_out:
        def finish(n, _):
            rows = pl.ds(pl.multiple_of(n * c, c), c)
            for h in range(N_HEADS):
                hs = slice(h * HEAD_DIM, (h + 1) * HEAD_DIM)
                o = o_ref[0, rows, hs] + o_ref[1, rows, hs]
                o = o * lax.rsqrt(jnp.mean(o * o, axis=-1, keepdims=True) + EPS) * nw_ref[...]
                y_ref[0, rows, hs] = (o * _silu(gate_ref[0, rows, hs].astype(F32))).astype(BF16)
            return _

        lax.fori_loop(0, n_chunks, finish, None)


def _gdn(big, small, cw, gp, nw, s0f, s0b, *, with_out):
    b, l, _ = big.shape
    st_shape = jax.ShapeDtypeStruct((b, N_HEADS * HEAD_DIM, HEAD_DIM), F32)
    st_spec = pl.BlockSpec((1, N_HEADS * HEAD_DIM, HEAD_DIM), lambda bi: (bi, 0, 0))
    in_specs = [pl.BlockSpec((1, l, 3 * GROUP_W), lambda bi: (bi, 0, 1)),
                pl.BlockSpec((1, l, GROUP_W), lambda bi: (bi, 0, 6)),
                pl.BlockSpec((1, l, SMALL_W), lambda bi: (bi, 0, 0)),
                pl.BlockSpec((GDN_CONV, 3 * GROUP_W), lambda bi: (0, 0)),
                pl.BlockSpec((8, SMALL_W), lambda bi: (0, 0)),
                pl.BlockSpec((1, HEAD_DIM), lambda bi: (0, 0)),
                st_spec, st_spec]
    out_shape = [st_shape, st_shape]
    out_specs = [st_spec, st_spec]
    scratch = [pltpu.VMEM((l, 3 * GROUP_W), F32), pltpu.VMEM((2 * N_HEADS * HEAD_DIM, HEAD_DIM), F32)]
    if with_out:
        out_shape = [jax.ShapeDtypeStruct((b, l, GROUP_W), BF16)] + out_shape
        out_specs = [pl.BlockSpec((1, l, GROUP_W), lambda bi: (bi, 0, 0))] + out_specs
        scratch = [scratch[0], pltpu.VMEM((2, l, GROUP_W), F32), scratch[1]]
    scratch += [pltpu.VMEM((2, l, SMALL_W), F32), pltpu.VMEM((2, l, CHUNK), F32)]
    res = pl.pallas_call(
        functools.partial(_gdn_kernel, seq=l, with_out=with_out),
        out_shape=tuple(out_shape), grid=(b,), in_specs=in_specs, out_specs=tuple(out_specs),
        scratch_shapes=scratch, compiler_params=_cparams(("parallel",)),
        name="gdn_out" if with_out else "gdn_state",
    )(big, big, small, cw, gp, nw.reshape(1, -1), s0f, s0b)
    return res if with_out else (None,) + tuple(res)


def _gdn_gate_params(a_log, dt_bias):
    gp = jnp.zeros((8, SMALL_W), F32)
    gp = gp.at[0, SM_A[0]:SM_A[0] + 2 * N_HEADS].set(-jnp.exp(a_log.astype(F32)).reshape(-1))
    gp = gp.at[1, SM_A[0]:SM_A[0] + 2 * N_HEADS].set(dt_bias.astype(F32).reshape(-1))
    return gp


def _rope(x, cos, sin, lane):
    swapped = jnp.where((lane % 64) < 32, pltpu.roll(x, 96, 1), pltpu.roll(x, 32, 1))
    return x * cos + swapped * sin


def _softmax_sink_pv(s, sink_col, vv):
    m = jnp.maximum(jnp.max(s, axis=-1, keepdims=True), sink_col)
    p = jnp.exp(s - m)
    den = jnp.sum(p, axis=-1, keepdims=True) + jnp.exp(sink_col - m)
    return _dot(p, vv) * (1.0 / den)


def _swa_kernel(*refs, seq, ctx_len, with_ctx):
    if with_ctx:
        (q_ref, k_ref, v_ref, gate_ref, kc_ref, vc_ref, cos_ref, sin_ref, sink_ref, qc_ref, gatec_ref,
         y_ref, yc_ref, kr_ref, bias_ref) = refs
    else:
        (q_ref, k_ref, v_ref, gate_ref, kc_ref, vc_ref, cos_ref, sin_ref, sink_ref,
         y_ref, kr_ref, bias_ref) = refs
    blk = SWA_BLOCK
    nb = seq // blk
    win = 3 * blk
    grp = N_HEADS // SWA_KV_HEADS
    scale = HEAD_DIM ** -0.5
    lane = _iota2((blk, HEAD_DIM), 1)

    def rope_k(n, _):
        rows = pl.ds(pl.multiple_of(n * blk, blk), blk)
        cos = cos_ref[rows, :]
        sin = sin_ref[rows, :]
        for g in range(SWA_KV_HEADS):
            hs = slice(g * HEAD_DIM, (g + 1) * HEAD_DIM)
            kr_ref[rows, hs] = _rope(k_ref[0, rows, hs].astype(F32), cos, sin, lane).astype(BF16)
        return _

    lax.fori_loop(0, nb, rope_k, None)

    rowq = _iota2((grp * blk, win + ctx_len), 0) % blk
    colk = _iota2((grp * blk, win + ctx_len), 1)
    is_ctx = colk >= win
    for w in range(3):
        visible = is_ctx | (jnp.abs(rowq + w * blk - colk) <= SWA_WINDOW)
        bias_ref[w] = jnp.where(visible, 0.0, NEG_BIG)

    def sink_column(g, n_rows):
        hr = _iota2((grp * n_rows, 1), 0) // n_rows
        col = jnp.zeros((grp * n_rows, 1), F32)
        for j in range(grp):
            col = jnp.where(hr == j, sink_ref[0:1, g * grp + j:g * grp + j + 1], col)
        return col

    def q_block(n, _):
        r0 = pl.multiple_of(n * blk, blk)
        rows = pl.ds(r0, blk)
        k0 = pl.multiple_of(jnp.clip(r0 - blk, 0, seq - win), blk)
        cos = cos_ref[rows, :]
        sin = sin_ref[rows, :]
        bias = bias_ref[(r0 - k0) // blk]
        for g in range(SWA_KV_HEADS):
            hs = slice(g * HEAD_DIM, (g + 1) * HEAD_DIM)
            qg = jnp.concatenate(
                [_rope(q_ref[0, rows, (g * grp + j) * HEAD_DIM:(g * grp + j + 1) * HEAD_DIM].astype(F32),
                       cos, sin, lane) * scale for j in range(grp)], axis=0).astype(BF16)
            kk = jnp.concatenate([kr_ref[pl.ds(k0, win), hs], kc_ref[0, :, hs]], axis=0)
            vv = jnp.concatenate([v_ref[0, pl.ds(k0, win), hs], vc_ref[0, :, hs]], axis=0)
            s = _dot_nt(qg, kk) + bias
            o = _softmax_sink_pv(s, sink_column(g, blk), vv)
            for j in range(grp):
                cs = slice((g * grp + j) * HEAD_DIM, (g * grp + j + 1) * HEAD_DIM)
                y_ref[0, rows, cs] = (o[j * blk:(j + 1) * blk, :]
                                      * _silu(gate_ref[0, rows, cs].astype(F32))).astype(BF16)
        return _

    lax.fori_loop(0, nb, q_block, None, unroll=2)

    if with_ctx:
        for g in range(SWA_KV_HEADS):
            hs = slice(g * HEAD_DIM, (g + 1) * HEAD_DIM)
            qg = jnp.concatenate([qc_ref[0, :, (g * grp + j) * HEAD_DIM:(g * grp + j + 1) * HEAD_DIM]
                                  for j in range(grp)], axis=0)
            s = _dot_nt(qg, kc_ref[0, :, hs]) * scale
            o = _softmax_sink_pv(s, sink_column(g, ctx_len), vc_ref[0, :, hs])
            for j in range(grp):
                cs = slice((g * grp + j) * HEAD_DIM, (g * grp + j + 1) * HEAD_DIM)
                yc_ref[0, :, cs] = (o[j * ctx_len:(j + 1) * ctx_len, :]
                                    * _silu(gatec_ref[0, :, cs].astype(F32))).astype(BF16)


def _swa(big, bigc, cos, sin, sink, *, with_ctx):
    b, l, _ = big.shape
    lc = bigc.shape[1]
    in_specs = [pl.BlockSpec((1, l, 512), lambda bi: (bi, 0, 11)),
                pl.BlockSpec((1, l, 256), lambda bi: (bi, 0, 24)),
                pl.BlockSpec((1, l, 256), lambda bi: (bi, 0, 25)),
                pl.BlockSpec((1, l, 512), lambda bi: (bi, 0, 13)),
                pl.BlockSpec((1, lc, 256), lambda bi: (bi, 0, 24)),
                pl.BlockSpec((1, lc, 256), lambda bi: (bi, 0, 25)),
                pl.BlockSpec((l, HEAD_DIM), lambda bi: (0, 0)),
                pl.BlockSpec((l, HEAD_DIM), lambda bi: (0, 0)),
                pl.BlockSpec((1, LANES), lambda bi: (0, 0))]
    args = [big, big, big, big, bigc, bigc, cos, sin, sink]
    out_shape = [jax.ShapeDtypeStruct((b, l, GROUP_W), BF16)]
    out_specs = [pl.BlockSpec((1, l, GROUP_W), lambda bi: (bi, 0, 0))]
    if with_ctx:
        in_specs += [pl.BlockSpec((1, lc, 512), lambda bi: (bi, 0, 11)),
                     pl.BlockSpec((1, lc, 512), lambda bi: (bi, 0, 13))]
        args += [bigc, bigc]
        out_shape.append(jax.ShapeDtypeStruct((b, lc, GROUP_W), BF16))
        out_specs.append(pl.BlockSpec((1, lc, GROUP_W), lambda bi: (bi, 0, 0)))
    res = pl.pallas_call(
        functools.partial(_swa_kernel, seq=l, ctx_len=lc, with_ctx=with_ctx),
        out_shape=tuple(out_shape), grid=(b,), in_specs=in_specs, out_specs=tuple(out_specs),
        scratch_shapes=[pltpu.VMEM((l, SWA_KV_HEADS * HEAD_DIM), BF16),
                        pltpu.VMEM((3, (N_HEADS // SWA_KV_HEADS) * SWA_BLOCK, 3 * SWA_BLOCK + lc), F32)],
        compiler_params=_cparams(("parallel",)),
        name="swa_ctx" if with_ctx else "swa",
    )(*args)
    return (res[0], res[1]) if with_ctx else (res[0], None)


def _rope_tables(seq_len):
    rows = seq_len // GRID_W
    row = jnp.repeat(jnp.arange(rows, dtype=F32), GRID_W)
    col = jnp.tile(jnp.arange(GRID_W, dtype=F32), rows)
    axis_dim = HEAD_DIM // 2
    inv_freq = ROPE_BASE ** (-jnp.arange(0, axis_dim, 2, dtype=F32) / axis_dim)
    ang_r = row[:, None] * inv_freq
    ang_c = col[:, None] * inv_freq
    cos = jnp.concatenate([jnp.cos(ang_r), jnp.cos(ang_r), jnp.cos(ang_c), jnp.cos(ang_c)], axis=-1)
    sin = jnp.concatenate([-jnp.sin(ang_r), jnp.sin(ang_r), -jnp.sin(ang_c), jnp.sin(ang_c)], axis=-1)
    return cos, sin


_PACK_RUNS = ((0, 0, 1024), (1024, 1056, 2048), (3072, 3120, 4096))
_LR_COL0 = 1024
_GDN_GATE_COL0 = 3104


def _pack_kernel(wt_ref, big_ref, small_ref):
    step = 512
    for dst, src, width in _PACK_RUNS:
        for o in range(0, width, step):
            big_ref[0, :, dst + o:dst + o + step] = wt_ref[0, src + o:src + o + step, :].T.astype(BF16)
    assert _LR_COL0 % LANES == 0 and _GDN_GATE_COL0 % LANES == SM_A[0]
    gate_win0 = _GDN_GATE_COL0 - SM_A[0]
    lane = _iota2((wt_ref.shape[2], LANES), 1)
    lr_win = wt_ref[0, _LR_COL0:_LR_COL0 + LANES, :].T
    gate_win = wt_ref[0, gate_win0:gate_win0 + LANES, :].T
    small = jnp.where(lane < SM_A[0], lr_win, jnp.where(lane < SM_B[1] + N_HEADS, gate_win, 0.0))
    small_ref[0] = small.astype(BF16)


def _pack_w_in(w_in):
    n_layers, d, n = w_in.shape
    kb = PACK_KB
    return pl.pallas_call(
        _pack_kernel,
        out_shape=(jax.ShapeDtypeStruct((n_layers, d, BIG_W), BF16),
                   jax.ShapeDtypeStruct((n_layers, d, SMALL_W), BF16)),
        grid=(n_layers, d // kb),
        in_specs=[pl.BlockSpec((1, n, kb), lambda l, i: (l, 0, i))],
        out_specs=(pl.BlockSpec((1, kb, BIG_W), lambda l, i: (l, i, 0)),
                   pl.BlockSpec((1, kb, SMALL_W), lambda l, i: (l, i, 0))),
        compiler_params=_cparams(("parallel", "parallel")),
        name="pack_w_in",
    )(jnp.swapaxes(w_in, 1, 2))


def kernel(x, c, ctx, c_ctx, ada_w, ada_b, norm_pre, norm_post, w_in, w_out, gla_w_decay, gla_b_decay,
           gla_norm, gdn_conv, gdn_a_log, gdn_dt_bias, gdn_norm, sc_conv, swa_sink):
    b, l, d = x.shape
    cos, sin = _rope_tables(l)
    cc = jnp.zeros((8, d), F32).at[:b].set(c).at[b].set(c_ctx)
    mod = _ada(cc, ada_w, ada_b)
    h, hc = x, ctx
    lc = ctx.shape[1]
    n_layers = ada_w.shape[0]
    w_big, w_small = _pack_w_in(w_in)
    w_o = w_out.astype(BF16)
    for layer in range(n_layers):
        with_ctx = layer < n_layers - 1
        shift, scale, gate = (mod[layer, :, i * d:(i + 1) * d] for i in range(3))

        def per_batch(t):
            return t[:b, None, :], t[b][None, None, :]
        (shift_l, shift_c), (scale_l, scale_c), (gate_l, gate_c) = per_batch(shift), per_batch(scale), per_batch(gate)
        g_pre = norm_pre[layer].reshape(1, d)
        big, small = _inproj(h, scale_l, shift_l, g_pre, w_big, w_small, layer)
        bigc, smallc = _inproj(hc.reshape(1, b * lc, d), scale_c, shift_c, g_pre, w_big, w_small, layer)
        bigc, smallc = bigc.reshape(b, lc, BIG_W), smallc.reshape(b, lc, SMALL_W)

        zf = jnp.zeros((b, N_HEADS * GLA_DK, HEAD_DIM), F32)
        yca, stf, stb = _gla(bigc, smallc, gla_w_decay[layer], gla_b_decay[layer], gla_norm[layer], zf, zf,
                             with_out=with_ctx)
        ya, _, _ = _gla(big, small, gla_w_decay[layer], gla_b_decay[layer], gla_norm[layer], stf, stb,
                        with_out=True)

        gp = _gdn_gate_params(gdn_a_log[layer], gdn_dt_bias[layer])
        zg = jnp.zeros((b, N_HEADS * HEAD_DIM, HEAD_DIM), F32)
        ycb, gtf, gtb = _gdn(bigc, smallc, gdn_conv[layer], gp, gdn_norm[layer], zg, zg, with_out=with_ctx)
        yb, _, _ = _gdn(big, small, gdn_conv[layer], gp, gdn_norm[layer], gtf, gtb, with_out=True)

        sink = jnp.zeros((1, LANES), F32).at[0, :N_HEADS].set(swa_sink[layer])
        yd, ycd = _swa(big, bigc, cos, sin, sink, with_ctx=with_ctx)

        g_post = norm_post[layer].reshape(1, d)
        h_new = _outproj_sconv(ya, yb, yd, big, sc_conv[layer], w_o, layer, h, gate_l, g_post)
        if with_ctx:
            ycc = _sconv(bigc, sc_conv[layer])
            flat = [t.reshape(1, b * lc, GROUP_W) for t in (yca, ycb, ycc, ycd)]
            hc = _outproj(*flat, w_o, layer, hc.reshape(1, b * lc, d), gate_c, g_post).reshape(b, lc, d)
        h = h_new
    return h
```

```python
import functools

import jax
import jax.numpy as jnp
from jax import lax
from jax.experimental import pallas as pl
from jax.experimental.pallas import tpu as pltpu

F32 = jnp.float32
BF16 = jnp.bfloat16

D_MODEL = 2048
N_LAYERS = 2
GRID_W = 64
GROUP_W = 512
HEAD_DIM = 128
N_HEADS = 4
EPS = 1e-6
GLA_DK = 64
GLA_RANK = 16
GLA_TAU = 16.0
GDN_CONV = 5
SC_CONV = 3
SWA_KV_HEADS = 2
SWA_WINDOW = 128
ROPE_BASE = 10000.0

_SPLITS = (
    ("gla_q", 256), ("gla_k", 256), ("gla_v", 512), ("gla_lr_f", 16), ("gla_lr_b", 16), ("gla_gate", 512),
    ("gdn_q", 512), ("gdn_k", 512), ("gdn_v", 512),
    ("gdn_a_f", 4), ("gdn_a_b", 4), ("gdn_b_f", 4), ("gdn_b_b", 4), ("gdn_gate", 512),
    ("sc_b", 512), ("sc_c", 512), ("sc_h", 512), ("sc_gate", 512),
    ("swa_q", 512), ("swa_k", 256), ("swa_v", 256), ("swa_gate", 512),
)
BIG_W = 7168
SMALL_W = 128
SM_LR = (0, 16)
SM_A = (32, 36)
SM_B = (40, 44)

LANES = 128
VMEM_LIMIT = 56 * 1024 * 1024

CHUNK = 128
SWA_BLOCK = 128
INPROJ_TM = 1024
INPROJ_TN = 1024
OUTPROJ_TM = 512
ADA_TN = 1024
PACK_KB = 256

NEG_BIG = -1e30


def _cparams(sem):
    return pltpu.CompilerParams(dimension_semantics=sem, vmem_limit_bytes=VMEM_LIMIT)


def _dot(a, b):
    return lax.dot_general(a.astype(BF16), b.astype(BF16), (((1,), (0,)), ((), ())),
                           preferred_element_type=F32)


def _dot_nt(a, b):
    return lax.dot_general(a.astype(BF16), b.astype(BF16), (((1,), (1,)), ((), ())),
                           preferred_element_type=F32)


def _dot_tn(a, b):
    return lax.dot_general(a.astype(BF16), b.astype(BF16), (((0,), (0,)), ((), ())),
                           preferred_element_type=F32)


def _split(x):
    hi = x.astype(BF16)
    lo = (x - hi.astype(F32)).astype(BF16)
    return hi, lo


def _dot_exact_lhs(a_bf16, x):
    hi, lo = _split(x)
    return _dot(a_bf16, hi) + _dot(a_bf16, lo)


def _dot_exact_rhs(x, b_bf16):
    hi, lo = _split(x)
    return _dot(hi, b_bf16) + _dot(lo, b_bf16)


def _sigmoid(x):
    return 0.5 * jnp.tanh(0.5 * x) + 0.5


def _silu(x):
    return x * _sigmoid(x)


def _softplus(x):
    return jnp.maximum(x, 0.0) + jnp.log(1.0 + jnp.exp(-jnp.abs(x)))


def _log_sigmoid(x):
    return jnp.minimum(x, 0.0) - jnp.log(1.0 + jnp.exp(-jnp.abs(x)))


def _iota2(shape, dim):
    return lax.broadcasted_iota(jnp.int32, shape, dim)


ADA_K_SPLIT = 4


def _ada_kernel(c_ref, *refs):
    w_refs, (b_ref, o_ref) = refs[:ADA_K_SPLIT], refs[ADA_K_SPLIT:]
    a = _silu(c_ref[...])
    kq = a.shape[1] // ADA_K_SPLIT
    acc = b_ref[0]
    for q, w_ref in enumerate(w_refs):
        acc = acc + _dot(a[:, q * kq:(q + 1) * kq], w_ref[0])
    o_ref[0] = acc


def _ada(cc, ada_w, ada_b):
    n_layers, d, n = ada_w.shape
    tn = ADA_TN
    kq = d // ADA_K_SPLIT
    w_specs = [pl.BlockSpec((1, kq, tn), lambda l, j, q=q: (l, q, j)) for q in range(ADA_K_SPLIT)]
    return pl.pallas_call(
        _ada_kernel,
        out_shape=jax.ShapeDtypeStruct((n_layers, 8, n), F32),
        grid=(n_layers, n // tn),
        in_specs=[pl.BlockSpec((8, d), lambda l, j: (0, 0))] + w_specs
        + [pl.BlockSpec((1, 1, tn), lambda l, j: (l, 0, j))],
        out_specs=pl.BlockSpec((1, 8, tn), lambda l, j: (l, 0, j)),
        compiler_params=_cparams(("parallel", "parallel")),
        name="ada_mod",
    )(cc, *([ada_w] * ADA_K_SPLIT), ada_b.reshape(n_layers, 1, n))


def _inproj_kernel(x_ref, sc_ref, sh_ref, g_ref, wb_ref, ws_ref, ob_ref, os_ref, xn_ref):
    @pl.when(pl.program_id(2) == 0)
    def _():
        rows_per_step = 128
        gain = g_ref[...] * (1.0 + sc_ref[0])

        def norm_rows(i, _):
            rows = pl.ds(pl.multiple_of(i * rows_per_step, rows_per_step), rows_per_step)
            x = x_ref[0, rows, :]
            y = x * lax.rsqrt(jnp.mean(x * x, axis=-1, keepdims=True) + EPS)
            xn_ref[rows, :] = (y * gain + sh_ref[0]).astype(BF16)
            return _

        lax.fori_loop(0, xn_ref.shape[0] // rows_per_step, norm_rows, None)
        os_ref[0] = jnp.dot(xn_ref[...], ws_ref[0], preferred_element_type=F32)

    ob_ref[0] = jnp.dot(xn_ref[...], wb_ref[0], preferred_element_type=F32).astype(BF16)


def _inproj(x, scale, shift, g, w_big, w_small, layer):
    b, l, d = x.shape
    tm, tn = min(l, INPROJ_TM), INPROJ_TN
    return pl.pallas_call(
        _inproj_kernel,
        out_shape=(jax.ShapeDtypeStruct((b, l, BIG_W), BF16), jax.ShapeDtypeStruct((b, l, SMALL_W), F32)),
        grid=(b, l // tm, BIG_W // tn),
        in_specs=[pl.BlockSpec((1, tm, d), lambda bi, i, j: (bi, i, 0)),
                  pl.BlockSpec((1, 1, d), lambda bi, i, j: (bi, 0, 0)),
                  pl.BlockSpec((1, 1, d), lambda bi, i, j: (bi, 0, 0)),
                  pl.BlockSpec((1, d), lambda bi, i, j: (0, 0)),
                  pl.BlockSpec((1, d, tn), lambda bi, i, j: (layer, 0, j)),
                  pl.BlockSpec((1, d, SMALL_W), lambda bi, i, j: (layer, 0, 0))],
        out_specs=(pl.BlockSpec((1, tm, tn), lambda bi, i, j: (bi, i, j)),
                   pl.BlockSpec((1, tm, SMALL_W), lambda bi, i, j: (bi, i, 0))),
        scratch_shapes=[pltpu.VMEM((tm, d), BF16)],
        compiler_params=_cparams(("parallel", "parallel", "arbitrary")),
        name="inproj",
    )(x, scale, shift, g, w_big, w_small)


def _group_dot(group, k, w_ref):
    return jnp.dot(group, w_ref[0, k * GROUP_W:(k + 1) * GROUP_W, :], preferred_element_type=F32)


def _outproj_tail(y, hl_ref, hr_ref, gate_ref, g_ref, o_ref):
    yn = gate_ref[0] * (y * lax.rsqrt(jnp.mean(y * y, axis=-1, keepdims=True) + EPS) * g_ref[...])
    half = hl_ref.shape[2]
    o_ref[0, :, 0:half] = hl_ref[0] + yn[:, 0:half]
    o_ref[0, :, half:2 * half] = hr_ref[0] + yn[:, half:2 * half]


def _outproj_kernel(ya_ref, yb_ref, yc_ref, yd_ref, w_ref, hl_ref, hr_ref, gate_ref, g_ref, o_ref):
    y = _group_dot(ya_ref[0], 0, w_ref)
    for k, ref in ((1, yb_ref), (2, yc_ref), (3, yd_ref)):
        y += _group_dot(ref[0], k, w_ref)
    _outproj_tail(y, hl_ref, hr_ref, gate_ref, g_ref, o_ref)


SC_HALO = 16


def _outproj_sconv_kernel(ya_ref, yb_ref, yd_ref, sb_ref, sc_ref, sh_ref, sg_ref, cp_ref, hp_ref, cn_ref, hn_ref,
                          cw_ref, w_ref, hl_ref, hr_ref, gate_ref, g_ref, o_ref):
    y = _group_dot(ya_ref[0], 0, w_ref) + _group_dot(yb_ref[0], 1, w_ref) + _group_dot(yd_ref[0], 3, w_ref)
    i = pl.program_id(1)
    tm = sc_ref.shape[1]
    p = sc_ref[0].astype(F32) * sh_ref[0].astype(F32)
    p_prev = jnp.where(i > 0, cp_ref[0].astype(F32) * hp_ref[0].astype(F32), 0.0)
    p_next = jnp.where(i < pl.num_programs(1) - 1, cn_ref[0].astype(F32) * hn_ref[0].astype(F32), 0.0)
    pe = jnp.concatenate([p_prev, p, p_next], axis=0)
    conv = (pe[SC_HALO - 1:SC_HALO - 1 + tm, :] * cw_ref[0:1, :] + p * cw_ref[1:2, :]
            + pe[SC_HALO + 1:SC_HALO + 1 + tm, :] * cw_ref[2:3, :])
    yc = (sb_ref[0].astype(F32) * conv * _silu(sg_ref[0].astype(F32))).astype(BF16)
    _outproj_tail(y + _group_dot(yc, 2, w_ref), hl_ref, hr_ref, gate_ref, g_ref, o_ref)


def _outproj_sconv(ya, yb, yd, big, cw, w_out, layer, h, gate, g):
    b, l, d = h.shape
    tm = min(l, OUTPROJ_TM)
    assert SC_CONV == 3 and tm % SC_HALO == 0
    col0 = 3584 // GROUP_W
    per = tm // SC_HALO
    n_halo = l // SC_HALO
    yspec = pl.BlockSpec((1, tm, GROUP_W), lambda bi, i: (bi, i, 0))

    def col(k):
        return pl.BlockSpec((1, tm, GROUP_W), lambda bi, i, k=k: (bi, i, col0 + k))

    def prev(k):
        return pl.BlockSpec((1, SC_HALO, GROUP_W), lambda bi, i, k=k: (bi, jnp.maximum(i * per - 1, 0), col0 + k))

    def nxt(k):
        return pl.BlockSpec((1, SC_HALO, GROUP_W),
                            lambda bi, i, k=k: (bi, jnp.minimum((i + 1) * per, n_halo - 1), col0 + k))
    return pl.pallas_call(
        _outproj_sconv_kernel,
        out_shape=jax.ShapeDtypeStruct((b, l, d), F32),
        grid=(b, l // tm),
        in_specs=[yspec, yspec, yspec, col(0), col(1), col(2), col(3), prev(1), prev(2), nxt(1), nxt(2),
                  pl.BlockSpec((SC_CONV, GROUP_W), lambda bi, i: (0, 0)),
                  pl.BlockSpec((1, d, d), lambda bi, i: (layer, 0, 0)),
                  pl.BlockSpec((1, tm, d // 2), lambda bi, i: (bi, i, 0)),
                  pl.BlockSpec((1, tm, d // 2), lambda bi, i: (bi, i, 1)),
                  pl.BlockSpec((1, 1, d), lambda bi, i: (bi, 0, 0)),
                  pl.BlockSpec((1, d), lambda bi, i: (0, 0))],
        out_specs=pl.BlockSpec((1, tm, d), lambda bi, i: (bi, i, 0)),
        compiler_params=_cparams(("parallel", "parallel")),
        name="outproj_sconv",
    )(ya, yb, yd, big, big, big, big, big, big, big, big, cw, w_out, h, h, gate, g)


def _outproj(ya, yb, yc, yd, w_out, layer, h, gate, g):
    b, l, d = h.shape
    tm = min(l, OUTPROJ_TM)
    yspec = pl.BlockSpec((1, tm, GROUP_W), lambda bi, i: (bi, i, 0))
    return pl.pallas_call(
        _outproj_kernel,
        out_shape=jax.ShapeDtypeStruct((b, l, d), F32),
        grid=(b, l // tm),
        in_specs=[yspec, yspec, yspec, yspec,
                  pl.BlockSpec((1, d, d), lambda bi, i: (layer, 0, 0)),
                  pl.BlockSpec((1, tm, d // 2), lambda bi, i: (bi, i, 0)),
                  pl.BlockSpec((1, tm, d // 2), lambda bi, i: (bi, i, 1)),
                  pl.BlockSpec((1, 1, d), lambda bi, i: (bi, 0, 0)),
                  pl.BlockSpec((1, d), lambda bi, i: (0, 0))],
        out_specs=pl.BlockSpec((1, tm, d), lambda bi, i: (bi, i, 0)),
        compiler_params=_cparams(("parallel", "parallel")),
        name="outproj",
    )(ya, yb, yc, yd, w_out, h, h, gate, g)


def _sconv_kernel(b_ref, c_ref, h_ref, gate_ref, w_ref, o_ref, p_ref, *, seq):
    zeros = jnp.zeros((8, GROUP_W), F32)
    p_ref[0:8, :] = zeros
    p_ref[seq + 8:seq + 16, :] = zeros
    p_ref[8:seq + 8, :] = c_ref[0].astype(F32) * h_ref[0].astype(F32)
    rows = min(seq, 256)
    w = w_ref[...]
    for t0 in range(0, seq, rows):
        conv = (p_ref[t0 + 7:t0 + 7 + rows, :] * w[0:1, :] + p_ref[t0 + 8:t0 + 8 + rows, :] * w[1:2, :]
                + p_ref[t0 + 9:t0 + 9 + rows, :] * w[2:3, :])
        y = b_ref[0, t0:t0 + rows, :].astype(F32) * conv * _silu(gate_ref[0, t0:t0 + rows, :].astype(F32))
        o_ref[0, t0:t0 + rows, :] = y.astype(BF16)


def _sconv(big, w):
    b, l, _ = big.shape
    col0 = 3584 // GROUP_W

    def spec(k):
        return pl.BlockSpec((1, l, GROUP_W), lambda bi, k=k: (bi, 0, col0 + k))
    return pl.pallas_call(
        functools.partial(_sconv_kernel, seq=l),
        out_shape=jax.ShapeDtypeStruct((b, l, GROUP_W), BF16),
        grid=(b,),
        in_specs=[spec(0), spec(1), spec(2), spec(3), pl.BlockSpec((SC_CONV, GROUP_W), lambda bi: (0, 0))],
        out_specs=pl.BlockSpec((1, l, GROUP_W), lambda bi: (bi, 0, 0)),
        scratch_shapes=[pltpu.VMEM((l + 16, GROUP_W), F32)],
        compiler_params=_cparams(("parallel",)),
        name="sconv",
    )(big, big, big, big, w)


def _gla_kernel(q_ref, k_ref, v_ref, gate_ref, sm_ref, wd_ref, bd_ref, nw_ref, s0f_ref, s0b_ref,
                *rest, seq, with_out):
    if with_out:
        y_ref, stf_ref, stb_ref, o_ref, st_ref, ko_ref, dec_ref, qp_ref, kp_ref, qi_ref = rest
    else:
        stf_ref, stb_ref, st_ref, ko_ref, dec_ref = rest
        y_ref = o_ref = qp_ref = kp_ref = qi_ref = None
    c = CHUNK
    n_chunks = seq // c
    ri = _iota2((c, c), 0)
    ci = _iota2((c, c), 1)
    incl = (ri >= ci, ri <= ci)
    tri = tuple(jnp.where(m, 1.0, 0.0).astype(BF16) for m in incl)
    qscale = GLA_DK ** -0.5
    last = (c - 1, 0)
    kw = N_HEADS * GLA_DK
    assert c == HEAD_DIM
    r4 = _iota2((N_HEADS * c, c), 0) % c
    c4 = _iota2((N_HEADS * c, c), 1)
    incl4 = (r4 >= c4, r4 <= c4)
    lane_head = _iota2((c, kw), 1) // GLA_DK
    row_head = _iota2((kw, HEAD_DIM), 0) // GLA_DK
    zero16 = jnp.zeros((c, kw), BF16)

    def prep(n, _):
        rows = pl.ds(pl.multiple_of(n * c, c), c)
        sm = sm_ref[0, rows, :]
        kf = k_ref[0, rows, :].astype(F32)
        if with_out:
            qf = q_ref[0, rows, :].astype(F32) * qscale
        for d in (0, 1):
            z = _dot(sm[:, SM_LR[d]:SM_LR[d] + GLA_RANK], wd_ref[d]) + bd_ref[d]
            la = _log_sigmoid(z) * (1.0 / GLA_TAU)
            cum = _dot_exact_lhs(tri[d], la)
            tot = cum[last[d]:last[d] + 1, :]
            krows = pl.ds(pl.multiple_of(n * kw, kw), kw)
            dec_ref[d, krows, :] = jnp.broadcast_to(jnp.exp(tot), (HEAD_DIM, kw)).T
            if with_out:
                mid = cum[c // 2:c // 2 + 1, :]
                qp = qf * jnp.exp(cum - mid)
                kp = kf * jnp.exp(mid - cum)
                qp_ref[d, rows, :] = qp.astype(BF16)
                kp_ref[d, rows, :] = kp.astype(BF16)
                qi_ref[d, rows, :] = (qp * jnp.exp(mid)).astype(BF16)
                ko = kp * jnp.exp(tot - mid)
            else:
                ko = kf * jnp.exp(tot - cum)
            ko_ref[d, krows, :] = ko.T.astype(BF16)
        return _

    lax.fori_loop(0, n_chunks, prep, None, unroll=min(4, n_chunks))

    def step(i, _, second_half):
        chunk_of = (i, n_chunks - 1 - i)
        rows = [pl.ds(pl.multiple_of(n * c, c), c) for n in chunk_of]
        krows = [pl.ds(pl.multiple_of(n * kw, kw), kw) for n in chunk_of]
        dirs = (0, 1)
        sts = [st_ref[d] for d in dirs]
        vs = [v_ref[0, rows[d], :] for d in dirs]

        def dv(h):
            return slice(h * HEAD_DIM, (h + 1) * HEAD_DIM)

        if with_out:
            qp4 = [jnp.concatenate([jnp.where(lane_head == h, qp_ref[d, rows[d], :], zero16)
                                    for h in range(N_HEADS)], axis=0) for d in dirs]
            attn4 = [jnp.where(incl4[d], _dot_nt(qp4[d], kp_ref[d, rows[d], :]), 0.0) for d in dirs]
            s_bd = [jnp.concatenate([jnp.where(row_head == h, sts[d], 0.0).astype(BF16)
                                     for h in range(N_HEADS)], axis=1) for d in dirs]
            inter = [_dot(qi_ref[d, rows[d], :], s_bd[d]) for d in dirs]
            intra = [[_dot(attn4[d][h * c:(h + 1) * c, :], vs[d][:, dv(h)]) for h in range(N_HEADS)]
                     for d in dirs]
            for d in dirs:
                o = inter[d] + jnp.concatenate(intra[d], axis=1)
                if not second_half:
                    o_ref[d, rows[d], :] = o
                    continue
                o = o + o_ref[1 - d, rows[d], :]
                for h in range(N_HEADS):
                    oh = o[:, dv(h)]
                    oh = oh * lax.rsqrt(jnp.mean(oh * oh, axis=-1, keepdims=True) + EPS) * nw_ref[...]
                    y_ref[0, rows[d], dv(h)] = (oh * _silu(gate_ref[0, rows[d], dv(h)].astype(F32))).astype(BF16)
        upd = [_dot(ko_ref[d, krows[d], :], vs[d]) for d in dirs]
        for d in dirs:
            diag = [upd[d][h * GLA_DK:(h + 1) * GLA_DK, dv(h)] for h in range(N_HEADS)]
            st_ref[d] = sts[d] * dec_ref[d, krows[d], :] + jnp.concatenate(diag, axis=0)
        return _

    assert n_chunks % 2 == 0
    half = n_chunks // 2
    st_ref[0] = s0f_ref[0]
    st_ref[1] = s0b_ref[0]
    lax.fori_loop(0, half, functools.partial(step, second_half=False), None, unroll=min(4, half))
    lax.fori_loop(half, n_chunks, functools.partial(step, second_half=True), None, unroll=min(4, half))
    stf_ref[0] = st_ref[0]
    stb_ref[0] = st_ref[1]


def _gla(big, small, wd, bd, nw, s0f, s0b, *, with_out):
    b, l, _ = big.shape
    st_shape = jax.ShapeDtypeStruct((b, N_HEADS * GLA_DK, HEAD_DIM), F32)
    st_spec = pl.BlockSpec((1, N_HEADS * GLA_DK, HEAD_DIM), lambda bi: (bi, 0, 0))
    in_specs = [pl.BlockSpec((1, l, 256), lambda bi: (bi, 0, 0)),
                pl.BlockSpec((1, l, 256), lambda bi: (bi, 0, 1)),
                pl.BlockSpec((1, l, 512), lambda bi: (bi, 0, 1)),
                pl.BlockSpec((1, l, 512), lambda bi: (bi, 0, 2)),
                pl.BlockSpec((1, l, SMALL_W), lambda bi: (bi, 0, 0)),
                pl.BlockSpec((2, GLA_RANK, 256), lambda bi: (0, 0, 0)),
                pl.BlockSpec((2, 1, 256), lambda bi: (0, 0, 0)),
                pl.BlockSpec((1, HEAD_DIM), lambda bi: (0, 0)),
                st_spec, st_spec]
    out_shape = [st_shape, st_shape]
    out_specs = [st_spec, st_spec]
    kw = N_HEADS * GLA_DK
    n_chunks = l // CHUNK
    scratch = [pltpu.VMEM((2, kw, HEAD_DIM), F32),
               pltpu.VMEM((2, n_chunks * kw, CHUNK), BF16),
               pltpu.VMEM((2, n_chunks * kw, HEAD_DIM), F32)]
    if with_out:
        out_shape = [jax.ShapeDtypeStruct((b, l, GROUP_W), BF16)] + out_shape
        out_specs = [pl.BlockSpec((1, l, GROUP_W), lambda bi: (bi, 0, 0))] + out_specs
        scratch = [pltpu.VMEM((2, l, GROUP_W), F32)] + scratch + [pltpu.VMEM((2, l, kw), BF16)] * 3
    res = pl.pallas_call(
        functools.partial(_gla_kernel, seq=l, with_out=with_out),
        out_shape=tuple(out_shape), grid=(b,), in_specs=in_specs, out_specs=tuple(out_specs),
        scratch_shapes=scratch, compiler_params=_cparams(("parallel",)),
        name="gla_out" if with_out else "gla_state",
    )(big, big, big, big, small, wd, bd.reshape(2, 1, -1), nw.reshape(1, -1), s0f, s0b)
    return res if with_out else (None,) + tuple(res)


def _unit_triangular_inverses(lows, ri, ci):
    c = lows[0].shape[0]
    eye = jnp.where(ri == ci, 1.0, 0.0)
    pair = (ri // 2) == (ci // 2)
    ts = [eye - jnp.where(pair, low, 0.0) for low in lows]
    s = 2
    while s < c:
        sel = ((ri // (2 * s)) == (ci // (2 * s))) & ((ri // s) != (ci // s))
        tb = [t.astype(BF16) for t in ts]
        ps = [_dot(t, jnp.where(sel, low, 0.0)) for t, low in zip(tb, lows)]
        ts = [t - _dot(p, t16) for t, p, t16 in zip(ts, ps, tb)]
        s *= 2
    return ts


def _gdn_kernel(qkv_ref, gate_ref, sm_ref, cw_ref, gp_ref, nw_ref, s0f_ref, s0b_ref,
                *rest, seq, with_out):
    if with_out:
        y_ref, stf_ref, stb_ref, x_ref, o_ref, st_ref, cumc_ref, cumr_ref = rest
    else:
        stf_ref, stb_ref, x_ref, st_ref, cumc_ref, cumr_ref = rest
        y_ref = o_ref = None
    c = CHUNK
    assert SMALL_W == c
    n_chunks = seq // c
    width = 3 * GROUP_W
    ri = _iota2((c, c), 0)
    ci = _iota2((c, c), 1)
    lower = ri >= ci
    upper = ri <= ci
    tril = jnp.where(lower, 1.0, 0.0).astype(BF16)
    triu = jnp.where(upper, 1.0, 0.0).astype(BF16)

    halo = 16
    off_taps = tuple(j for j in range(GDN_CONV) if j != GDN_CONV // 2)
    sr = _iota2((len(off_taps) * c, c + 2 * halo), 0)
    sc_ = _iota2((len(off_taps) * c, c + 2 * halo), 1)
    tap = jnp.zeros_like(sr)
    for idx, j in enumerate(off_taps):
        tap = jnp.where(sr // c == idx, j, tap)
    shift_mat = jnp.where(sc_ == (sr % c) + halo - GDN_CONV // 2 + tap, 1.0, 0.0).astype(BF16)

    def conv_chunk(n, _):
        r0 = pl.multiple_of(n * c, c)
        prev0 = pl.multiple_of(jnp.maximum(r0 - 16, 0), 16)
        next0 = pl.multiple_of(jnp.minimum(r0 + c, seq - 16), 16)
        prev = jnp.where(n > 0, qkv_ref[0, pl.ds(prev0, 16), :].astype(F32), 0.0).astype(BF16)
        nxt = jnp.where(n < n_chunks - 1, qkv_ref[0, pl.ds(next0, 16), :].astype(F32), 0.0).astype(BF16)
        cur = qkv_ref[0, pl.ds(r0, c), :]
        xe = jnp.concatenate([prev, cur, nxt], axis=0)
        shifted = _dot(shift_mat, xe)
        acc = cur.astype(F32) * cw_ref[GDN_CONV // 2:GDN_CONV // 2 + 1, :]
        for idx, j in enumerate(off_taps):
            acc = acc + shifted[idx * c:(idx + 1) * c, :] * cw_ref[j:j + 1, :]
        acc = _silu(acc)
        for h in range(2 * N_HEADS):
            hs = slice(h * HEAD_DIM, (h + 1) * HEAD_DIM)
            t = acc[:, hs]
            t = t * lax.rsqrt(jnp.sum(t * t, axis=-1, keepdims=True) + EPS)
            if h < N_HEADS:
                t = t * (HEAD_DIM ** -0.5)
            x_ref[pl.ds(r0, c), hs] = t
        x_ref[pl.ds(r0, c), 2 * GROUP_W:width] = acc[:, 2 * GROUP_W:width]
        g = gp_ref[0:1, :] * _softplus(sm_ref[0, pl.ds(r0, c), :] + gp_ref[1:2, :])
        gt = g.T
        for d in (0, 1):
            cumc_ref[d, pl.ds(r0, c), :] = _dot_exact_lhs(tril if d == 0 else triu, g)
            cumr_ref[d, pl.ds(r0, c), :] = _dot_exact_rhs(gt, triu if d == 0 else tril)
        return _

    lax.fori_loop(0, n_chunks, conv_chunk, None)

    chains = [(d, h) for d in (0, 1) for h in range(N_HEADS)]
    n_state = N_HEADS * HEAD_DIM

    def step(i, _, second_half):
        rows = [pl.ds(pl.multiple_of(n * c, c), c) for n in (i, n_chunks - 1 - i)]
        cums = [cumc_ref[d, rows[d], :] for d in (0, 1)]
        cum_rows = [cumr_ref[d, rows[d], :] for d in (0, 1)]
        betas = [_sigmoid(sm_ref[0, rows[d], :]) for d in (0, 1)]
        incl = (lower, upper)
        strict = (ri > ci, ri < ci)
        last = (c - 1, 0)

        kfs, vbs, kbs, e_cs, tots, decays, kos, qfs = [], [], [], [], [], [], [], []
        for d, h in chains:
            la, lb = SM_A[d] + h, SM_B[d] + h
            cum_c = cums[d][:, la:la + 1]
            cum_r = cum_rows[d][la:la + 1, :]
            tot = cums[d][last[d]:last[d] + 1, la:la + 1]
            beta_c = betas[d][:, lb:lb + 1]
            kf = x_ref[rows[d], GROUP_W + h * HEAD_DIM:GROUP_W + (h + 1) * HEAD_DIM]
            vf = x_ref[rows[d], 2 * GROUP_W + h * HEAD_DIM:2 * GROUP_W + (h + 1) * HEAD_DIM]
            e_c = jnp.exp(cum_c)
            kfs.append(kf)
            kbs.append(kf * beta_c)
            vbs.append(vf * beta_c)
            e_cs.append(e_c)
            tots.append(tot)
            decays.append(jnp.where(incl[d], jnp.exp(jnp.minimum(cum_c - cum_r, 0.0)), 0.0))
            kos.append(kf * jnp.exp(tot - cum_c))
            if with_out:
                qfs.append(x_ref[rows[d], h * HEAD_DIM:(h + 1) * HEAD_DIM])

        if with_out:
            prods = [_dot_nt(jnp.concatenate([kb, qf], axis=0), kf) for kb, qf, kf in zip(kbs, qfs, kfs)]
            attns = [p[c:, :] * dec for p, dec in zip(prods, decays)]
        else:
            prods = [_dot_nt(kb, kf) for kb, kf in zip(kbs, kfs)]
        lows = [jnp.where(strict[d], p[:c, :] * dec, 0.0) for (d, _), p, dec in zip(chains, prods, decays)]
        ts = _unit_triangular_inverses(lows, ri, ci)
        uws = [_dot(t, jnp.concatenate([vb, kb * e_c], axis=1)) for t, vb, kb, e_c in zip(ts, vbs, kbs, e_cs)]
        sts = [st_ref[j * HEAD_DIM:(j + 1) * HEAD_DIM, :] for j in range(len(chains))]
        if with_out:
            reads = [_dot(jnp.concatenate([uw[:, HEAD_DIM:], qf * e_c], axis=0), st)
                     for uw, qf, e_c, st in zip(uws, qfs, e_cs, sts)]
        else:
            reads = [_dot(uw[:, HEAD_DIM:], st) for uw, st in zip(uws, sts)]
        v_news = [uw[:, :HEAD_DIM] - r[:c, :] for uw, r in zip(uws, reads)]
        if with_out:
            writes = [_dot(jnp.concatenate([a, ko.T], axis=0), vn) for a, ko, vn in zip(attns, kos, v_news)]
        else:
            writes = [_dot_tn(ko, vn) for ko, vn in zip(kos, v_news)]
        for j, (d, h) in enumerate(chains):
            if with_out:
                hs = slice(h * HEAD_DIM, (h + 1) * HEAD_DIM)
                o = reads[j][c:, :] + writes[j][:c, :]
                if second_half:
                    o = o + o_ref[1 - d, rows[d], hs]
                    o = o * lax.rsqrt(jnp.mean(o * o, axis=-1, keepdims=True) + EPS) * nw_ref[...]
                    y_ref[0, rows[d], hs] = (o * _silu(gate_ref[0, rows[d], hs].astype(F32))).astype(BF16)
                else:
                    o_ref[d, rows[d], hs] = o
                upd = writes[j][c:, :]
            else:
                upd = writes[j]
            st_ref[j * HEAD_DIM:(j + 1) * HEAD_DIM, :] = sts[j] * jnp.exp(tots[j]) + upd
        return _

    assert n_chunks % 2 == 0
    half = n_chunks // 2
    st_ref[0:n_state, :] = s0f_ref[0]
    st_ref[n_state:2 * n_state, :] = s0b_ref[0]
    lax.fori_loop(0, half, functools.partial(step, second_half=False), None, unroll=min(2, half))
    lax.fori_loop(half, n_chunks, functools.partial(step, second_half=True), None, unroll=min(2, half))
    stf_ref[0] = st_ref[0:n_state, :]
    stb_ref[0] = st_ref[n_state:2 * n_state, :]


def _gdn(big, small, cw, gp, nw, s0f, s0b, *, with_out):
    b, l, _ = big.shape
    st_shape = jax.ShapeDtypeStruct((b, N_HEADS * HEAD_DIM, HEAD_DIM), F32)
    st_spec = pl.BlockSpec((1, N_HEADS * HEAD_DIM, HEAD_DIM), lambda bi: (bi, 0, 0))
    in_specs = [pl.BlockSpec((1, l, 3 * GROUP_W), lambda bi: (bi, 0, 1)),
                pl.BlockSpec((1, l, GROUP_W), lambda bi: (bi, 0, 6)),
                pl.BlockSpec((1, l, SMALL_W), lambda bi: (bi, 0, 0)),
                pl.BlockSpec((GDN_CONV, 3 * GROUP_W), lambda bi: (0, 0)),
                pl.BlockSpec((8, SMALL_W), lambda bi: (0, 0)),
                pl.BlockSpec((1, HEAD_DIM), lambda bi: (0, 0)),
                st_spec, st_spec]
    out_shape = [st_shape, st_shape]
    out_specs = [st_spec, st_spec]
    scratch = [pltpu.VMEM((l, 3 * GROUP_W), F32), pltpu.VMEM((2 * N_HEADS * HEAD_DIM, HEAD_DIM), F32)]
    if with_out:
        out_shape = [jax.ShapeDtypeStruct((b, l, GROUP_W), BF16)] + out_shape
        out_specs = [pl.BlockSpec((1, l, GROUP_W), lambda bi: (bi, 0, 0))] + out_specs
        scratch = [scratch[0], pltpu.VMEM((2, l, GROUP_W), F32), scratch[1]]
    scratch += [pltpu.VMEM((2, l, SMALL_W), F32), pltpu.VMEM((2, l, CHUNK), F32)]
    res = pl.pallas_call(
        functools.partial(_gdn_kernel, seq=l, with_out=with_out),
        out_shape=tuple(out_shape), grid=(b,), in_specs=in_specs, out_specs=tuple(out_specs),
        scratch_shapes=scratch, compiler_params=_cparams(("parallel",)),
        name="gdn_out" if with_out else "gdn_state",
    )(big, big, small, cw, gp, nw.reshape(1, -1), s0f, s0b)
    return res if with_out else (None,) + tuple(res)


def _gdn_gate_params(a_log, dt_bias):
    gp = jnp.zeros((8, SMALL_W), F32)
    gp = gp.at[0, SM_A[0]:SM_A[0] + 2 * N_HEADS].set(-jnp.exp(a_log.astype(F32)).reshape(-1))
    gp = gp.at[1, SM_A[0]:SM_A[0] + 2 * N_HEADS].set(dt_bias.astype(F32).reshape(-1))
    return gp


def _rope(x, cos, sin, lane):
    swapped = jnp.where((lane % 64) < 32, pltpu.roll(x, 96, 1), pltpu.roll(x, 32, 1))
    return x * cos + swapped * sin


def _softmax_sink_pv(s, sink_col, vv):
    m = jnp.maximum(jnp.max(s, axis=-1, keepdims=True), sink_col)
    p = jnp.exp(s - m)
    den = jnp.sum(p, axis=-1, keepdims=True) + jnp.exp(sink_col - m)
    return _dot(p, vv) * (1.0 / den)


def _swa_kernel(*refs, seq, ctx_len, with_ctx):
    if with_ctx:
        (q_ref, k_ref, v_ref, gate_ref, kc_ref, vc_ref, cos_ref, sin_ref, sink_ref, qc_ref, gatec_ref,
         y_ref, yc_ref, kr_ref, bias_ref) = refs
    else:
        (q_ref, k_ref, v_ref, gate_ref, kc_ref, vc_ref, cos_ref, sin_ref, sink_ref,
         y_ref, kr_ref, bias_ref) = refs
    blk = SWA_BLOCK
    nb = seq // blk
    win = 3 * blk
    grp = N_HEADS // SWA_KV_HEADS
    scale = HEAD_DIM ** -0.5
    lane = _iota2((blk, HEAD_DIM), 1)

    def rope_k(n, _):
        rows = pl.ds(pl.multiple_of(n * blk, blk), blk)
        cos = cos_ref[rows, :]
        sin = sin_ref[rows, :]
        for g in range(SWA_KV_HEADS):
            hs = slice(g * HEAD_DIM, (g + 1) * HEAD_DIM)
            kr_ref[rows, hs] = _rope(k_ref[0, rows, hs].astype(F32), cos, sin, lane).astype(BF16)
        return _

    lax.fori_loop(0, nb, rope_k, None)

    rowq = _iota2((grp * blk, win + ctx_len), 0) % blk
    colk = _iota2((grp * blk, win + ctx_len), 1)
    is_ctx = colk >= win
    for w in range(3):
        visible = is_ctx | (jnp.abs(rowq + w * blk - colk) <= SWA_WINDOW)
        bias_ref[w] = jnp.where(visible, 0.0, NEG_BIG)

    def sink_column(g, n_rows):
        hr = _iota2((grp * n_rows, 1), 0) // n_rows
        col = jnp.zeros((grp * n_rows, 1), F32)
        for j in range(grp):
            col = jnp.where(hr == j, sink_ref[0:1, g * grp + j:g * grp + j + 1], col)
        return col

    def q_block(n, _):
        r0 = pl.multiple_of(n * blk, blk)
        rows = pl.ds(r0, blk)
        k0 = pl.multiple_of(jnp.clip(r0 - blk, 0, seq - win), blk)
        cos = cos_ref[rows, :]
        sin = sin_ref[rows, :]
        bias = bias_ref[(r0 - k0) // blk]
        for g in range(SWA_KV_HEADS):
            hs = slice(g * HEAD_DIM, (g + 1) * HEAD_DIM)
            qg = jnp.concatenate(
                [_rope(q_ref[0, rows, (g * grp + j) * HEAD_DIM:(g * grp + j + 1) * HEAD_DIM].astype(F32),
                       cos, sin, lane) * scale for j in range(grp)], axis=0).astype(BF16)
            kk = jnp.concatenate([kr_ref[pl.ds(k0, win), hs], kc_ref[0, :, hs]], axis=0)
            vv = jnp.concatenate([v_ref[0, pl.ds(k0, win), hs], vc_ref[0, :, hs]], axis=0)
            s = _dot_nt(qg, kk) + bias
            o = _softmax_sink_pv(s, sink_column(g, blk), vv)
            for j in range(grp):
                cs = slice((g * grp + j) * HEAD_DIM, (g * grp + j + 1) * HEAD_DIM)
                y_ref[0, rows, cs] = (o[j * blk:(j + 1) * blk, :]
                                      * _silu(gate_ref[0, rows, cs].astype(F32))).astype(BF16)
        return _

    lax.fori_loop(0, nb, q_block, None, unroll=2)

    if with_ctx:
        for g in range(SWA_KV_HEADS):
            hs = slice(g * HEAD_DIM, (g + 1) * HEAD_DIM)
            qg = jnp.concatenate([qc_ref[0, :, (g * grp + j) * HEAD_DIM:(g * grp + j + 1) * HEAD_DIM]
                                  for j in range(grp)], axis=0)
            s = _dot_nt(qg, kc_ref[0, :, hs]) * scale
            o = _softmax_sink_pv(s, sink_column(g, ctx_len), vc_ref[0, :, hs])
            for j in range(grp):
                cs = slice((g * grp + j) * HEAD_DIM, (g * grp + j + 1) * HEAD_DIM)
                yc_ref[0, :, cs] = (o[j * ctx_len:(j + 1) * ctx_len, :]
                                    * _silu(gatec_ref[0, :, cs].astype(F32))).astype(BF16)


def _swa(big, bigc, cos, sin, sink, *, with_ctx):
    b, l, _ = big.shape
    lc = bigc.shape[1]
    in_specs = [pl.BlockSpec((1, l, 512), lambda bi: (bi, 0, 11)),
                pl.BlockSpec((1, l, 256), lambda bi: (bi, 0, 24)),
                pl.BlockSpec((1, l, 256), lambda bi: (bi, 0, 25)),
                pl.BlockSpec((1, l, 512), lambda bi: (bi, 0, 13)),
                pl.BlockSpec((1, lc, 256), lambda bi: (bi, 0, 24)),
                pl.BlockSpec((1, lc, 256), lambda bi: (bi, 0, 25)),
                pl.BlockSpec((l, HEAD_DIM), lambda bi: (0, 0)),
                pl.BlockSpec((l, HEAD_DIM), lambda bi: (0, 0)),
                pl.BlockSpec((1, LANES), lambda bi: (0, 0))]
    args = [big, big, big, big, bigc, bigc, cos, sin, sink]
    out_shape = [jax.ShapeDtypeStruct((b, l, GROUP_W), BF16)]
    out_specs = [pl.BlockSpec((1, l, GROUP_W), lambda bi: (bi, 0, 0))]
    if with_ctx:
        in_specs += [pl.BlockSpec((1, lc, 512), lambda bi: (bi, 0, 11)),
                     pl.BlockSpec((1, lc, 512), lambda bi: (bi, 0, 13))]
        args += [bigc, bigc]
        out_shape.append(jax.ShapeDtypeStruct((b, lc, GROUP_W), BF16))
        out_specs.append(pl.BlockSpec((1, lc, GROUP_W), lambda bi: (bi, 0, 0)))
    res = pl.pallas_call(
        functools.partial(_swa_kernel, seq=l, ctx_len=lc, with_ctx=with_ctx),
        out_shape=tuple(out_shape), grid=(b,), in_specs=in_specs, out_specs=tuple(out_specs),
        scratch_shapes=[pltpu.VMEM((l, SWA_KV_HEADS * HEAD_DIM), BF16),
                        pltpu.VMEM((3, (N_HEADS // SWA_KV_HEADS) * SWA_BLOCK, 3 * SWA_BLOCK + lc), F32)],
        compiler_params=_cparams(("parallel",)),
        name="swa_ctx" if with_ctx else "swa",
    )(*args)
    return (res[0], res[1]) if with_ctx else (res[0], None)


def _rope_tables(seq_len):
    rows = seq_len // GRID_W
    row = jnp.repeat(jnp.arange(rows, dtype=F32), GRID_W)
    col = jnp.tile(jnp.arange(GRID_W, dtype=F32), rows)
    axis_dim = HEAD_DIM // 2
    inv_freq = ROPE_BASE ** (-jnp.arange(0, axis_dim, 2, dtype=F32) / axis_dim)
    ang_r = row[:, None] * inv_freq
    ang_c = col[:, None] * inv_freq
    cos = jnp.concatenate([jnp.cos(ang_r), jnp.cos(ang_r), jnp.cos(ang_c), jnp.cos(ang_c)], axis=-1)
    sin = jnp.concatenate([-jnp.sin(ang_r), jnp.sin(ang_r), -jnp.sin(ang_c), jnp.sin(ang_c)], axis=-1)
    return cos, sin


_PACK_RUNS = ((0, 0, 1024), (1024, 1056, 2048), (3072, 3120, 4096))
_LR_COL0 = 1024
_GDN_GATE_COL0 = 3104


def _pack_kernel(wt_ref, big_ref, small_ref):
    step = 512
    for dst, src, width in _PACK_RUNS:
        for o in range(0, width, step):
            big_ref[0, :, dst + o:dst + o + step] = wt_ref[0, src + o:src + o + step, :].T.astype(BF16)
    assert _LR_COL0 % LANES == 0 and _GDN_GATE_COL0 % LANES == SM_A[0]
    gate_win0 = _GDN_GATE_COL0 - SM_A[0]
    lane = _iota2((wt_ref.shape[2], LANES), 1)
    lr_win = wt_ref[0, _LR_COL0:_LR_COL0 + LANES, :].T
    gate_win = wt_ref[0, gate_win0:gate_win0 + LANES, :].T
    small = jnp.where(lane < SM_A[0], lr_win, jnp.where(lane < SM_B[1] + N_HEADS, gate_win, 0.0))
    small_ref[0] = small.astype(BF16)


def _pack_w_in(w_in):
    n_layers, d, n = w_in.shape
    kb = PACK_KB
    return pl.pallas_call(
        _pack_kernel,
        out_shape=(jax.ShapeDtypeStruct((n_layers, d, BIG_W), BF16),
                   jax.ShapeDtypeStruct((n_layers, d, SMALL_W), BF16)),
        grid=(n_layers, d // kb),
        in_specs=[pl.BlockSpec((1, n, kb), lambda l, i: (l, 0, i))],
        out_specs=(pl.BlockSpec((1, kb, BIG_W), lambda l, i: (l, i, 0)),
                   pl.BlockSpec((1, kb, SMALL_W), lambda l, i: (l, i, 0))),
        compiler_params=_cparams(("parallel", "parallel")),
        name="pack_w_in",
    )(jnp.swapaxes(w_in, 1, 2))


def kernel(x, c, ctx, c_ctx, ada_w, ada_b, norm_pre, norm_post, w_in, w_out, gla_w_decay, gla_b_decay,
           gla_norm, gdn_conv, gdn_a_log, gdn_dt_bias, gdn_norm, sc_conv, swa_sink):
    b, l, d = x.shape
    cos, sin = _rope_tables(l)
    cc = jnp.zeros((8, d), F32).at[:b].set(c).at[b].set(c_ctx)
    mod = _ada(cc, ada_w, ada_b)
    h, hc = x, ctx
    lc = ctx.shape[1]
    n_layers = ada_w.shape[0]
    w_big, w_small = _pack_w_in(w_in)
    w_o = w_out.astype(BF16)
    for layer in range(n_layers):
        with_ctx = layer < n_layers - 1
        shift, scale, gate = (mod[layer, :, i * d:(i + 1) * d] for i in range(3))

        def per_batch(t):
            return t[:b, None, :], t[b][None, None, :]
        (shift_l, shift_c), (scale_l, scale_c), (gate_l, gate_c) = per_batch(shift), per_batch(scale), per_batch(gate)
        g_pre = norm_pre[layer].reshape(1, d)
        big, small = _inproj(h, scale_l, shift_l, g_pre, w_big, w_small, layer)
        bigc, smallc = _inproj(hc.reshape(1, b * lc, d), scale_c, shift_c, g_pre, w_big, w_small, layer)
        bigc, smallc = bigc.reshape(b, lc, BIG_W), smallc.reshape(b, lc, SMALL_W)

        zf = jnp.zeros((b, N_HEADS * GLA_DK, HEAD_DIM), F32)
        yca, stf, stb = _gla(bigc, smallc, gla_w_decay[layer], gla_b_decay[layer], gla_norm[layer], zf, zf,
                             with_out=with_ctx)
        ya, _, _ = _gla(big, small, gla_w_decay[layer], gla_b_decay[layer], gla_norm[layer], stf, stb,
                        with_out=True)

        gp = _gdn_gate_params(gdn_a_log[layer], gdn_dt_bias[layer])
        zg = jnp.zeros((b, N_HEADS * HEAD_DIM, HEAD_DIM), F32)
        ycb, gtf, gtb = _gdn(bigc, smallc, gdn_conv[layer], gp, gdn_norm[layer], zg, zg, with_out=with_ctx)
        yb, _, _ = _gdn(big, small, gdn_conv[layer], gp, gdn_norm[layer], gtf, gtb, with_out=True)

        sink = jnp.zeros((1, LANES), F32).at[0, :N_HEADS].set(swa_sink[layer])
        yd, ycd = _swa(big, bigc, cos, sin, sink, with_ctx=with_ctx)

        g_post = norm_post[layer].reshape(1, d)
        h_new = _outproj_sconv(ya, yb, yd, big, sc_conv[layer], w_o, layer, h, gate_l, g_post)
        if with_ctx:
            ycc = _sconv(bigc, sc_conv[layer])
            flat = [t.reshape(1, b * lc, GROUP_W) for t in (yca, ycb, ycc, ycd)]
            hc = _outproj(*flat, w_o, layer, hc.reshape(1, b * lc, d), gate_c, g_post).reshape(b, lc, d)
        h = h_new
    return h
```

```python
import functools

import jax
import jax.numpy as jnp
from jax import lax
from jax.experimental import pallas as pl
from jax.experimental.pallas import tpu as pltpu

F32 = jnp.float32
BF16 = jnp.bfloat16

D_MODEL = 2048
N_LAYERS = 2
GRID_W = 64
GROUP_W = 512
HEAD_DIM = 128
N_HEADS = 4
EPS = 1e-6
GLA_DK = 64
GLA_RANK = 16
GLA_TAU = 16.0
GDN_CONV = 5
SC_CONV = 3
SWA_KV_HEADS = 2
SWA_WINDOW = 128
ROPE_BASE = 10000.0

_SPLITS = (
    ("gla_q", 256), ("gla_k", 256), ("gla_v", 512), ("gla_lr_f", 16), ("gla_lr_b", 16), ("gla_gate", 512),
    ("gdn_q", 512), ("gdn_k", 512), ("gdn_v", 512),
    ("gdn_a_f", 4), ("gdn_a_b", 4), ("gdn_b_f", 4), ("gdn_b_b", 4), ("gdn_gate", 512),
    ("sc_b", 512), ("sc_c", 512), ("sc_h", 512), ("sc_gate", 512),
    ("swa_q", 512), ("swa_k", 256), ("swa_v", 256), ("swa_gate", 512),
)
BIG_W = 7168
SMALL_W = 128
SM_LR = (0, 16)
SM_A = (32, 36)
SM_B = (40, 44)

LANES = 128
VMEM_LIMIT = 56 * 1024 * 1024

CHUNK = 128
SWA_BLOCK = 128
INPROJ_TM = 1024
INPROJ_TN = 1024
OUTPROJ_TM = 512
ADA_TN = 1024
PACK_KB = 256

NEG_BIG = -1e30


def _cparams(sem):
    return pltpu.CompilerParams(dimension_semantics=sem, vmem_limit_bytes=VMEM_LIMIT)


def _dot(a, b):
    return lax.dot_general(a.astype(BF16), b.astype(BF16), (((1,), (0,)), ((), ())),
                           preferred_element_type=F32)


def _dot_nt(a, b):
    return lax.dot_general(a.astype(BF16), b.astype(BF16), (((1,), (1,)), ((), ())),
                           preferred_element_type=F32)


def _dot_tn(a, b):
    return lax.dot_general(a.astype(BF16), b.astype(BF16), (((0,), (0,)), ((), ())),
                           preferred_element_type=F32)


def _split(x):
    hi = x.astype(BF16)
    lo = (x - hi.astype(F32)).astype(BF16)
    return hi, lo


def _dot_exact_lhs(a_bf16, x):
    hi, lo = _split(x)
    return _dot(a_bf16, hi) + _dot(a_bf16, lo)


def _dot_exact_rhs(x, b_bf16):
    hi, lo = _split(x)
    return _dot(hi, b_bf16) + _dot(lo, b_bf16)


def _sigmoid(x):
    return 0.5 * jnp.tanh(0.5 * x) + 0.5


def _silu(x):
    return x * _sigmoid(x)


def _softplus(x):
    return jnp.maximum(x, 0.0) + jnp.log(1.0 + jnp.exp(-jnp.abs(x)))


def _log_sigmoid(x):
    return jnp.minimum(x, 0.0) - jnp.log(1.0 + jnp.exp(-jnp.abs(x)))


def _iota2(shape, dim):
    return lax.broadcasted_iota(jnp.int32, shape, dim)


ADA_K_SPLIT = 4


def _ada_kernel(c_ref, *refs):
    w_refs, (b_ref, o_ref) = refs[:ADA_K_SPLIT], refs[ADA_K_SPLIT:]
    a = _silu(c_ref[...])
    kq = a.shape[1] // ADA_K_SPLIT
    acc = b_ref[0]
    for q, w_ref in enumerate(w_refs):
        acc = acc + _dot(a[:, q * kq:(q + 1) * kq], w_ref[0])
    o_ref[0] = acc


def _ada(cc, ada_w, ada_b):
    n_layers, d, n = ada_w.shape
    tn = ADA_TN
    kq = d // ADA_K_SPLIT
    w_specs = [pl.BlockSpec((1, kq, tn), lambda l, j, q=q: (l, q, j)) for q in range(ADA_K_SPLIT)]
    return pl.pallas_call(
        _ada_kernel,
        out_shape=jax.ShapeDtypeStruct((n_layers, 8, n), F32),
        grid=(n_layers, n // tn),
        in_specs=[pl.BlockSpec((8, d), lambda l, j: (0, 0))] + w_specs
        + [pl.BlockSpec((1, 1, tn), lambda l, j: (l, 0, j))],
        out_specs=pl.BlockSpec((1, 8, tn), lambda l, j: (l, 0, j)),
        compiler_params=_cparams(("parallel", "parallel")),
        name="ada_mod",
    )(cc, *([ada_w] * ADA_K_SPLIT), ada_b.reshape(n_layers, 1, n))


def _inproj_kernel(x_ref, sc_ref, sh_ref, g_ref, wb_ref, ws_ref, ob_ref, os_ref, xn_ref):
    @pl.when(pl.program_id(2) == 0)
    def _():
        rows_per_step = 128
        gain = g_ref[...] * (1.0 + sc_ref[0])

        def norm_rows(i, _):
            rows = pl.ds(pl.multiple_of(i * rows_per_step, rows_per_step), rows_per_step)
            x = x_ref[0, rows, :]
            y = x * lax.rsqrt(jnp.mean(x * x, axis=-1, keepdims=True) + EPS)
            xn_ref[rows, :] = (y * gain + sh_ref[0]).astype(BF16)
            return _

        lax.fori_loop(0, xn_ref.shape[0] // rows_per_step, norm_rows, None)
        os_ref[0] = jnp.dot(xn_ref[...], ws_ref[0], preferred_element_type=F32)

    ob_ref[0] = jnp.dot(xn_ref[...], wb_ref[0], preferred_element_type=F32).astype(BF16)


def _inproj(x, scale, shift, g, w_big, w_small, layer):
    b, l, d = x.shape
    tm, tn = min(l, INPROJ_TM), INPROJ_TN
    return pl.pallas_call(
        _inproj_kernel,
        out_shape=(jax.ShapeDtypeStruct((b, l, BIG_W), BF16), jax.ShapeDtypeStruct((b, l, SMALL_W), F32)),
        grid=(b, l // tm, BIG_W // tn),
        in_specs=[pl.BlockSpec((1, tm, d), lambda bi, i, j: (bi, i, 0)),
                  pl.BlockSpec((1, 1, d), lambda bi, i, j: (bi, 0, 0)),
                  pl.BlockSpec((1, 1, d), lambda bi, i, j: (bi, 0, 0)),
                  pl.BlockSpec((1, d), lambda bi, i, j: (0, 0)),
                  pl.BlockSpec((1, d, tn), lambda bi, i, j: (layer, 0, j)),
                  pl.BlockSpec((1, d, SMALL_W), lambda bi, i, j: (layer, 0, 0))],
        out_specs=(pl.BlockSpec((1, tm, tn), lambda bi, i, j: (bi, i, j)),
                   pl.BlockSpec((1, tm, SMALL_W), lambda bi, i, j: (bi, i, 0))),
        scratch_shapes=[pltpu.VMEM((tm, d), BF16)],
        compiler_params=_cparams(("parallel", "parallel", "arbitrary")),
        name="inproj",
    )(x, scale, shift, g, w_big, w_small)


def _group_dot(group, k, w_ref):
    return jnp.dot(group, w_ref[0, k * GROUP_W:(k + 1) * GROUP_W, :], preferred_element_type=F32)


def _outproj_tail(y, hl_ref, hr_ref, gate_ref, g_ref, o_ref):
    yn = gate_ref[0] * (y * lax.rsqrt(jnp.mean(y * y, axis=-1, keepdims=True) + EPS) * g_ref[...])
    half = hl_ref.shape[2]
    o_ref[0, :, 0:half] = hl_ref[0] + yn[:, 0:half]
    o_ref[0, :, half:2 * half] = hr_ref[0] + yn[:, half:2 * half]


def _outproj_kernel(ya_ref, yb_ref, yc_ref, yd_ref, w_ref, hl_ref, hr_ref, gate_ref, g_ref, o_ref):
    y = _group_dot(ya_ref[0], 0, w_ref)
    for k, ref in ((1, yb_ref), (2, yc_ref), (3, yd_ref)):
        y += _group_dot(ref[0], k, w_ref)
    _outproj_tail(y, hl_ref, hr_ref, gate_ref, g_ref, o_ref)


SC_HALO = 16


def _outproj_sconv_kernel(ya_ref, yb_ref, yd_ref, sb_ref, sc_ref, sh_ref, sg_ref, cp_ref, hp_ref, cn_ref, hn_ref,
                          cw_ref, w_ref, hl_ref, hr_ref, gate_ref, g_ref, o_ref):
    y = _group_dot(ya_ref[0], 0, w_ref) + _group_dot(yb_ref[0], 1, w_ref) + _group_dot(yd_ref[0], 3, w_ref)
    i = pl.program_id(1)
    tm = sc_ref.shape[1]
    p = sc_ref[0].astype(F32) * sh_ref[0].astype(F32)
    p_prev = jnp.where(i > 0, cp_ref[0].astype(F32) * hp_ref[0].astype(F32), 0.0)
    p_next = jnp.where(i < pl.num_programs(1) - 1, cn_ref[0].astype(F32) * hn_ref[0].astype(F32), 0.0)
    pe = jnp.concatenate([p_prev, p, p_next], axis=0)
    conv = (pe[SC_HALO - 1:SC_HALO - 1 + tm, :] * cw_ref[0:1, :] + p * cw_ref[1:2, :]
            + pe[SC_HALO + 1:SC_HALO + 1 + tm, :] * cw_ref[2:3, :])
    yc = (sb_ref[0].astype(F32) * conv * _silu(sg_ref[0].astype(F32))).astype(BF16)
    _outproj_tail(y + _group_dot(yc, 2, w_ref), hl_ref, hr_ref, gate_ref, g_ref, o_ref)


def _outproj_sconv(ya, yb, yd, big, cw, w_out, layer, h, gate, g):
    b, l, d = h.shape
    tm = min(l, OUTPROJ_TM)
    assert SC_CONV == 3 and tm % SC_HALO == 0
    col0 = 3584 // GROUP_W
    per = tm // SC_HALO
    n_halo = l // SC_HALO
    yspec = pl.BlockSpec((1, tm, GROUP_W), lambda bi, i: (bi, i, 0))

    def col(k):
        return pl.BlockSpec((1, tm, GROUP_W), lambda bi, i, k=k: (bi, i, col0 + k))

    def prev(k):
        return pl.BlockSpec((1, SC_HALO, GROUP_W), lambda bi, i, k=k: (bi, jnp.maximum(i * per - 1, 0), col0 + k))

    def nxt(k):
        return pl.BlockSpec((1, SC_HALO, GROUP_W),
                            lambda bi, i, k=k: (bi, jnp.minimum((i + 1) * per, n_halo - 1), col0 + k))
    return pl.pallas_call(
        _outproj_sconv_kernel,
        out_shape=jax.ShapeDtypeStruct((b, l, d), F32),
        grid=(b, l // tm),
        in_specs=[yspec, yspec, yspec, col(0), col(1), col(2), col(3), prev(1), prev(2), nxt(1), nxt(2),
                  pl.BlockSpec((SC_CONV, GROUP_W), lambda bi, i: (0, 0)),
                  pl.BlockSpec((1, d, d), lambda bi, i: (layer, 0, 0)),
                  pl.BlockSpec((1, tm, d // 2), lambda bi, i: (bi, i, 0)),
                  pl.BlockSpec((1, tm, d // 2), lambda bi, i: (bi, i, 1)),
                  pl.BlockSpec((1, 1, d), lambda bi, i: (bi, 0, 0)),
                  pl.BlockSpec((1, d), lambda bi, i: (0, 0))],
        out_specs=pl.BlockSpec((1, tm, d), lambda bi, i: (bi, i, 0)),
        compiler_params=_cparams(("parallel", "parallel")),
        name="outproj_sconv",
    )(ya, yb, yd, big, big, big, big, big, big, big, big, cw, w_out, h, h, gate, g)


def _outproj(ya, yb, yc, yd, w_out, layer, h, gate, g):
    b, l, d = h.shape
    tm = min(l, OUTPROJ_TM)
    yspec = pl.BlockSpec((1, tm, GROUP_W), lambda bi, i: (bi, i, 0))
    return pl.pallas_call(
        _outproj_kernel,
        out_shape=jax.ShapeDtypeStruct((b, l, d), F32),
        grid=(b, l // tm),
        in_specs=[yspec, yspec, yspec, yspec,
                  pl.BlockSpec((1, d, d), lambda bi, i: (layer, 0, 0)),
                  pl.BlockSpec((1, tm, d // 2), lambda bi, i: (bi, i, 0)),
                  pl.BlockSpec((1, tm, d // 2), lambda bi, i: (bi, i, 1)),
                  pl.BlockSpec((1, 1, d), lambda bi, i: (bi, 0, 0)),
                  pl.BlockSpec((1, d), lambda bi, i: (0, 0))],
        out_specs=pl.BlockSpec((1, tm, d), lambda bi, i: (bi, i, 0)),
        compiler_params=_cparams(("parallel", "parallel")),
        name="outproj",
    )(ya, yb, yc, yd, w_out, h, h, gate, g)


def _sconv_kernel(b_ref, c_ref, h_ref, gate_ref, w_ref, o_ref, p_ref, *, seq):
    zeros = jnp.zeros((8, GROUP_W), F32)
    p_ref[0:8, :] = zeros
    p_ref[seq + 8:seq + 16, :] = zeros
    p_ref[8:seq + 8, :] = c_ref[0].astype(F32) * h_ref[0].astype(F32)
    rows = min(seq, 256)
    w = w_ref[...]
    for t0 in range(0, seq, rows):
        conv = (p_ref[t0 + 7:t0 + 7 + rows, :] * w[0:1, :] + p_ref[t0 + 8:t0 + 8 + rows, :] * w[1:2, :]
                + p_ref[t0 + 9:t0 + 9 + rows, :] * w[2:3, :])
        y = b_ref[0, t0:t0 + rows, :].astype(F32) * conv * _silu(gate_ref[0, t0:t0 + rows, :].astype(F32))
        o_ref[0, t0:t0 + rows, :] = y.astype(BF16)


def _sconv(big, w):
    b, l, _ = big.shape
    col0 = 3584 // GROUP_W

    def spec(k):
        return pl.BlockSpec((1, l, GROUP_W), lambda bi, k=k: (bi, 0, col0 + k))
    return pl.pallas_call(
        functools.partial(_sconv_kernel, seq=l),
        out_shape=jax.ShapeDtypeStruct((b, l, GROUP_W), BF16),
        grid=(b,),
        in_specs=[spec(0), spec(1), spec(2), spec(3), pl.BlockSpec((SC_CONV, GROUP_W), lambda bi: (0, 0))],
        out_specs=pl.BlockSpec((1, l, GROUP_W), lambda bi: (bi, 0, 0)),
        scratch_shapes=[pltpu.VMEM((l + 16, GROUP_W), F32)],
        compiler_params=_cparams(("parallel",)),
        name="sconv",
    )(big, big, big, big, w)


def _gla_kernel(q_ref, k_ref, v_ref, gate_ref, sm_ref, wd_ref, bd_ref, nw_ref, s0f_ref, s0b_ref,
                *rest, seq, with_out):
    if with_out:
        y_ref, stf_ref, stb_ref, o_ref, st_ref, ko_ref, dec_ref, qp_ref, kp_ref, qi_ref = rest
    else:
        stf_ref, stb_ref, st_ref, ko_ref, dec_ref = rest
        y_ref = o_ref = qp_ref = kp_ref = qi_ref = None
    c = CHUNK
    n_chunks = seq // c
    ri = _iota2((c, c), 0)
    ci = _iota2((c, c), 1)
    incl = (ri >= ci, ri <= ci)
    tri = tuple(jnp.where(m, 1.0, 0.0).astype(BF16) for m in incl)
    qscale = GLA_DK ** -0.5
    last = (c - 1, 0)
    kw = N_HEADS * GLA_DK
    assert c == HEAD_DIM
    r4 = _iota2((N_HEADS * c, c), 0) % c
    c4 = _iota2((N_HEADS * c, c), 1)
    incl4 = (r4 >= c4, r4 <= c4)
    lane_head = _iota2((c, kw), 1) // GLA_DK
    row_head = _iota2((kw, HEAD_DIM), 0) // GLA_DK
    zero16 = jnp.zeros((c, kw), BF16)

    def prep(n, _):
        rows = pl.ds(pl.multiple_of(n * c, c), c)
        sm = sm_ref[0, rows, :]
        kf = k_ref[0, rows, :].astype(F32)
        if with_out:
            qf = q_ref[0, rows, :].astype(F32) * qscale
        for d in (0, 1):
            z = _dot(sm[:, SM_LR[d]:SM_LR[d] + GLA_RANK], wd_ref[d]) + bd_ref[d]
            la = _log_sigmoid(z) * (1.0 / GLA_TAU)
            cum = _dot_exact_lhs(tri[d], la)
            tot = cum[last[d]:last[d] + 1, :]
            krows = pl.ds(pl.multiple_of(n * kw, kw), kw)
            dec_ref[d, krows, :] = jnp.broadcast_to(jnp.exp(tot), (HEAD_DIM, kw)).T
            if with_out:
                mid = cum[c // 2:c // 2 + 1, :]
                qp = qf * jnp.exp(cum - mid)
                kp = kf * jnp.exp(mid - cum)
                qp_ref[d, rows, :] = qp.astype(BF16)
                kp_ref[d, rows, :] = kp.astype(BF16)
                qi_ref[d, rows, :] = (qp * jnp.exp(mid)).astype(BF16)
                ko = kp * jnp.exp(tot - mid)
            else:
                ko = kf * jnp.exp(tot - cum)
            ko_ref[d, krows, :] = ko.T.astype(BF16)
        return _

    lax.fori_loop(0, n_chunks, prep, None, unroll=min(4, n_chunks))

    def step(i, _, second_half):
        chunk_of = (i, n_chunks - 1 - i)
        rows = [pl.ds(pl.multiple_of(n * c, c), c) for n in chunk_of]
        krows = [pl.ds(pl.multiple_of(n * kw, kw), kw) for n in chunk_of]
        dirs = (0, 1)
        sts = [st_ref[d] for d in dirs]
        vs = [v_ref[0, rows[d], :] for d in dirs]

        def dv(h):
            return slice(h * HEAD_DIM, (h + 1) * HEAD_DIM)

        if with_out:
            qp4 = [jnp.concatenate([jnp.where(lane_head == h, qp_ref[d, rows[d], :], zero16)
                                    for h in range(N_HEADS)], axis=0) for d in dirs]
            attn4 = [jnp.where(incl4[d], _dot_nt(qp4[d], kp_ref[d, rows[d], :]), 0.0) for d in dirs]
            s_bd = [jnp.concatenate([jnp.where(row_head == h, sts[d], 0.0).astype(BF16)
                                     for h in range(N_HEADS)], axis=1) for d in dirs]
            inter = [_dot(qi_ref[d, rows[d], :], s_bd[d]) for d in dirs]
            intra = [[_dot(attn4[d][h * c:(h + 1) * c, :], vs[d][:, dv(h)]) for h in range(N_HEADS)]
                     for d in dirs]
            for d in dirs:
                o = inter[d] + jnp.concatenate(intra[d], axis=1)
                if not second_half:
                    o_ref[d, rows[d], :] = o
                    continue
                o = o + o_ref[1 - d, rows[d], :]
                for h in range(N_HEADS):
                    oh = o[:, dv(h)]
                    oh = oh * lax.rsqrt(jnp.mean(oh * oh, axis=-1, keepdims=True) + EPS) * nw_ref[...]
                    y_ref[0, rows[d], dv(h)] = (oh * _silu(gate_ref[0, rows[d], dv(h)].astype(F32))).astype(BF16)
        upd = [_dot(ko_ref[d, krows[d], :], vs[d]) for d in dirs]
        for d in dirs:
            diag = [upd[d][h * GLA_DK:(h + 1) * GLA_DK, dv(h)] for h in range(N_HEADS)]
            st_ref[d] = sts[d] * dec_ref[d, krows[d], :] + jnp.concatenate(diag, axis=0)
        return _

    assert n_chunks % 2 == 0
    half = n_chunks // 2
    st_ref[0] = s0f_ref[0]
    st_ref[1] = s0b_ref[0]
    lax.fori_loop(0, half, functools.partial(step, second_half=False), None, unroll=min(4, half))
    lax.fori_loop(half, n_chunks, functools.partial(step, second_half=True), None, unroll=min(4, half))
    stf_ref[0] = st_ref[0]
    stb_ref[0] = st_ref[1]


def _gla(big, small, wd, bd, nw, s0f, s0b, *, with_out):
    b, l, _ = big.shape
    st_shape = jax.ShapeDtypeStruct((b, N_HEADS * GLA_DK, HEAD_DIM), F32)
    st_spec = pl.BlockSpec((1, N_HEADS * GLA_DK, HEAD_DIM), lambda bi: (bi, 0, 0))
    in_specs = [pl.BlockSpec((1, l, 256), lambda bi: (bi, 0, 0)),
                pl.BlockSpec((1, l, 256), lambda bi: (bi, 0, 1)),
                pl.BlockSpec((1, l, 512), lambda bi: (bi, 0, 1)),
                pl.BlockSpec((1, l, 512), lambda bi: (bi, 0, 2)),
                pl.BlockSpec((1, l, SMALL_W), lambda bi: (bi, 0, 0)),
                pl.BlockSpec((2, GLA_RANK, 256), lambda bi: (0, 0, 0)),
                pl.BlockSpec((2, 1, 256), lambda bi: (0, 0, 0)),
                pl.BlockSpec((1, HEAD_DIM), lambda bi: (0, 0)),
                st_spec, st_spec]
    out_shape = [st_shape, st_shape]
    out_specs = [st_spec, st_spec]
    kw = N_HEADS * GLA_DK
    n_chunks = l // CHUNK
    scratch = [pltpu.VMEM((2, kw, HEAD_DIM), F32),
               pltpu.VMEM((2, n_chunks * kw, CHUNK), BF16),
               pltpu.VMEM((2, n_chunks * kw, HEAD_DIM), F32)]
    if with_out:
        out_shape = [jax.ShapeDtypeStruct((b, l, GROUP_W), BF16)] + out_shape
        out_specs = [pl.BlockSpec((1, l, GROUP_W), lambda bi: (bi, 0, 0))] + out_specs
        scratch = [pltpu.VMEM((2, l, GROUP_W), F32)] + scratch + [pltpu.VMEM((2, l, kw), BF16)] * 3
    res = pl.pallas_call(
        functools.partial(_gla_kernel, seq=l, with_out=with_out),
        out_shape=tuple(out_shape), grid=(b,), in_specs=in_specs, out_specs=tuple(out_specs),
        scratch_shapes=scratch, compiler_params=_cparams(("parallel",)),
        name="gla_out" if with_out else "gla_state",
    )(big, big, big, big, small, wd, bd.reshape(2, 1, -1), nw.reshape(1, -1), s0f, s0b)
    return res if with_out else (None,) + tuple(res)


def _unit_triangular_inverses(lows, ri, ci):
    c = lows[0].shape[0]
    eye = jnp.where(ri == ci, 1.0, 0.0)
    pair = (ri // 2) == (ci // 2)
    ts = [eye - jnp.where(pair, low, 0.0) for low in lows]
    s = 2
    while s < c:
        sel = ((ri // (2 * s)) == (ci // (2 * s))) & ((ri // s) != (ci // s))
        tb = [t.astype(BF16) for t in ts]
        ps = [_dot(t, jnp.where(sel, low, 0.0)) for t, low in zip(tb, lows)]
        ts = [t - _dot(p, t16) for t, p, t16 in zip(ts, ps, tb)]
        s *= 2
    return ts


def _gdn_kernel(qkv_ref, gate_ref, sm_ref, cw_ref, gp_ref, nw_ref, s0f_ref, s0b_ref,
                *rest, seq, with_out):
    if with_out:
        y_ref, stf_ref, stb_ref, x_ref, o_ref, st_ref, cumc_ref, cumr_ref = rest
    else:
        stf_ref, stb_ref, x_ref, st_ref, cumc_ref, cumr_ref = rest
        y_ref = o_ref = None
    c = CHUNK
    assert SMALL_W == c
    n_chunks = seq // c
    width = 3 * GROUP_W
    ri = _iota2((c, c), 0)
    ci = _iota2((c, c), 1)
    lower = ri >= ci
    upper = ri <= ci
    tril = jnp.where(lower, 1.0, 0.0).astype(BF16)
    triu = jnp.where(upper, 1.0, 0.0).astype(BF16)

    halo = 16
    off_taps = tuple(j for j in range(GDN_CONV) if j != GDN_CONV // 2)
    sr = _iota2((len(off_taps) * c, c + 2 * halo), 0)
    sc_ = _iota2((len(off_taps) * c, c + 2 * halo), 1)
    tap = jnp.zeros_like(sr)
    for idx, j in enumerate(off_taps):
        tap = jnp.where(sr // c == idx, j, tap)
    shift_mat = jnp.where(sc_ == (sr % c) + halo - GDN_CONV // 2 + tap, 1.0, 0.0).astype(BF16)

    def conv_chunk(n, _):
        r0 = pl.multiple_of(n * c, c)
        prev0 = pl.multiple_of(jnp.maximum(r0 - 16, 0), 16)
        next0 = pl.multiple_of(jnp.minimum(r0 + c, seq - 16), 16)
        prev = jnp.where(n > 0, qkv_ref[0, pl.ds(prev0, 16), :].astype(F32), 0.0).astype(BF16)
        nxt = jnp.where(n < n_chunks - 1, qkv_ref[0, pl.ds(next0, 16), :].astype(F32), 0.0).astype(BF16)
        cur = qkv_ref[0, pl.ds(r0, c), :]
        xe = jnp.concatenate([prev, cur, nxt], axis=0)
        shifted = _dot(shift_mat, xe)
        acc = cur.astype(F32) * cw_ref[GDN_CONV // 2:GDN_CONV // 2 + 1, :]
        for idx, j in enumerate(off_taps):
            acc = acc + shifted[idx * c:(idx + 1) * c, :] * cw_ref[j:j + 1, :]
        acc = _silu(acc)
        for h in range(2 * N_HEADS):
            hs = slice(h * HEAD_DIM, (h + 1) * HEAD_DIM)
            t = acc[:, hs]
            t = t * lax.rsqrt(jnp.sum(t * t, axis=-1, keepdims=True) + EPS)
            if h < N_HEADS:
                t = t * (HEAD_DIM ** -0.5)
            x_ref[pl.ds(r0, c), hs] = t
        x_ref[pl.ds(r0, c), 2 * GROUP_W:width] = acc[:, 2 * GROUP_W:width]
        g = gp_ref[0:1, :] * _softplus(sm_ref[0, pl.ds(r0, c), :] + gp_ref[1:2, :])
        gt = g.T
        for d in (0, 1):
            cumc_ref[d, pl.ds(r0, c), :] = _dot_exact_lhs(tril if d == 0 else triu, g)
            cumr_ref[d, pl.ds(r0, c), :] = _dot_exact_rhs(gt, triu if d == 0 else tril)
        return _

    lax.fori_loop(0, n_chunks, conv_chunk, None, unroll=2)

    chains = [(d, h) for d in (0, 1) for h in range(N_HEADS)]
    n_state = N_HEADS * HEAD_DIM

    def step(i, _, second_half):
        rows = [pl.ds(pl.multiple_of(n * c, c), c) for n in (i, n_chunks - 1 - i)]
        cums = [cumc_ref[d, rows[d], :] for d in (0, 1)]
        cum_rows = [cumr_ref[d, rows[d], :] for d in (0, 1)]
        betas = [_sigmoid(sm_ref[0, rows[d], :]) for d in (0, 1)]
        incl = (lower, upper)
        strict = (ri > ci, ri < ci)
        last = (c - 1, 0)

        kfs, vbs, kbs, e_cs, tots, decays, kos, qfs = [], [], [], [], [], [], [], []
        for d, h in chains:
            la, lb = SM_A[d] + h, SM_B[d] + h
            cum_c = cums[d][:, la:la + 1]
            cum_r = cum_rows[d][la:la + 1, :]
            tot = cums[d][last[d]:last[d] + 1, la:la + 1]
            beta_c = betas[d][:, lb:lb + 1]
            kf = x_ref[rows[d], GROUP_W + h * HEAD_DIM:GROUP_W + (h + 1) * HEAD_DIM]
            vf = x_ref[rows[d], 2 * GROUP_W + h * HEAD_DIM:2 * GROUP_W + (h + 1) * HEAD_DIM]
            e_c = jnp.exp(cum_c)
            kfs.append(kf)
            kbs.append(kf * beta_c)
            vbs.append(vf * beta_c)
            e_cs.append(e_c)
            tots.append(tot)
            decays.append(jnp.where(incl[d], jnp.exp(jnp.minimum(cum_c - cum_r, 0.0)), 0.0))
            kos.append(kf * jnp.exp(tot - cum_c))
            if with_out:
                qfs.append(x_ref[rows[d], h * HEAD_DIM:(h + 1) * HEAD_DIM])

        if with_out:
            prods = [_dot_nt(jnp.concatenate([kb, qf], axis=0), kf) for kb, qf, kf in zip(kbs, qfs, kfs)]
            attns = [p[c:, :] * dec for p, dec in zip(prods, decays)]
        else:
            prods = [_dot_nt(kb, kf) for kb, kf in zip(kbs, kfs)]
        lows = [jnp.where(strict[d], p[:c, :] * dec, 0.0) for (d, _), p, dec in zip(chains, prods, decays)]
        ts = _unit_triangular_inverses(lows, ri, ci)
        uws = [_dot(t, jnp.concatenate([vb, kb * e_c], axis=1)) for t, vb, kb, e_c in zip(ts, vbs, kbs, e_cs)]
        sts = [st_ref[j * HEAD_DIM:(j + 1) * HEAD_DIM, :] for j in range(len(chains))]
        if with_out:
            reads = [_dot(jnp.concatenate([uw[:, HEAD_DIM:], qf * e_c], axis=0), st)
                     for uw, qf, e_c, st in zip(uws, qfs, e_cs, sts)]
        else:
            reads = [_dot(uw[:, HEAD_DIM:], st) for uw, st in zip(uws, sts)]
        v_news = [uw[:, :HEAD_DIM] - r[:c, :] for uw, r in zip(uws, reads)]
        if with_out:
            writes = [_dot(jnp.concatenate([a, ko.T], axis=0), vn) for a, ko, vn in zip(attns, kos, v_news)]
        else:
            writes = [_dot_tn(ko, vn) for ko, vn in zip(kos, v_news)]
        for j, (d, h) in enumerate(chains):
            if with_out:
                hs = slice(h * HEAD_DIM, (h + 1) * HEAD_DIM)
                o = reads[j][c:, :] + writes[j][:c, :]
                if second_half:
                    o = o + o_ref[1 - d, rows[d], hs]
                    o = o * lax.rsqrt(jnp.mean(o * o, axis=-1, keepdims=True) + EPS) * nw_ref[...]
                    y_ref[0, rows[d], hs] = (o * _silu(gate_ref[0, rows[d], hs].astype(F32))).astype(BF16)
                else:
                    o_ref[d, rows[d], hs] = o
                upd = writes[j][c:, :]
            else:
                upd = writes[j]
            st_ref[j * HEAD_DIM:(j + 1) * HEAD_DIM, :] = sts[j] * jnp.exp(tots[j]) + upd
        return _

    assert n_chunks % 2 == 0
    half = n_chunks // 2
    st_ref[0:n_state, :] = s0f_ref[0]
    st_ref[n_state:2 * n_state, :] = s0b_ref[0]
    lax.fori_loop(0, half, functools.partial(step, second_half=False), None, unroll=min(2, half))
    lax.fori_loop(half, n_chunks, functools.partial(step, second_half=True), None, unroll=min(2, half))
    stf_ref[0] = st_ref[0:n_state, :]
    stb_ref[0] = st_ref[n_state:2 * n_state, :]


def _gdn(big, small, cw, gp, nw, s0f, s0b, *, with_out):
    b, l, _ = big.shape
    st_shape = jax.ShapeDtypeStruct((b, N_HEADS * HEAD_DIM, HEAD_DIM), F32)
    st_spec = pl.BlockSpec((1, N_HEADS * HEAD_DIM, HEAD_DIM), lambda bi: (bi, 0, 0))
    in_specs = [pl.BlockSpec((1, l, 3 * GROUP_W), lambda bi: (bi, 0, 1)),
                pl.BlockSpec((1, l, GROUP_W), lambda bi: (bi, 0, 6)),
                pl.BlockSpec((1, l, SMALL_W), lambda bi: (bi, 0, 0)),
                pl.BlockSpec((GDN_CONV, 3 * GROUP_W), lambda bi: (0, 0)),
                pl.BlockSpec((8, SMALL_W), lambda bi: (0, 0)),
                pl.BlockSpec((1, HEAD_DIM), lambda bi: (0, 0)),
                st_spec, st_spec]
    out_shape = [st_shape, st_shape]
    out_specs = [st_spec, st_spec]
    scratch = [pltpu.VMEM((l, 3 * GROUP_W), F32), pltpu.VMEM((2 * N_HEADS * HEAD_DIM, HEAD_DIM), F32)]
    if with_out:
        out_shape = [jax.ShapeDtypeStruct((b, l, GROUP_W), BF16)] + out_shape
        out_specs = [pl.BlockSpec((1, l, GROUP_W), lambda bi: (bi, 0, 0))] + out_specs
        scratch = [scratch[0], pltpu.VMEM((2, l, GROUP_W), F32), scratch[1]]
    scratch += [pltpu.VMEM((2, l, SMALL_W), F32), pltpu.VMEM((2, l, CHUNK), F32)]
    res = pl.pallas_call(
        functools.partial(_gdn_kernel, seq=l, with_out=with_out),
        out_shape=tuple(out_shape), grid=(b,), in_specs=in_specs, out_specs=tuple(out_specs),
        scratch_shapes=scratch, compiler_params=_cparams(("parallel",)),
        name="gdn_out" if with_out else "gdn_state",
    )(big, big, small, cw, gp, nw.reshape(1, -1), s0f, s0b)
    return res if with_out else (None,) + tuple(res)


def _gdn_gate_params(a_log, dt_bias):
    gp = jnp.zeros((8, SMALL_W), F32)
    gp = gp.at[0, SM_A[0]:SM_A[0] + 2 * N_HEADS].set(-jnp.exp(a_log.astype(F32)).reshape(-1))
    gp = gp.at[1, SM_A[0]:SM_A[0] + 2 * N_HEADS].set(dt_bias.astype(F32).reshape(-1))
    return gp


def _rope(x, cos, sin, lane):
    swapped = jnp.where((lane % 64) < 32, pltpu.roll(x, 96, 1), pltpu.roll(x, 32, 1))
    return x * cos + swapped * sin


def _softmax_sink_pv(s, sink_col, vv):
    m = jnp.maximum(jnp.max(s, axis=-1, keepdims=True), sink_col)
    p = jnp.exp(s - m)
    den = jnp.sum(p, axis=-1, keepdims=True) + jnp.exp(sink_col - m)
    return _dot(p, vv) * (1.0 / den)


def _swa_kernel(*refs, seq, ctx_len, with_ctx):
    if with_ctx:
        (q_ref, k_ref, v_ref, gate_ref, kc_ref, vc_ref, cos_ref, sin_ref, sink_ref, qc_ref, gatec_ref,
         y_ref, yc_ref, kr_ref, bias_ref) = refs
    else:
        (q_ref, k_ref, v_ref, gate_ref, kc_ref, vc_ref, cos_ref, sin_ref, sink_ref,
         y_ref, kr_ref, bias_ref) = refs
    blk = SWA_BLOCK
    nb = seq // blk
    win = 3 * blk
    grp = N_HEADS // SWA_KV_HEADS
    scale = HEAD_DIM ** -0.5
    lane = _iota2((blk, HEAD_DIM), 1)

    def rope_k(n, _):
        rows = pl.ds(pl.multiple_of(n * blk, blk), blk)
        cos = cos_ref[rows, :]
        sin = sin_ref[rows, :]
        for g in range(SWA_KV_HEADS):
            hs = slice(g * HEAD_DIM, (g + 1) * HEAD_DIM)
            kr_ref[rows, hs] = _rope(k_ref[0, rows, hs].astype(F32), cos, sin, lane).astype(BF16)
        return _

    lax.fori_loop(0, nb, rope_k, None)

    rowq = _iota2((grp * blk, win + ctx_len), 0) % blk
    colk = _iota2((grp * blk, win + ctx_len), 1)
    is_ctx = colk >= win
    for w in range(3):
        visible = is_ctx | (jnp.abs(rowq + w * blk - colk) <= SWA_WINDOW)
        bias_ref[w] = jnp.where(visible, 0.0, NEG_BIG)

    def sink_column(g, n_rows):
        hr = _iota2((grp * n_rows, 1), 0) // n_rows
        col = jnp.zeros((grp * n_rows, 1), F32)
        for j in range(grp):
            col = jnp.where(hr == j, sink_ref[0:1, g * grp + j:g * grp + j + 1], col)
        return col

    def q_block(n, _):
        r0 = pl.multiple_of(n * blk, blk)
        rows = pl.ds(r0, blk)
        k0 = pl.multiple_of(jnp.clip(r0 - blk, 0, seq - win), blk)
        cos = cos_ref[rows, :]
        sin = sin_ref[rows, :]
        bias = bias_ref[(r0 - k0) // blk]
        for g in range(SWA_KV_HEADS):
            hs = slice(g * HEAD_DIM, (g + 1) * HEAD_DIM)
            qg = jnp.concatenate(
                [_rope(q_ref[0, rows, (g * grp + j) * HEAD_DIM:(g * grp + j + 1) * HEAD_DIM].astype(F32),
                       cos, sin, lane) * scale for j in range(grp)], axis=0).astype(BF16)
            kk = jnp.concatenate([kr_ref[pl.ds(k0, win), hs], kc_ref[0, :, hs]], axis=0)
            vv = jnp.concatenate([v_ref[0, pl.ds(k0, win), hs], vc_ref[0, :, hs]], axis=0)
            s = _dot_nt(qg, kk) + bias
            o = _softmax_sink_pv(s, sink_column(g, blk), vv)
            for j in range(grp):
                cs = slice((g * grp + j) * HEAD_DIM, (g * grp + j + 1) * HEAD_DIM)
                y_ref[0, rows, cs] = (o[j * blk:(j + 1) * blk, :]
                                      * _silu(gate_ref[0, rows, cs].astype(F32))).astype(BF16)
        return _

    lax.fori_loop(0, nb, q_block, None, unroll=2)

    if with_ctx:
        for g in range(SWA_KV_HEADS):
            hs = slice(g * HEAD_DIM, (g + 1) * HEAD_DIM)
            qg = jnp.concatenate([qc_ref[0, :, (g * grp + j) * HEAD_DIM:(g * grp + j + 1) * HEAD_DIM]
                                  for j in range(grp)], axis=0)
            s = _dot_nt(qg, kc_ref[0, :, hs]) * scale
            o = _softmax_sink_pv(s, sink_column(g, ctx_len), vc_ref[0, :, hs])
            for j in range(grp):
                cs = slice((g * grp + j) * HEAD_DIM, (g * grp + j + 1) * HEAD_DIM)
                yc_ref[0, :, cs] = (o[j * ctx_len:(j + 1) * ctx_len, :]
                                    * _silu(gatec_ref[0, :, cs].astype(F32))).astype(BF16)


def _swa(big, bigc, cos, sin, sink, *, with_ctx):
    b, l, _ = big.shape
    lc = bigc.shape[1]
    in_specs = [pl.BlockSpec((1, l, 512), lambda bi: (bi, 0, 11)),
                pl.BlockSpec((1, l, 256), lambda bi: (bi, 0, 24)),
                pl.BlockSpec((1, l, 256), lambda bi: (bi, 0, 25)),
                pl.BlockSpec((1, l, 512), lambda bi: (bi, 0, 13)),
                pl.BlockSpec((1, lc, 256), lambda bi: (bi, 0, 24)),
                pl.BlockSpec((1, lc, 256), lambda bi: (bi, 0, 25)),
                pl.BlockSpec((l, HEAD_DIM), lambda bi: (0, 0)),
                pl.BlockSpec((l, HEAD_DIM), lambda bi: (0, 0)),
                pl.BlockSpec((1, LANES), lambda bi: (0, 0))]
    args = [big, big, big, big, bigc, bigc, cos, sin, sink]
    out_shape = [jax.ShapeDtypeStruct((b, l, GROUP_W), BF16)]
    out_specs = [pl.BlockSpec((1, l, GROUP_W), lambda bi: (bi, 0, 0))]
    if with_ctx:
        in_specs += [pl.BlockSpec((1, lc, 512), lambda bi: (bi, 0, 11)),
                     pl.BlockSpec((1, lc, 512), lambda bi: (bi, 0, 13))]
        args += [bigc, bigc]
        out_shape.append(jax.ShapeDtypeStruct((b, lc, GROUP_W), BF16))
        out_specs.append(pl.BlockSpec((1, lc, GROUP_W), lambda bi: (bi, 0, 0)))
    res = pl.pallas_call(
        functools.partial(_swa_kernel, seq=l, ctx_len=lc, with_ctx=with_ctx),
        out_shape=tuple(out_shape), grid=(b,), in_specs=in_specs, out_specs=tuple(out_specs),
        scratch_shapes=[pltpu.VMEM((l, SWA_KV_HEADS * HEAD_DIM), BF16),
                        pltpu.VMEM((3, (N_HEADS // SWA_KV_HEADS) * SWA_BLOCK, 3 * SWA_BLOCK + lc), F32)],
        compiler_params=_cparams(("parallel",)),
        name="swa_ctx" if with_ctx else "swa",
    )(*args)
    return (res[0], res[1]) if with_ctx else (res[0], None)


def _rope_tables(seq_len):
    rows = seq_len // GRID_W
    row = jnp.repeat(jnp.arange(rows, dtype=F32), GRID_W)
    col = jnp.tile(jnp.arange(GRID_W, dtype=F32), rows)
    axis_dim = HEAD_DIM // 2
    inv_freq = ROPE_BASE ** (-jnp.arange(0, axis_dim, 2, dtype=F32) / axis_dim)
    ang_r = row[:, None] * inv_freq
    ang_c = col[:, None] * inv_freq
    cos = jnp.concatenate([jnp.cos(ang_r), jnp.cos(ang_r), jnp.cos(ang_c), jnp.cos(ang_c)], axis=-1)
    sin = jnp.concatenate([-jnp.sin(ang_r), jnp.sin(ang_r), -jnp.sin(ang_c), jnp.sin(ang_c)], axis=-1)
    return cos, sin


_PACK_RUNS = ((0, 0, 1024), (1024, 1056, 2048), (3072, 3120, 4096))
_LR_COL0 = 1024
_GDN_GATE_COL0 = 3104


def _pack_kernel(wt_ref, big_ref, small_ref):
    step = 512
    for dst, src, width in _PACK_RUNS:
        for o in range(0, width, step):
            big_ref[0, :, dst + o:dst + o + step] = wt_ref[0, src + o:src + o + step, :].T.astype(BF16)
    assert _LR_COL0 % LANES == 0 and _GDN_GATE_COL0 % LANES == SM_A[0]
    gate_win0 = _GDN_GATE_COL0 - SM_A[0]
    lane = _iota2((wt_ref.shape[2], LANES), 1)
    lr_win = wt_ref[0, _LR_COL0:_LR_COL0 + LANES, :].T
    gate_win = wt_ref[0, gate_win0:gate_win0 + LANES, :].T
    small = jnp.where(lane < SM_A[0], lr_win, jnp.where(lane < SM_B[1] + N_HEADS, gate_win, 0.0))
    small_ref[0] = small.astype(BF16)


def _pack_w_in(w_in):
    n_layers, d, n = w_in.shape
    kb = PACK_KB
    return pl.pallas_call(
        _pack_kernel,
        out_shape=(jax.ShapeDtypeStruct((n_layers, d, BIG_W), BF16),
                   jax.ShapeDtypeStruct((n_layers, d, SMALL_W), BF16)),
        grid=(n_layers, d // kb),
        in_specs=[pl.BlockSpec((1, n, kb), lambda l, i: (l, 0, i))],
        out_specs=(pl.BlockSpec((1, kb, BIG_W), lambda l, i: (l, i, 0)),
                   pl.BlockSpec((1, kb, SMALL_W), lambda l, i: (l, i, 0))),
        compiler_params=_cparams(("parallel", "parallel")),
        name="pack_w_in",
    )(jnp.swapaxes(w_in, 1, 2))


def kernel(x, c, ctx, c_ctx, ada_w, ada_b, norm_pre, norm_post, w_in, w_out, gla_w_decay, gla_b_decay,
           gla_norm, gdn_conv, gdn_a_log, gdn_dt_bias, gdn_norm, sc_conv, swa_sink):
    b, l, d = x.shape
    cos, sin = _rope_tables(l)
    cc = jnp.zeros((8, d), F32).at[:b].set(c).at[b].set(c_ctx)
    mod = _ada(cc, ada_w, ada_b)
    h, hc = x, ctx
    lc = ctx.shape[1]
    n_layers = ada_w.shape[0]
    w_big, w_small = _pack_w_in(w_in)
    w_o = w_out.astype(BF16)
    for layer in range(n_layers):
        with_ctx = layer < n_layers - 1
        shift, scale, gate = (mod[layer, :, i * d:(i + 1) * d] for i in range(3))

        def per_batch(t):
            return t[:b, None, :], t[b][None, None, :]
        (shift_l, shift_c), (scale_l, scale_c), (gate_l, gate_c) = per_batch(shift), per_batch(scale), per_batch(gate)
        g_pre = norm_pre[layer].reshape(1, d)
        big, small = _inproj(h, scale_l, shift_l, g_pre, w_big, w_small, layer)
        bigc, smallc = _inproj(hc.reshape(1, b * lc, d), scale_c, shift_c, g_pre, w_big, w_small, layer)
        bigc, smallc = bigc.reshape(b, lc, BIG_W), smallc.reshape(b, lc, SMALL_W)

        zf = jnp.zeros((b, N_HEADS * GLA_DK, HEAD_DIM), F32)
        yca, stf, stb = _gla(bigc, smallc, gla_w_decay[layer], gla_b_decay[layer], gla_norm[layer], zf, zf,
                             with_out=with_ctx)
        ya, _, _ = _gla(big, small, gla_w_decay[layer], gla_b_decay[layer], gla_norm[layer], stf, stb,
                        with_out=True)

        gp = _gdn_gate_params(gdn_a_log[layer], gdn_dt_bias[layer])
        zg = jnp.zeros((b, N_HEADS * HEAD_DIM, HEAD_DIM), F32)
        ycb, gtf, gtb = _gdn(bigc, smallc, gdn_conv[layer], gp, gdn_norm[layer], zg, zg, with_out=with_ctx)
        yb, _, _ = _gdn(big, small, gdn_conv[layer], gp, gdn_norm[layer], gtf, gtb, with_out=True)

        sink = jnp.zeros((1, LANES), F32).at[0, :N_HEADS].set(swa_sink[layer])
        yd, ycd = _swa(big, bigc, cos, sin, sink, with_ctx=with_ctx)

        g_post = norm_post[layer].reshape(1, d)
        h_new = _outproj_sconv(ya, yb, yd, big, sc_conv[layer], w_o, layer, h, gate_l, g_post)
        if with_ctx:
            ycc = _sconv(bigc, sc_conv[layer])
            flat = [t.reshape(1, b * lc, GROUP_W) for t in (yca, ycb, ycc, ycd)]
            hc = _outproj(*flat, w_o, layer, hc.reshape(1, b * lc, d), gate_c, g_post).reshape(b, lc, d)
        h = h_new
    return h
```
